```python
import jax, jax.numpy as jnp
from jax import lax
import numpy as np

D_MODEL = 2048
BATCH = 8
SEQ = 2048
DEPTH = 2

GRID_W = 64
CTX_LEN = 256
Q_BLOCK = 128
HEAD_DIM = 128
BRANCH_W = 512
N_BRANCHES = 4
MIX_W = N_BRANCHES * BRANCH_W
A_HEADS = 4
A_KV_HEADS = 2
B_HEADS = 4
NA_KH = 8
NA_KW = 16
C_HEADS = 4
C_KV_HEADS = 2
WINDOW = 128
D_HEADS = 4
MLA_KV_RANK = 512
MLA_NOPE = 128
MLA_ROPE = 64
MLA_V = 128
MLA_QK = MLA_NOPE + MLA_ROPE
ROPE_THETA = 10000.0
EPS = 1e-6
IN_SPLITS = (
    A_HEADS * HEAD_DIM, A_KV_HEADS * HEAD_DIM, A_KV_HEADS * HEAD_DIM, BRANCH_W,
    B_HEADS * HEAD_DIM, B_HEADS * HEAD_DIM, B_HEADS * HEAD_DIM, BRANCH_W,
    C_HEADS * HEAD_DIM, C_KV_HEADS * HEAD_DIM, C_KV_HEADS * HEAD_DIM, BRANCH_W,
    D_HEADS * MLA_QK, MLA_KV_RANK, MLA_ROPE, BRANCH_W,
)
IN_COLS = sum(IN_SPLITS)

kernel_name = 'hybrid_parallel_heads_flow_block'


def rms_norm(x, g):
    xf = x.astype(jnp.float32)
    y = xf * lax.rsqrt(jnp.mean(xf * xf, axis=-1, keepdims=True) + EPS)
    return (y * g.astype(jnp.float32)).astype(x.dtype)


def heads(t, n):
    return t.reshape(t.shape[:-1] + (n, t.shape[-1] // n))


def split_cols(y):
    return jnp.split(y, np.cumsum(IN_SPLITS)[:-1].tolist(), axis=-1)


def axial_rope_tables(n_tokens, rot_dim):
    t = jnp.arange(n_tokens)
    row = (t // GRID_W).astype(jnp.float32)
    col = (t % GRID_W).astype(jnp.float32)
    n_freq = rot_dim // 4
    inv_freq = ROPE_THETA ** (-jnp.arange(n_freq, dtype=jnp.float32) / n_freq)
    ang = jnp.concatenate([row[:, None] * inv_freq, col[:, None] * inv_freq], axis=-1)
    return jnp.cos(ang), jnp.sin(ang)


def apply_rope(x, cos, sin):
    x1, x2 = jnp.split(x.astype(jnp.float32), 2, axis=-1)
    cs, sn = cos[None, :, None, :], sin[None, :, None, :]
    return jnp.concatenate([x1 * cs - x2 * sn, x1 * sn + x2 * cs], axis=-1).astype(x.dtype)


def softmax_with_sink(s, sink):
    if sink is None:
        return jax.nn.softmax(s, axis=-1)
    sk = sink.astype(jnp.float32).reshape(s.shape[1], s.shape[2], 1, 1)
    m = jnp.maximum(jnp.max(s, axis=-1, keepdims=True), sk)
    e = jnp.exp(s - m)
    return e / (jnp.sum(e, axis=-1, keepdims=True) + jnp.exp(sk - m))


def to_blocks(q):
    B, S = q.shape[:2]
    return q.reshape((B, S // Q_BLOCK, Q_BLOCK) + q.shape[2:]).swapaxes(0, 1)


def from_blocks(o):
    o = o.swapaxes(0, 1)
    return o.reshape(o.shape[0], o.shape[1] * o.shape[2], -1)


def ctx_attn(q, k, v, sink=None):
    Hq, dq = q.shape[2:]
    Hkv = k.shape[2]
    qg = q.reshape(q.shape[:2] + (Hkv, Hq // Hkv, dq))
    s = jnp.einsum('bqhgd,bkhd->bhgqk', qg, k, preferred_element_type=jnp.float32) * dq ** -0.5
    p = softmax_with_sink(s, sink).astype(v.dtype)
    o = jnp.einsum('bhgqk,bkhd->bqhgd', p, v)
    return o.reshape(o.shape[:2] + (-1,))


def global_attn_blocks(q, k, v):
    Hq, dq = q.shape[2:]
    Hkv = k.shape[2]
    qb = to_blocks(q.reshape(q.shape[:2] + (Hkv, Hq // Hkv, dq)))

    def block(qi):
        s = jnp.einsum('bqhgd,bkhd->bhgqk', qi, k, preferred_element_type=jnp.float32) * dq ** -0.5
        p = jax.nn.softmax(s, axis=-1).astype(v.dtype)
        return jnp.einsum('bhgqk,bkhd->bqhgd', p, v)

    return from_blocks(lax.map(block, qb))


def window_attn_blocks(q, k, v, kc, vc, sink):
    S = q.shape[1]
    Hq, dq = q.shape[2:]
    Hkv = k.shape[2]
    L = kc.shape[1]
    span = Q_BLOCK + 2 * WINDOW
    pad = ((0, 0), (WINDOW, WINDOW), (0, 0), (0, 0))
    kp, vp = jnp.pad(k, pad), jnp.pad(v, pad)
    qb = to_blocks(q.reshape(q.shape[:2] + (Hkv, Hq // Hkv, dq)))
    scale = dq ** -0.5

    def block(args):
        i, qi = args
        kb = lax.dynamic_slice_in_dim(kp, i * Q_BLOCK, span, axis=1)
        vb = lax.dynamic_slice_in_dim(vp, i * Q_BLOCK, span, axis=1)
        qpos = i * Q_BLOCK + jnp.arange(Q_BLOCK)
        kpos = i * Q_BLOCK - WINDOW + jnp.arange(span)
        valid = (jnp.abs(qpos[:, None] - kpos[None, :]) <= WINDOW) & (kpos >= 0)[None, :] & (kpos < S)[None, :]
        s_lat = jnp.einsum('bqhgd,bkhd->bhgqk', qi, kb, preferred_element_type=jnp.float32) * scale
        s_lat = jnp.where(valid, s_lat, -jnp.inf)
        s_ctx = jnp.einsum('bqhgd,bkhd->bhgqk', qi, kc, preferred_element_type=jnp.float32) * scale
        p = softmax_with_sink(jnp.concatenate([s_ctx, s_lat], axis=-1), sink).astype(v.dtype)
        return (jnp.einsum('bhgqk,bkhd->bqhgd', p[..., :L], vc)
                + jnp.einsum('bhgqk,bkhd->bqhgd', p[..., L:], vb))

    return from_blocks(lax.map(block, (jnp.arange(S // Q_BLOCK), qb)))


def neighbourhood_attn_blocks(q, k, v, kc, vc, rpb):
    S = q.shape[1]
    dq = q.shape[3]
    L = kc.shape[1]
    rows = S // GRID_W
    kh = min(NA_KH, rows)
    q_rows = Q_BLOCK // GRID_W
    strip_rows = min(kh + q_rows - 1, rows)
    strip = strip_rows * GRID_W
    qb = to_blocks(q)
    scale = dq ** -0.5
    q_local = jnp.arange(Q_BLOCK)
    k_local = jnp.arange(strip)

    def block(args):
        i, qi = args
        r0 = i * q_rows
        strip_start = jnp.minimum(jnp.clip(r0 - kh // 2, 0, rows - kh), rows - strip_rows)
        kb = lax.dynamic_slice_in_dim(k, strip_start * GRID_W, strip, axis=1)
        vb = lax.dynamic_slice_in_dim(v, strip_start * GRID_W, strip, axis=1)
        qr, qcol = r0 + q_local // GRID_W, q_local % GRID_W
        kr, kcol = strip_start + k_local // GRID_W, k_local % GRID_W
        rs = jnp.clip(qr - kh // 2, 0, rows - kh)
        cs = jnp.clip(qcol - NA_KW // 2, 0, GRID_W - NA_KW)
        valid = ((kr[None, :] >= rs[:, None]) & (kr[None, :] < rs[:, None] + kh)
                 & (kcol[None, :] >= cs[:, None]) & (kcol[None, :] < cs[:, None] + NA_KW))
        dr = jnp.clip(kr[None, :] - qr[:, None] + NA_KH - 1, 0, 2 * NA_KH - 2)
        dc = jnp.clip(kcol[None, :] - qcol[:, None] + NA_KW - 1, 0, 2 * NA_KW - 2)
        bias = rpb[:, dr, dc].astype(jnp.float32)
        s_lat = jnp.einsum('bqhd,bkhd->bhqk', qi, kb, preferred_element_type=jnp.float32) * scale + bias[None]
        s_lat = jnp.where(valid, s_lat, -jnp.inf)
        s_ctx = jnp.einsum('bqhd,bkhd->bhqk', qi, kc, preferred_element_type=jnp.float32) * scale
        p = jax.nn.softmax(jnp.concatenate([s_ctx, s_lat], axis=-1), axis=-1).astype(v.dtype)
        return (jnp.einsum('bhqk,bkhd->bqhd', p[..., :L], vc)
                + jnp.einsum('bhqk,bkhd->bqhd', p[..., L:], vb))

    return from_blocks(lax.map(block, (jnp.arange(S // Q_BLOCK), qb)))


def mixer_a(p, pc, q_g, k_g, cos, sin, with_ctx):
    q, k, v, g = p
    qc, kc, vc, gc = pc
    q = apply_rope(rms_norm(heads(q, A_HEADS), q_g), cos, sin)
    k = apply_rope(rms_norm(heads(k, A_KV_HEADS), k_g), cos, sin)
    kc = rms_norm(heads(kc, A_KV_HEADS), k_g)
    v, vc = heads(v, A_KV_HEADS), heads(vc, A_KV_HEADS)
    o = global_attn_blocks(q, jnp.concatenate([kc, k], axis=1), jnp.concatenate([vc, v], axis=1)) * jax.nn.silu(g)
    if not with_ctx:
        return o, None
    oc = ctx_attn(rms_norm(heads(qc, A_HEADS), q_g), kc, vc) * jax.nn.silu(gc)
    return o, oc


def mixer_b(p, pc, q_g, k_g, rpb, with_ctx):
    q, k, v, g = p
    qc, kc, vc, gc = pc
    q = rms_norm(heads(q, B_HEADS), q_g)
    k = rms_norm(heads(k, B_HEADS), k_g)
    kc = rms_norm(heads(kc, B_HEADS), k_g)
    v, vc = heads(v, B_HEADS), heads(vc, B_HEADS)
    o = neighbourhood_attn_blocks(q, k, v, kc, vc, rpb) * jax.nn.silu(g)
    if not with_ctx:
        return o, None
    oc = ctx_attn(rms_norm(heads(qc, B_HEADS), q_g), kc, vc) * jax.nn.silu(gc)
    return o, oc


def mixer_c(p, pc, q_g, k_g, sink, cos, sin, with_ctx):
    q, k, v, g = p
    qc, kc, vc, gc = pc
    q = apply_rope(rms_norm(heads(q, C_HEADS), q_g), cos, sin)
    k = apply_rope(rms_norm(heads(k, C_KV_HEADS), k_g), cos, sin)
    kc = rms_norm(heads(kc, C_KV_HEADS), k_g)
    v, vc = heads(v, C_KV_HEADS), heads(vc, C_KV_HEADS)
    o = window_attn_blocks(q, k, v, kc, vc, sink) * jax.nn.silu(g)
    if not with_ctx:
        return o, None
    oc = ctx_attn(rms_norm(heads(qc, C_HEADS), q_g), kc, vc, sink) * jax.nn.silu(gc)
    return o, oc


def mixer_d(p, pc, q_g, k_g, kv_g, w_uk, w_uv, cos, sin, with_ctx):
    def keys_values(ckv, k_rope):
        ckv = rms_norm(ckv, kv_g)
        k_nope = heads(ckv @ w_uk, D_HEADS)
        v = heads(ckv @ w_uv, D_HEADS)
        k_rope = jnp.broadcast_to(k_rope[:, :, None, :], k_nope.shape[:-1] + (MLA_ROPE,))
        return rms_norm(jnp.concatenate([k_nope, k_rope], axis=-1), k_g), v

    def rope_tail(t):
        return jnp.concatenate([t[..., :MLA_NOPE], apply_rope(t[..., MLA_NOPE:], cos, sin)], axis=-1)

    q, ckv, k_rope, g = p
    qc, ckv_c, k_rope_c, gc = pc
    q = rope_tail(rms_norm(heads(q, D_HEADS), q_g))
    k, v = keys_values(ckv, k_rope)
    k = rope_tail(k)
    kc, vc = keys_values(ckv_c, k_rope_c)
    o = global_attn_blocks(q, jnp.concatenate([kc, k], axis=1), jnp.concatenate([vc, v], axis=1)) * jax.nn.silu(g)
    if not with_ctx:
        return o, None
    oc = ctx_attn(rms_norm(heads(qc, D_HEADS), q_g), kc, vc) * jax.nn.silu(gc)
    return o, oc


def setup_inputs(seed: int = 0) -> dict:
    key = jax.random.key(seed)
    ks = jax.random.split(key, 22)
    f32 = jnp.float32

    def normal(k, shape, scale=1.0):
        return scale * jax.random.normal(k, shape, f32)

    def gain(k, shape):
        return 1.0 + 0.05 * jax.random.normal(k, shape, f32)

    D = D_MODEL
    return {
        'x': normal(ks[0], (BATCH, SEQ, D)),
        'c': normal(ks[1], (BATCH, D)),
        'ctx': normal(ks[2], (BATCH, CTX_LEN, D)),
        'c_ctx': normal(ks[3], (D,)),
        'norm_g': gain(ks[4], (DEPTH, D)),
        'w_ada': normal(ks[5], (DEPTH, D, 3 * D), 0.5 * D ** -0.5),
        'b_ada': normal(ks[6], (DEPTH, 3 * D), 0.01),
        'w_in': normal(ks[7], (DEPTH, D, IN_COLS), D ** -0.5),
        'w_out': normal(ks[8], (DEPTH, MIX_W, D), MIX_W ** -0.5),
        'a_q_g': gain(ks[9], (DEPTH, HEAD_DIM)),
        'a_k_g': gain(ks[10], (DEPTH, HEAD_DIM)),
        'b_q_g': gain(ks[11], (DEPTH, HEAD_DIM)),
        'b_k_g': gain(ks[12], (DEPTH, HEAD_DIM)),
        'b_rpb': normal(ks[13], (DEPTH, B_HEADS, 2 * NA_KH - 1, 2 * NA_KW - 1), 0.1),
        'c_q_g': gain(ks[14], (DEPTH, HEAD_DIM)),
        'c_k_g': gain(ks[15], (DEPTH, HEAD_DIM)),
        'c_sink': normal(ks[16], (DEPTH, C_HEADS), 0.5),
        'd_q_g': gain(ks[17], (DEPTH, MLA_QK)),
        'd_k_g': gain(ks[18], (DEPTH, MLA_QK)),
        'd_kv_g': gain(ks[19], (DEPTH, MLA_KV_RANK)),
        'd_w_uk': normal(ks[20], (DEPTH, MLA_KV_RANK, D_HEADS * MLA_NOPE), MLA_KV_RANK ** -0.5),
        'd_w_uv': normal(ks[21], (DEPTH, MLA_KV_RANK, D_HEADS * MLA_V), MLA_KV_RANK ** -0.5),
    }


def reference(x, c, ctx, c_ctx, norm_g, w_ada, b_ada, w_in, w_out,
              a_q_g, a_k_g, b_q_g, b_k_g, b_rpb, c_q_g, c_k_g, c_sink,
              d_q_g, d_k_g, d_kv_g, d_w_uk, d_w_uv):
    S = x.shape[1]
    cos_h, sin_h = axial_rope_tables(S, HEAD_DIM)
    cos_r, sin_r = axial_rope_tables(S, MLA_ROPE)
    hc = ctx.astype(x.dtype)
    for l in range(DEPTH):
        with_ctx = l < DEPTH - 1
        sh_x, sc_x, gt_x = jnp.split((jax.nn.silu(c) @ w_ada[l] + b_ada[l])[:, None, :], 3, axis=-1)
        sh_c, sc_c, gt_c = jnp.split(jax.nn.silu(c_ctx) @ w_ada[l] + b_ada[l], 3, axis=-1)
        h_x = rms_norm(x, norm_g[l]) * (1 + sc_x) + sh_x
        h_c = rms_norm(hc, norm_g[l]) * (1 + sc_c) + sh_c
        px = split_cols(h_x @ w_in[l])
        pc = split_cols(h_c @ w_in[l])
        oa, oa_c = mixer_a(px[0:4], pc[0:4], a_q_g[l], a_k_g[l], cos_h, sin_h, with_ctx)
        ob, ob_c = mixer_b(px[4:8], pc[4:8], b_q_g[l], b_k_g[l], b_rpb[l], with_ctx)
        oc, oc_c = mixer_c(px[8:12], pc[8:12], c_q_g[l], c_k_g[l], c_sink[l], cos_h, sin_h, with_ctx)
        od, od_c = mixer_d(px[12:16], pc[12:16], d_q_g[l], d_k_g[l], d_kv_g[l], d_w_uk[l], d_w_uv[l],
                           cos_r, sin_r, with_ctx)
        x = x + gt_x * (jnp.concatenate([oa, ob, oc, od], axis=-1) @ w_out[l])
        if with_ctx:
            hc = hc + gt_c * (jnp.concatenate([oa_c, ob_c, oc_c, od_c], axis=-1) @ w_out[l])
    return x
```

```python
import functools

import jax
import jax.numpy as jnp
from jax import lax
from jax.experimental import pallas as pl
from jax.experimental.pallas import tpu as pltpu

D_MODEL = 2048
GRID_W = 64
HEAD_DIM = 128
BRANCH_W = 512
N_HEADS = 4
NA_KH = 8
NA_KW = 16
WINDOW = 128
MLA_KV_RANK = 512
MLA_NOPE = 128
MLA_ROPE = 64
MLA_QK = MLA_NOPE + MLA_ROPE
ROPE_THETA = 10000.0
EPS = 1e-6
NEG = -1e30

LANE = 128
TQ = 256
TM_IN = 768
TN_IN = 512
N_CHUNKS = 56
NA_STRIP = 12 * GRID_W
WIN_SPAN = TQ + 2 * WINDOW
VMEM_LIMIT = 48 * 1024 * 1024

A_Q, A_K, A_V, A_G = 0, 4, 6, 8
B_Q, B_K, B_V, B_G = 12, 16, 20, 24
C_Q, C_K, C_V, C_G = 28, 32, 34, 36
D_QN, D_G, D_CKV, D_QR, D_KR = 40, 44, 48, 52, 54

_NT = (((1,), (1,)), ((), ()))


def _params(*sem):
    return pltpu.CompilerParams(dimension_semantics=sem, vmem_limit_bytes=VMEM_LIMIT)


def _silu(x):
    return x * jax.nn.sigmoid(x)


def _ada_kernel(c_ref, w_ref, b_ref, o_ref):
    a = _silu(c_ref[...]).astype(jnp.bfloat16)
    o_ref[...] = jnp.dot(a, w_ref[...].astype(jnp.bfloat16),
                         preferred_element_type=jnp.float32) + b_ref[...]


def _ada(cc, w_ada, b_ada):
    depth, d, n = w_ada.shape
    tn = 512
    return pl.pallas_call(
        _ada_kernel,
        grid=(depth, n // tn),
        in_specs=[pl.BlockSpec((16, d), lambda l, j: (0, 0)),
                  pl.BlockSpec((None, d, tn), lambda l, j: (l, 0, j)),
                  pl.BlockSpec((None, 1, tn), lambda l, j: (l, 0, j))],
        out_specs=pl.BlockSpec((None, 16, tn), lambda l, j: (l, 0, j)),
        out_shape=jax.ShapeDtypeStruct((depth, 16, n), jnp.float32),
        compiler_params=_params("arbitrary", "arbitrary"),
    )(cc, w_ada, b_ada.reshape(depth, 1, n))


def _inproj_kernel(x_ref, mod_ref, ng_ref, w_ref, o_ref, h_ref, *, ctx_len):
    b, t, j = pl.program_id(0), pl.program_id(1), pl.program_id(2)
    tm, d = x_ref.shape

    @pl.when(j == 0)
    def _():
        x = x_ref[...]
        y = x * lax.rsqrt(jnp.mean(x * x, axis=-1, keepdims=True) + EPS) * ng_ref[...]
        rows = t * tm + lax.broadcasted_iota(jnp.int32, (tm, 1), 0)
        is_ctx = rows < ctx_len
        sh = jnp.where(is_ctx, mod_ref[8:9, 0:d], mod_ref[pl.ds(b, 1), 0:d])
        sc = jnp.where(is_ctx, mod_ref[8:9, d:2 * d], mod_ref[pl.ds(b, 1), d:2 * d])
        h_ref[...] = (y * (1.0 + sc) + sh).astype(jnp.bfloat16)

    acc = jnp.dot(h_ref[...], w_ref[...], preferred_element_type=jnp.float32)
    for c in range(acc.shape[1] // LANE):
        o_ref[c] = acc[:, c * LANE:(c + 1) * LANE].astype(jnp.bfloat16)


def _inproj(stream, mod, ng, w, ctx_len):
    bsz, t, d = stream.shape
    n = w.shape[1]
    cpt = TN_IN // LANE
    return pl.pallas_call(
        functools.partial(_inproj_kernel, ctx_len=ctx_len),
        grid=(bsz, t // TM_IN, n // TN_IN),
        in_specs=[pl.BlockSpec((None, TM_IN, d), lambda b, i, j: (b, i, 0)),
                  pl.BlockSpec(mod.shape, lambda b, i, j: (0, 0)),
                  pl.BlockSpec((1, d), lambda b, i, j: (0, 0)),
                  pl.BlockSpec((d, TN_IN), lambda b, i, j: (0, j))],
        out_specs=pl.BlockSpec((None, cpt, TM_IN, LANE), lambda b, i, j: (b, j, i, 0)),
        out_shape=jax.ShapeDtypeStruct((bsz, n // LANE, t, LANE), jnp.bfloat16),
        scratch_shapes=[pltpu.VMEM((TM_IN, d), jnp.bfloat16)],
        compiler_params=_params("arbitrary", "arbitrary", "arbitrary"),
    )(stream, mod, ng, w)


def _norm_rope(x, gain, cos=None, sin=None, scale=None):
    y = x * lax.rsqrt(jnp.mean(x * x, axis=-1, keepdims=True) + EPS) * gain
    if cos is not None:
        y = y * cos + pltpu.roll(y, 64, 1) * sin
    if scale is not None:
        y = y * scale
    return y


def _chunk_spec(n, rows, chunk0, row_fn):
    return pl.BlockSpec((None, n, rows, LANE), lambda b, i: (b, chunk0 // n, row_fn(i), 0))


def _gated_store(o_ref, h, o, g_ref):
    g = g_ref[h].astype(jnp.float32)
    o_ref[:, h * LANE:(h + 1) * LANE] = (o * _silu(g)).astype(o_ref.dtype)


def _attn_a_kernel(q_ref, k_ref, v_ref, g_ref, cq_ref, sq_ref, ck_ref, sk_ref, qg_ref, kg_ref,
                   o_ref, kp_ref, *, with_ctx, ctx_len):
    i = pl.program_id(1)
    tq = q_ref.shape[1]
    scale = HEAD_DIM ** -0.5

    @pl.when(i == 0)
    def _():
        for kv in range(2):
            kp_ref[kv] = _norm_rope(k_ref[kv].astype(jnp.float32), kg_ref[...],
                                    ck_ref[...], sk_ref[...]).astype(jnp.bfloat16)

    def attend(nk):
        for kv in range(2):
            q2 = jnp.concatenate(
                [_norm_rope(q_ref[2 * kv + j].astype(jnp.float32), qg_ref[...], cq_ref[...], sq_ref[...],
                            scale).astype(jnp.bfloat16) for j in range(2)], axis=0)
            s = lax.dot_general(q2, kp_ref[kv, 0:nk, :], _NT, preferred_element_type=jnp.float32)
            p = jnp.exp(s - jnp.max(s, axis=-1, keepdims=True))
            l = jnp.sum(p, axis=-1, keepdims=True)
            o = jnp.dot(p.astype(jnp.bfloat16), v_ref[kv, 0:nk, :], preferred_element_type=jnp.float32) / l
            for j in range(2):
                _gated_store(o_ref, 2 * kv + j, o[j * tq:(j + 1) * tq], g_ref)

    if with_ctx:
        pl.when(i == 0)(lambda: attend(ctx_len))
        pl.when(i > 0)(lambda: attend(k_ref.shape[1]))
    else:
        attend(k_ref.shape[1])


def _attn_c_kernel(sink_ref, q_ref, k_ref, v_ref, g_ref, cq_ref, sq_ref, ck_ref, sk_ref, qg_ref, kg_ref,
                   o_ref, kp_ref, *, with_ctx, ctx_len):
    i = pl.program_id(1)
    tq = q_ref.shape[1]
    n_lat = k_ref.shape[1] - ctx_len
    scale = HEAD_DIM ** -0.5

    @pl.when(i == 0)
    def _():
        for kv in range(2):
            kp_ref[kv] = _norm_rope(k_ref[kv].astype(jnp.float32), kg_ref[...],
                                    ck_ref[...], sk_ref[...]).astype(jnp.bfloat16)

    def attend(latent):
        if latent:
            q0 = (i - 1 if with_ctx else i) * tq
            ks = jnp.clip(q0 - WINDOW, 0, n_lat - WIN_SPAN)
            row0 = pl.multiple_of(ctx_len + ks, LANE)
            qi = lax.broadcasted_iota(jnp.int32, (2 * tq, WIN_SPAN), 0) & (tq - 1)
            ki = lax.broadcasted_iota(jnp.int32, (2 * tq, WIN_SPAN), 1)
            dist = (qi - ki) + (q0 - ks)
            wmask = jnp.where(jnp.abs(dist) <= WINDOW, 0.0, NEG)
        head_row = lax.broadcasted_iota(jnp.int32, (2 * tq, 1), 0) < tq
        for kv in range(2):
            q2 = jnp.concatenate(
                [_norm_rope(q_ref[2 * kv + j].astype(jnp.float32), qg_ref[...], cq_ref[...], sq_ref[...],
                            scale).astype(jnp.bfloat16) for j in range(2)], axis=0)
            sink = jnp.where(head_row, sink_ref[2 * kv], sink_ref[2 * kv + 1])
            s_c = lax.dot_general(q2, kp_ref[kv, 0:ctx_len, :], _NT, preferred_element_type=jnp.float32)
            m = jnp.maximum(jnp.max(s_c, axis=-1, keepdims=True), sink)
            if latent:
                s_l = lax.dot_general(q2, kp_ref[kv, pl.ds(row0, WIN_SPAN), :], _NT,
                                      preferred_element_type=jnp.float32) + wmask
                m = jnp.maximum(m, jnp.max(s_l, axis=-1, keepdims=True))
            p_c = jnp.exp(s_c - m)
            l = jnp.sum(p_c, axis=-1, keepdims=True) + jnp.exp(sink - m)
            o = jnp.dot(p_c.astype(jnp.bfloat16), v_ref[kv, 0:ctx_len, :], preferred_element_type=jnp.float32)
            if latent:
                p_l = jnp.exp(s_l - m)
                l = l + jnp.sum(p_l, axis=-1, keepdims=True)
                o = o + jnp.dot(p_l.astype(jnp.bfloat16), v_ref[kv, pl.ds(row0, WIN_SPAN), :],
                                preferred_element_type=jnp.float32)
            o = o / l
            for j in range(2):
                _gated_store(o_ref, 2 * kv + j, o[j * tq:(j + 1) * tq], g_ref)

    if with_ctx:
        pl.when(i == 0)(lambda: attend(False))
        pl.when(i > 0)(lambda: attend(True))
    else:
        attend(True)


def _attn_gqa(proj, tabs, qg, kg, sink, chunks, with_ctx, ctx_len):
    cq0, ck0, cv0, cg0 = chunks
    bsz, _, t, _ = proj.shape
    blk0 = 0 if with_ctx else ctx_len // TQ
    nq = t // TQ - blk0
    row = lambda i: i + blk0
    cos, sin = tabs
    in_specs = [_chunk_spec(4, TQ, cq0, row), _chunk_spec(2, t, ck0, lambda i: 0),
                _chunk_spec(2, t, cv0, lambda i: 0), _chunk_spec(4, TQ, cg0, row),
                pl.BlockSpec((TQ, LANE), lambda b, i: (row(i), 0)),
                pl.BlockSpec((TQ, LANE), lambda b, i: (row(i), 0)),
                pl.BlockSpec((t, LANE), lambda b, i: (0, 0)),
                pl.BlockSpec((t, LANE), lambda b, i: (0, 0)),
                pl.BlockSpec((1, LANE), lambda b, i: (0, 0)),
                pl.BlockSpec((1, LANE), lambda b, i: (0, 0))]
    args = [proj, proj, proj, proj, cos, sin, cos, sin, qg, kg]
    if sink is None:
        body = _attn_a_kernel
    else:
        body = _attn_c_kernel
        in_specs = [pl.BlockSpec(memory_space=pltpu.SMEM)] + in_specs
        args = [sink] + args
    return pl.pallas_call(
        functools.partial(body, with_ctx=with_ctx, ctx_len=ctx_len),
        grid=(bsz, nq),
        in_specs=in_specs,
        out_specs=pl.BlockSpec((None, TQ, BRANCH_W), lambda b, i: (b, i, 0)),
        out_shape=jax.ShapeDtypeStruct((bsz, nq * TQ, BRANCH_W), jnp.bfloat16),
        scratch_shapes=[pltpu.VMEM((2, t, LANE), jnp.bfloat16)],
        compiler_params=_params("arbitrary", "arbitrary"),
    )(*args)


def _attn_b_kernel(q_ref, k_ref, v_ref, g_ref, bias_ref, qg_ref, kg_ref, o_ref, kp_ref, *, with_ctx, ctx_len):
    i = pl.program_id(1)
    tq = q_ref.shape[1]
    rows = (k_ref.shape[1] - ctx_len) // GRID_W
    strip_rows = NA_STRIP // GRID_W
    scale = HEAD_DIM ** -0.5

    @pl.when(i == 0)
    def _():
        for h in range(N_HEADS):
            kp_ref[h] = _norm_rope(k_ref[h].astype(jnp.float32), kg_ref[...]).astype(jnp.bfloat16)

    def attend(latent):
        if latent:
            r0 = (i - 1 if with_ctx else i) * (tq // GRID_W)
            ss = jnp.clip(r0 - NA_KH // 2, 0, rows - strip_rows)
            row0 = pl.multiple_of(ctx_len + ss * GRID_W, LANE)
        for h in range(N_HEADS):
            q = _norm_rope(q_ref[h].astype(jnp.float32), qg_ref[...], scale=scale).astype(jnp.bfloat16)
            s_c = lax.dot_general(q, kp_ref[h, 0:ctx_len, :], _NT, preferred_element_type=jnp.float32)
            m = jnp.max(s_c, axis=-1, keepdims=True)
            if latent:
                s_l = lax.dot_general(q, kp_ref[h, pl.ds(row0, NA_STRIP), :], _NT,
                                      preferred_element_type=jnp.float32) + bias_ref[h]
                m = jnp.maximum(m, jnp.max(s_l, axis=-1, keepdims=True))
            p_c = jnp.exp(s_c - m)
            l = jnp.sum(p_c, axis=-1, keepdims=True)
            o = jnp.dot(p_c.astype(jnp.bfloat16), v_ref[h, 0:ctx_len, :], preferred_element_type=jnp.float32)
            if latent:
                p_l = jnp.exp(s_l - m)
                l = l + jnp.sum(p_l, axis=-1, keepdims=True)
                o = o + jnp.dot(p_l.astype(jnp.bfloat16), v_ref[h, pl.ds(row0, NA_STRIP), :],
                                preferred_element_type=jnp.float32)
            _gated_store(o_ref, h, o / l, g_ref)

    if with_ctx:
        pl.when(i == 0)(lambda: attend(False))
        pl.when(i > 0)(lambda: attend(True))
    else:
        attend(True)


def _na_bias_tables(rpb, n_lat):
    import numpy as np
    rows = n_lat // GRID_W
    q_rows = TQ // GRID_W
    strip_rows = NA_STRIP // GRID_W
    n_blocks = n_lat // TQ
    out = []
    for blk in (0, 1, n_blocks - 1):
        r0 = blk * q_rows
        ss = min(max(r0 - NA_KH // 2, 0), rows - strip_rows)
        ql, kl = np.arange(TQ), np.arange(NA_STRIP)
        qr, qc = r0 + ql // GRID_W, ql % GRID_W
        kr, kc = ss + kl // GRID_W, kl % GRID_W
        rs = np.clip(qr - NA_KH // 2, 0, rows - NA_KH)
        cs = np.clip(qc - NA_KW // 2, 0, GRID_W - NA_KW)
        valid = ((kr[None] >= rs[:, None]) & (kr[None] < rs[:, None] + NA_KH)
                 & (kc[None] >= cs[:, None]) & (kc[None] < cs[:, None] + NA_KW))
        dr = np.clip(kr[None] - qr[:, None] + NA_KH - 1, 0, 2 * NA_KH - 2)
        dc = np.clip(kc[None] - qc[:, None] + NA_KW - 1, 0, 2 * NA_KW - 2)
        out.append(jnp.where(valid[None], rpb[:, dr, dc].astype(jnp.float32), NEG))
    return jnp.stack(out)


def _attn_b(proj, bias, qg, kg, with_ctx, ctx_len):
    bsz, _, t, _ = proj.shape
    blk0 = 0 if with_ctx else ctx_len // TQ
    nq = t // TQ - blk0
    n_lat_blocks = (t - ctx_len) // TQ
    row = lambda i: i + blk0

    def variant(i):
        jl = i + blk0 - ctx_len // TQ
        return jnp.where(jl <= 0, 0, jnp.where(jl == n_lat_blocks - 1, 2, 1))

    return pl.pallas_call(
        functools.partial(_attn_b_kernel, with_ctx=with_ctx, ctx_len=ctx_len),
        grid=(bsz, nq),
        in_specs=[_chunk_spec(4, TQ, B_Q, row), _chunk_spec(4, t, B_K, lambda i: 0),
                  _chunk_spec(4, t, B_V, lambda i: 0), _chunk_spec(4, TQ, B_G, row),
                  pl.BlockSpec((None, N_HEADS, TQ, NA_STRIP), lambda b, i: (variant(i), 0, 0, 0)),
                  pl.BlockSpec((1, LANE), lambda b, i: (0, 0)),
                  pl.BlockSpec((1, LANE), lambda b, i: (0, 0))],
        out_specs=pl.BlockSpec((None, TQ, BRANCH_W), lambda b, i: (b, i, 0)),
        out_shape=jax.ShapeDtypeStruct((bsz, nq * TQ, BRANCH_W), jnp.bfloat16),
        scratch_shapes=[pltpu.VMEM((N_HEADS, t, LANE), jnp.bfloat16)],
        compiler_params=_params("arbitrary", "arbitrary"),
    )(proj, proj, proj, proj, bias, qg, kg)


def _attn_d_kernel(qn_ref, qr_ref, ckv_ref, kr_ref, g_ref, cq_ref, sq_ref, ck_ref, sk_ref,
                   qgn_ref, qgr_ref, kgn_ref, kgr_ref, kvg_ref, wuk_ref, wuv_ref,
                   o_ref, kp_ref, vp_ref, *, with_ctx, ctx_len):
    i = pl.program_id(1)
    tq = qn_ref.shape[1]
    t = ckv_ref.shape[1]
    scale = MLA_QK ** -0.5
    lane_grp = (lax.broadcasted_iota(jnp.int32, (1, LANE), 1) // (MLA_ROPE // 2)) % 2

    @pl.when(i == 0)
    def _():
        rc = 768
        for r in range(0, t, rc):
            c = [ckv_ref[j, r:r + rc, :].astype(jnp.float32) for j in range(4)]
            ms = sum(jnp.sum(cj * cj, axis=-1, keepdims=True) for cj in c) / MLA_KV_RANK
            inv = lax.rsqrt(ms + EPS)
            cn = jnp.concatenate([(c[j] * inv * kvg_ref[j:j + 1, :]) for j in range(4)],
                                 axis=-1).astype(jnp.bfloat16)
            kn = jnp.dot(cn, wuk_ref[...], preferred_element_type=jnp.float32)
            vv = jnp.dot(cn, wuv_ref[...], preferred_element_type=jnp.float32)
            kr = kr_ref[0, r:r + rc, :].astype(jnp.float32)
            kr_ss = 0.5 * jnp.sum(kr * kr, axis=-1, keepdims=True)
            cos, sin = ck_ref[r:r + rc, :], sk_ref[r:r + rc, :]
            for h in range(N_HEADS):
                kh = kn[:, h * LANE:(h + 1) * LANE]
                ms_h = (jnp.sum(kh * kh, axis=-1, keepdims=True) + kr_ss) / MLA_QK
                inv_h = lax.rsqrt(ms_h + EPS)
                kt = kr * inv_h * kgr_ref[...]
                kt = kt * cos + pltpu.roll(kt, 64, 1) * sin
                kp_ref[h, r:r + rc, :] = jnp.concatenate([kh * inv_h * kgn_ref[...], kt],
                                                         axis=-1).astype(jnp.bfloat16)
                vp_ref[h, r:r + rc, :] = vv[:, h * LANE:(h + 1) * LANE].astype(jnp.bfloat16)

    def attend(nk):
        for h in range(N_HEADS):
            qn = qn_ref[h].astype(jnp.float32)
            qt = jnp.where(lane_grp == h % 2, qr_ref[h // 2].astype(jnp.float32), 0.0)
            ms = (jnp.sum(qn * qn, axis=-1, keepdims=True) + jnp.sum(qt * qt, axis=-1, keepdims=True)) / MLA_QK
            inv = lax.rsqrt(ms + EPS)
            qt = qt * inv * qgr_ref[...]
            qt = qt * cq_ref[...] + pltpu.roll(qt, 64, 1) * sq_ref[...]
            q = (jnp.concatenate([qn * inv * qgn_ref[...], qt], axis=-1) * scale).astype(jnp.bfloat16)
            s = lax.dot_general(q, kp_ref[h, 0:nk, :], _NT, preferred_element_type=jnp.float32)
            p = jnp.exp(s - jnp.max(s, axis=-1, keepdims=True))
            l = jnp.sum(p, axis=-1, keepdims=True)
            o = jnp.dot(p.astype(jnp.bfloat16), vp_ref[h, 0:nk, :], preferred_element_type=jnp.float32) / l
            _gated_store(o_ref, h, o, g_ref)

    if with_ctx:
        pl.when(i == 0)(lambda: attend(ctx_len))
        pl.when(i > 0)(lambda: attend(t))
    else:
        attend(t)


def _attn_d(proj, tabs, gains, wuk, wuv, with_ctx, ctx_len):
    bsz, _, t, _ = proj.shape
    blk0 = 0 if with_ctx else ctx_len // TQ
    nq = t // TQ - blk0
    row = lambda i: i + blk0
    cos, sin = tabs
    qgn, qgr, kgn, kgr, kvg = gains
    vec = pl.BlockSpec((1, LANE), lambda b, i: (0, 0))
    return pl.pallas_call(
        functools.partial(_attn_d_kernel, with_ctx=with_ctx, ctx_len=ctx_len),
        grid=(bsz, nq),
        in_specs=[_chunk_spec(4, TQ, D_QN, row), _chunk_spec(2, TQ, D_QR, row),
                  _chunk_spec(4, t, D_CKV, lambda i: 0), _chunk_spec(1, t, D_KR, lambda i: 0),
                  _chunk_spec(4, TQ, D_G, row),
                  pl.BlockSpec((TQ, LANE), lambda b, i: (row(i), 0)),
                  pl.BlockSpec((TQ, LANE), lambda b, i: (row(i), 0)),
                  pl.BlockSpec((t, LANE), lambda b, i: (0, 0)),
                  pl.BlockSpec((t, LANE), lambda b, i: (0, 0)),
                  vec, vec, vec, vec,
                  pl.BlockSpec((4, LANE), lambda b, i: (0, 0)),
                  pl.BlockSpec(wuk.shape, lambda b, i: (0, 0)),
                  pl.BlockSpec(wuv.shape, lambda b, i: (0, 0))],
        out_specs=pl.BlockSpec((None, TQ, BRANCH_W), lambda b, i: (b, i, 0)),
        out_shape=jax.ShapeDtypeStruct((bsz, nq * TQ, BRANCH_W), jnp.bfloat16),
        scratch_shapes=[pltpu.VMEM((N_HEADS, t, 2 * LANE), jnp.bfloat16),
                        pltpu.VMEM((N_HEADS, t, LANE), jnp.bfloat16)],
        compiler_params=_params("arbitrary", "arbitrary"),
    )(proj, proj, proj, proj, proj, cos, sin, cos, sin, qgn, qgr, kgn, kgr, kvg, wuk, wuv)


def _outproj_kernel(s_ref, oa_ref, ob_ref, oc_ref, od_ref, w_ref, mod_ref, o_ref, *, blk0, ctx_blocks):
    b, i = pl.program_id(0), pl.program_id(1)
    d = s_ref.shape[1]
    acc = jnp.dot(oa_ref[...], w_ref[0], preferred_element_type=jnp.float32)
    acc += jnp.dot(ob_ref[...], w_ref[1], preferred_element_type=jnp.float32)
    acc += jnp.dot(oc_ref[...], w_ref[2], preferred_element_type=jnp.float32)
    acc += jnp.dot(od_ref[...], w_ref[3], preferred_element_type=jnp.float32)
    mod_row = jnp.where(i + blk0 < ctx_blocks, 8, b)
    gate = mod_ref[pl.ds(mod_row, 1), 2 * d:3 * d]
    o_ref[...] = s_ref[...] + gate * acc


def _outproj(stream, outs, w, mod, with_ctx, ctx_len):
    bsz, t, d = stream.shape
    blk0 = 0 if with_ctx else ctx_len // TQ
    nq = t // TQ - blk0
    mix = pl.BlockSpec((None, TQ, BRANCH_W), lambda b, i: (b, i, 0))
    return pl.pallas_call(
        functools.partial(_outproj_kernel, blk0=blk0, ctx_blocks=ctx_len // TQ),
        grid=(bsz, nq),
        in_specs=[pl.BlockSpec((None, TQ, d), lambda b, i: (b, i + blk0, 0)),
                  mix, mix, mix, mix,
                  pl.BlockSpec(w.shape, lambda b, i: (0, 0, 0)),
                  pl.BlockSpec(mod.shape, lambda b, i: (0, 0))],
        out_specs=pl.BlockSpec((None, TQ, d), lambda b, i: (b, i, 0)),
        out_shape=jax.ShapeDtypeStruct((bsz, nq * TQ, d), jnp.float32),
        compiler_params=_params("arbitrary", "arbitrary"),
    )(stream, *outs, w, mod)


def _permute_w_in(w_in):
    w = w_in.astype(jnp.bfloat16)
    sl = lambda a, n: w[..., a:a + n]
    qd, half = 5120, MLA_ROPE // 2
    pieces = [sl(0, qd)]
    pieces += [sl(qd + MLA_QK * h, MLA_NOPE) for h in range(N_HEADS)]
    pieces += [sl(6464, BRANCH_W), sl(5888, MLA_KV_RANK)]
    for pair in range(2):
        ha, hb = qd + MLA_QK * (2 * pair) + MLA_NOPE, qd + MLA_QK * (2 * pair + 1) + MLA_NOPE
        pieces += [sl(ha, half), sl(hb, half), sl(ha + half, half), sl(hb + half, half)]
    pieces += [sl(6400, half), sl(6400, half), sl(6400 + half, half), sl(6400 + half, half)]
    pieces += [jnp.zeros(w.shape[:-1] + (LANE,), w.dtype)]
    return jnp.concatenate(pieces, axis=-1)


def _rope_tables(n_lat, ctx_len, rot_dim):
    tpos = jnp.arange(n_lat)
    row = (tpos // GRID_W).astype(jnp.float32)
    col = (tpos % GRID_W).astype(jnp.float32)
    n_freq = rot_dim // 4
    inv_freq = ROPE_THETA ** (-jnp.arange(n_freq, dtype=jnp.float32) / n_freq)
    ang = jnp.concatenate([row[:, None] * inv_freq, col[:, None] * inv_freq], axis=-1)
    cos, sin = jnp.cos(ang), jnp.sin(ang)
    rep = LANE // rot_dim
    cos_l = jnp.concatenate([cos] * (2 * rep), axis=-1)
    sin_l = jnp.concatenate([-sin] * rep + [sin] * rep, axis=-1)
    cos_l = jnp.concatenate([jnp.ones((ctx_len, LANE), jnp.float32), cos_l], axis=0)
    sin_l = jnp.concatenate([jnp.zeros((ctx_len, LANE), jnp.float32), sin_l], axis=0)
    return cos_l, sin_l


def _dup_rope_gain(g):
    half = MLA_ROPE // 2
    r1, r2 = g[MLA_NOPE:MLA_NOPE + half], g[MLA_NOPE + half:]
    return g[None, :MLA_NOPE], jnp.concatenate([r1, r1, r2, r2])[None, :]


def kernel(x, c, ctx, c_ctx, norm_g, w_ada, b_ada, w_in, w_out, a_q_g, a_k_g, b_q_g, b_k_g, b_rpb,
           c_q_g, c_k_g, c_sink, d_q_g, d_k_g, d_kv_g, d_w_uk, d_w_uv):
    bsz, n_lat, d = x.shape
    ctx_len = ctx.shape[1]
    depth = w_in.shape[0]
    assert bsz <= 8 and ctx_len == TQ and n_lat % TQ == 0 and (ctx_len + n_lat) % TM_IN == 0

    cc = jnp.zeros((16, d), jnp.float32).at[:bsz].set(c).at[8].set(c_ctx)
    mod = _ada(cc, w_ada, b_ada)
    w_in_p = _permute_w_in(w_in)
    w_out_p = w_out.astype(jnp.bfloat16).reshape(depth, 4, BRANCH_W, d)
    wuk, wuv = d_w_uk.astype(jnp.bfloat16), d_w_uv.astype(jnp.bfloat16)
    tabs_h = _rope_tables(n_lat, ctx_len, HEAD_DIM)
    tabs_r = _rope_tables(n_lat, ctx_len, MLA_ROPE)

    stream = jnp.concatenate([ctx.astype(x.dtype), x], axis=1)
    for l in range(depth):
        with_ctx = l < depth - 1
        proj = _inproj(stream, mod[l], norm_g[l][None, :], w_in_p[l], ctx_len)
        oa = _attn_gqa(proj, tabs_h, a_q_g[l][None, :], a_k_g[l][None, :], None,
                       (A_Q, A_K, A_V, A_G), with_ctx, ctx_len)
        ob = _attn_b(proj, _na_bias_tables(b_rpb[l], n_lat), b_q_g[l][None, :], b_k_g[l][None, :],
                     with_ctx, ctx_len)
        oc = _attn_gqa(proj, tabs_h, c_q_g[l][None, :], c_k_g[l][None, :], c_sink[l],
                       (C_Q, C_K, C_V, C_G), with_ctx, ctx_len)
        qgn, qgr = _dup_rope_gain(d_q_g[l])
        kgn, kgr = _dup_rope_gain(d_k_g[l])
        od = _attn_d(proj, tabs_r, (qgn, qgr, kgn, kgr, d_kv_g[l].reshape(4, LANE)), wuk[l], wuv[l],
                     with_ctx, ctx_len)
        stream = _outproj(stream, (oa, ob, oc, od), w_out_p[l], mod[l], with_ctx, ctx_len)
    return stream
```

```python
import functools

import jax
import jax.numpy as jnp
from jax import lax
from jax.experimental import pallas as pl
from jax.experimental.pallas import tpu as pltpu

D_MODEL = 2048
GRID_W = 64
HEAD_DIM = 128
BRANCH_W = 512
N_HEADS = 4
NA_KH = 8
NA_KW = 16
WINDOW = 128
MLA_KV_RANK = 512
MLA_NOPE = 128
MLA_ROPE = 64
MLA_QK = MLA_NOPE + MLA_ROPE
ROPE_THETA = 10000.0
EPS = 1e-6
NEG = -1e30

LANE = 128
TQ = 256
TM_IN = 768
TN_IN = 512
N_CHUNKS = 56
NA_STRIP = 12 * GRID_W
WIN_SPAN = TQ + 2 * WINDOW
VMEM_LIMIT = 48 * 1024 * 1024

A_Q, A_K, A_V, A_G = 0, 4, 6, 8
B_Q, B_K, B_V, B_G = 12, 16, 20, 24
C_Q, C_K, C_V, C_G = 28, 32, 34, 36
D_QN, D_G, D_CKV, D_QR, D_KR = 40, 44, 48, 52, 54

_NT = (((1,), (1,)), ((), ()))


def _params(*sem):
    return pltpu.CompilerParams(dimension_semantics=sem, vmem_limit_bytes=VMEM_LIMIT)


def _silu(x):
    return x * jax.nn.sigmoid(x)


def _ada_kernel(c_ref, w_ref, b_ref, o_ref):
    a = _silu(c_ref[...]).astype(jnp.bfloat16)
    o_ref[...] = jnp.dot(a, w_ref[...].astype(jnp.bfloat16),
                         preferred_element_type=jnp.float32) + b_ref[...]


def _ada(cc, w_ada, b_ada):
    depth, d, n = w_ada.shape
    tn = 512
    return pl.pallas_call(
        _ada_kernel,
        grid=(depth, n // tn),
        in_specs=[pl.BlockSpec((16, d), lambda l, j: (0, 0)),
                  pl.BlockSpec((None, d, tn), lambda l, j: (l, 0, j)),
                  pl.BlockSpec((None, 1, tn), lambda l, j: (l, 0, j))],
        out_specs=pl.BlockSpec((None, 16, tn), lambda l, j: (l, 0, j)),
        out_shape=jax.ShapeDtypeStruct((depth, 16, n), jnp.float32),
        compiler_params=_params("arbitrary", "arbitrary"),
    )(cc, w_ada, b_ada.reshape(depth, 1, n))


def _inproj_kernel(x_ref, mod_ref, ng_ref, w_ref, o_ref, h_ref, *, ctx_len):
    b, t, j = pl.program_id(0), pl.program_id(1), pl.program_id(2)
    tm, d = x_ref.shape

    @pl.when(j == 0)
    def _():
        x = x_ref[...]
        y = x * lax.rsqrt(jnp.mean(x * x, axis=-1, keepdims=True) + EPS) * ng_ref[...]
        rows = t * tm + lax.broadcasted_iota(jnp.int32, (tm, 1), 0)
        is_ctx = rows < ctx_len
        sh = jnp.where(is_ctx, mod_ref[8:9, 0:d], mod_ref[pl.ds(b, 1), 0:d])
        sc = jnp.where(is_ctx, mod_ref[8:9, d:2 * d], mod_ref[pl.ds(b, 1), d:2 * d])
        h_ref[...] = (y * (1.0 + sc) + sh).astype(jnp.bfloat16)

    acc = jnp.dot(h_ref[...], w_ref[...], preferred_element_type=jnp.float32)
    for c in range(acc.shape[1] // LANE):
        o_ref[c] = acc[:, c * LANE:(c + 1) * LANE].astype(jnp.bfloat16)


def _inproj(stream, mod, ng, w, ctx_len):
    bsz, t, d = stream.shape
    n = w.shape[1]
    cpt = TN_IN // LANE
    return pl.pallas_call(
        functools.partial(_inproj_kernel, ctx_len=ctx_len),
        grid=(bsz, t // TM_IN, n // TN_IN),
        in_specs=[pl.BlockSpec((None, TM_IN, d), lambda b, i, j: (b, i, 0)),
                  pl.BlockSpec(mod.shape, lambda b, i, j: (0, 0)),
                  pl.BlockSpec((1, d), lambda b, i, j: (0, 0)),
                  pl.BlockSpec((d, TN_IN), lambda b, i, j: (0, j))],
        out_specs=pl.BlockSpec((None, cpt, TM_IN, LANE), lambda b, i, j: (b, j, i, 0)),
        out_shape=jax.ShapeDtypeStruct((bsz, n // LANE, t, LANE), jnp.bfloat16),
        scratch_shapes=[pltpu.VMEM((TM_IN, d), jnp.bfloat16)],
        compiler_params=_params("arbitrary", "arbitrary", "arbitrary"),
    )(stream, mod, ng, w)


def _norm_rope(x, gain, cos=None, sin=None, scale=None):
    y = x * lax.rsqrt(jnp.mean(x * x, axis=-1, keepdims=True) + EPS) * gain
    if cos is not None:
        y = y * cos + pltpu.roll(y, 64, 1) * sin
    if scale is not None:
        y = y * scale
    return y


def _chunk_spec(n, rows, chunk0, row_fn):
    return pl.BlockSpec((None, n, rows, LANE), lambda b, i: (b, chunk0 // n, row_fn(i), 0))


def _gated_store(o_ref, h, o, g_ref):
    g = g_ref[h].astype(jnp.float32)
    o_ref[:, h * LANE:(h + 1) * LANE] = (o * _silu(g)).astype(o_ref.dtype)


def _attn_a_kernel(q_ref, k_ref, v_ref, g_ref, cq_ref, sq_ref, ck_ref, sk_ref, qg_ref, kg_ref,
                   o_ref, kp_ref, *, with_ctx, ctx_len):
    i = pl.program_id(1)
    tq = q_ref.shape[1]
    scale = HEAD_DIM ** -0.5

    @pl.when(i == 0)
    def _():
        for kv in range(2):
            kp_ref[kv] = _norm_rope(k_ref[kv].astype(jnp.float32), kg_ref[...],
                                    ck_ref[...], sk_ref[...]).astype(jnp.bfloat16)

    def attend(nk):
        for kv in range(2):
            q2 = jnp.concatenate(
                [_norm_rope(q_ref[2 * kv + j].astype(jnp.float32), qg_ref[...], cq_ref[...], sq_ref[...],
                            scale).astype(jnp.bfloat16) for j in range(2)], axis=0)
            s = lax.dot_general(q2, kp_ref[kv, 0:nk, :], _NT, preferred_element_type=jnp.float32)
            p = jnp.exp(s - jnp.max(s, axis=-1, keepdims=True))
            l = jnp.sum(p, axis=-1, keepdims=True)
            o = jnp.dot(p.astype(jnp.bfloat16), v_ref[kv, 0:nk, :], preferred_element_type=jnp.float32) / l
            for j in range(2):
                _gated_store(o_ref, 2 * kv + j, o[j * tq:(j + 1) * tq], g_ref)

    if with_ctx:
        pl.when(i == 0)(lambda: attend(ctx_len))
        pl.when(i > 0)(lambda: attend(k_ref.shape[1]))
    else:
        attend(k_ref.shape[1])


def _attn_c_kernel(sink_ref, q_ref, k_ref, v_ref, g_ref, cq_ref, sq_ref, ck_ref, sk_ref, qg_ref, kg_ref,
                   o_ref, kp_ref, *, with_ctx, ctx_len):
    i = pl.program_id(1)
    tq = q_ref.shape[1]
    n_lat = k_ref.shape[1] - ctx_len
    scale = HEAD_DIM ** -0.5

    @pl.when(i == 0)
    def _():
        for kv in range(2):
            kp_ref[kv] = _norm_rope(k_ref[kv].astype(jnp.float32), kg_ref[...],
                                    ck_ref[...], sk_ref[...]).astype(jnp.bfloat16)

    def attend(latent):
        if latent:
            q0 = (i - 1 if with_ctx else i) * tq
            ks = jnp.clip(q0 - WINDOW, 0, n_lat - WIN_SPAN)
            row0 = pl.multiple_of(ctx_len + ks, LANE)
            qi = lax.broadcasted_iota(jnp.int32, (2 * tq, WIN_SPAN), 0) & (tq - 1)
            ki = lax.broadcasted_iota(jnp.int32, (2 * tq, WIN_SPAN), 1)
            dist = (qi - ki) + (q0 - ks)
            wmask = jnp.where(jnp.abs(dist) <= WINDOW, 0.0, NEG)
        head_row = lax.broadcasted_iota(jnp.int32, (2 * tq, 1), 0) < tq
        for kv in range(2):
            q2 = jnp.concatenate(
                [_norm_rope(q_ref[2 * kv + j].astype(jnp.float32), qg_ref[...], cq_ref[...], sq_ref[...],
                            scale).astype(jnp.bfloat16) for j in range(2)], axis=0)
            sink = jnp.where(head_row, sink_ref[2 * kv], sink_ref[2 * kv + 1])
            s_c = lax.dot_general(q2, kp_ref[kv, 0:ctx_len, :], _NT, preferred_element_type=jnp.float32)
            m = jnp.maximum(jnp.max(s_c, axis=-1, keepdims=True), sink)
            if latent:
                s_l = lax.dot_general(q2, kp_ref[kv, pl.ds(row0, WIN_SPAN), :], _NT,
                                      preferred_element_type=jnp.float32) + wmask
                m = jnp.maximum(m, jnp.max(s_l, axis=-1, keepdims=True))
            p_c = jnp.exp(s_c - m)
            l = jnp.sum(p_c, axis=-1, keepdims=True) + jnp.exp(sink - m)
            o = jnp.dot(p_c.astype(jnp.bfloat16), v_ref[kv, 0:ctx_len, :], preferred_element_type=jnp.float32)
            if latent:
                p_l = jnp.exp(s_l - m)
                l = l + jnp.sum(p_l, axis=-1, keepdims=True)
                o = o + jnp.dot(p_l.astype(jnp.bfloat16), v_ref[kv, pl.ds(row0, WIN_SPAN), :],
                                preferred_element_type=jnp.float32)
            o = o / l
            for j in range(2):
                _gated_store(o_ref, 2 * kv + j, o[j * tq:(j + 1) * tq], g_ref)

    if with_ctx:
        pl.when(i == 0)(lambda: attend(False))
        pl.when(i > 0)(lambda: attend(True))
    else:
        attend(True)


def _attn_gqa(proj, tabs, qg, kg, sink, chunks, with_ctx, ctx_len):
    cq0, ck0, cv0, cg0 = chunks
    bsz, _, t, _ = proj.shape
    blk0 = 0 if with_ctx else ctx_len // TQ
    nq = t // TQ - blk0
    row = lambda i: i + blk0
    cos, sin = tabs
    in_specs = [_chunk_spec(4, TQ, cq0, row), _chunk_spec(2, t, ck0, lambda i: 0),
                _chunk_spec(2, t, cv0, lambda i: 0), _chunk_spec(4, TQ, cg0, row),
                pl.BlockSpec((TQ, LANE), lambda b, i: (row(i), 0)),
                pl.BlockSpec((TQ, LANE), lambda b, i: (row(i), 0)),
                pl.BlockSpec((t, LANE), lambda b, i: (0, 0)),
                pl.BlockSpec((t, LANE), lambda b, i: (0, 0)),
                pl.BlockSpec((1, LANE), lambda b, i: (0, 0)),
                pl.BlockSpec((1, LANE), lambda b, i: (0, 0))]
    args = [proj, proj, proj, proj, cos, sin, cos, sin, qg, kg]
    if sink is None:
        body = _attn_a_kernel
    else:
        body = _attn_c_kernel
        in_specs = [pl.BlockSpec(memory_space=pltpu.SMEM)] + in_specs
        args = [sink] + args
    return pl.pallas_call(
        functools.partial(body, with_ctx=with_ctx, ctx_len=ctx_len),
        grid=(bsz, nq),
        in_specs=in_specs,
        out_specs=pl.BlockSpec((None, TQ, BRANCH_W), lambda b, i: (b, i, 0)),
        out_shape=jax.ShapeDtypeStruct((bsz, nq * TQ, BRANCH_W), jnp.bfloat16),
        scratch_shapes=[pltpu.VMEM((2, t, LANE), jnp.bfloat16)],
        compiler_params=_params("arbitrary", "arbitrary"),
    )(*args)


def _attn_b_kernel(q_ref, k_ref, v_ref, g_ref, bias_ref, qg_ref, kg_ref, o_ref, kp_ref, *, with_ctx, ctx_len):
    i = pl.program_id(1)
    tq = q_ref.shape[1]
    rows = (k_ref.shape[1] - ctx_len) // GRID_W
    strip_rows = NA_STRIP // GRID_W
    scale = HEAD_DIM ** -0.5

    @pl.when(i == 0)
    def _():
        for h in range(N_HEADS):
            kp_ref[h] = _norm_rope(k_ref[h].astype(jnp.float32), kg_ref[...]).astype(jnp.bfloat16)

    def attend(latent):
        if latent:
            r0 = (i - 1 if with_ctx else i) * (tq // GRID_W)
            ss = jnp.clip(r0 - NA_KH // 2, 0, rows - strip_rows)
            row0 = pl.multiple_of(ctx_len + ss * GRID_W, LANE)
        for h in range(N_HEADS):
            q = _norm_rope(q_ref[h].astype(jnp.float32), qg_ref[...], scale=scale).astype(jnp.bfloat16)
            s_c = lax.dot_general(q, kp_ref[h, 0:ctx_len, :], _NT, preferred_element_type=jnp.float32)
            m = jnp.max(s_c, axis=-1, keepdims=True)
            if latent:
                s_l = lax.dot_general(q, kp_ref[h, pl.ds(row0, NA_STRIP), :], _NT,
                                      preferred_element_type=jnp.float32) + bias_ref[h]
                m = jnp.maximum(m, jnp.max(s_l, axis=-1, keepdims=True))
            p_c = jnp.exp(s_c - m)
            l = jnp.sum(p_c, axis=-1, keepdims=True)
            o = jnp.dot(p_c.astype(jnp.bfloat16), v_ref[h, 0:ctx_len, :], preferred_element_type=jnp.float32)
            if latent:
                p_l = jnp.exp(s_l - m)
                l = l + jnp.sum(p_l, axis=-1, keepdims=True)
                o = o + jnp.dot(p_l.astype(jnp.bfloat16), v_ref[h, pl.ds(row0, NA_STRIP), :],
                                preferred_element_type=jnp.float32)
            _gated_store(o_ref, h, o / l, g_ref)

    if with_ctx:
        pl.when(i == 0)(lambda: attend(False))
        pl.when(i > 0)(lambda: attend(True))
    else:
        attend(True)


def _na_bias_kernel(rp_ref, o_ref, *, rows):
    q_rows, strip_rows = TQ // GRID_W, NA_STRIP // GRID_W
    qc = lax.broadcasted_iota(jnp.int32, (GRID_W, LANE), 0)
    lane = lax.broadcasted_iota(jnp.int32, (GRID_W, LANE), 1)
    kc = lane & (GRID_W - 1)
    cs = jnp.clip(qc - NA_KW // 2, 0, GRID_W - NA_KW)
    col_ok = (kc >= cs) & (kc < cs + NA_KW)
    second = lane >= GRID_W
    for var, r0 in enumerate((0, q_rows, rows - q_rows)):
        ss = min(max(r0 - NA_KH // 2, 0), rows - strip_rows)
        for j in range(q_rows):
            qr = r0 + j
            rs = min(max(qr - NA_KH // 2, 0), rows - NA_KH)
            for p in range(strip_rows // 2):
                kr0 = ss + 2 * p
                ok0, ok1 = rs <= kr0 < rs + NA_KH, rs <= kr0 + 1 < rs + NA_KH
                if ok0 or ok1:
                    e = kr0 - qr + NA_KH - 1
                    x = jnp.broadcast_to(rp_ref[e + 1:e + 2, :], (GRID_W, LANE))
                    band = pltpu.roll(x, LANE - (NA_KW - 1), 1, stride=1, stride_axis=0)
                    row_ok = second if (ok1 and not ok0) else (~second if (ok0 and not ok1) else None)
                    valid = col_ok if row_ok is None else (col_ok & row_ok)
                    tile = jnp.where(valid, band, NEG)
                else:
                    tile = jnp.full((GRID_W, LANE), NEG, jnp.float32)
                o_ref[var, j * GRID_W:(j + 1) * GRID_W, p * LANE:(p + 1) * LANE] = tile


def _na_bias_tables(rpb, n_lat):
    depth, nh, nr, nc = rpb.shape
    z = jnp.zeros((depth, nh, nr + 2, GRID_W), jnp.float32).at[:, :, 1:nr + 1, :nc].set(rpb)
    rp = jnp.concatenate([z[:, :, :-1], z[:, :, 1:]], axis=-1)
    return pl.pallas_call(
        functools.partial(_na_bias_kernel, rows=n_lat // GRID_W),
        grid=(depth, nh),
        in_specs=[pl.BlockSpec((None, None, nr + 1, LANE), lambda l, h: (l, h, 0, 0))],
        out_specs=pl.BlockSpec((None, 3, None, TQ, NA_STRIP), lambda l, h: (l, 0, h, 0, 0)),
        out_shape=jax.ShapeDtypeStruct((depth, 3, nh, TQ, NA_STRIP), jnp.float32),
        compiler_params=_params("arbitrary", "arbitrary"),
    )(rp)


def _attn_b(proj, bias, qg, kg, with_ctx, ctx_len):
    bsz, _, t, _ = proj.shape
    blk0 = 0 if with_ctx else ctx_len // TQ
    nq = t // TQ - blk0
    n_lat_blocks = (t - ctx_len) // TQ
    row = lambda i: i + blk0

    def variant(i):
        jl = i + blk0 - ctx_len // TQ
        return jnp.where(jl <= 0, 0, jnp.where(jl == n_lat_blocks - 1, 2, 1))

    return pl.pallas_call(
        functools.partial(_attn_b_kernel, with_ctx=with_ctx, ctx_len=ctx_len),
        grid=(bsz, nq),
        in_specs=[_chunk_spec(4, TQ, B_Q, row), _chunk_spec(4, t, B_K, lambda i: 0),
                  _chunk_spec(4, t, B_V, lambda i: 0), _chunk_spec(4, TQ, B_G, row),
                  pl.BlockSpec((None, N_HEADS, TQ, NA_STRIP), lambda b, i: (variant(i), 0, 0, 0)),
                  pl.BlockSpec((1, LANE), lambda b, i: (0, 0)),
                  pl.BlockSpec((1, LANE), lambda b, i: (0, 0))],
        out_specs=pl.BlockSpec((None, TQ, BRANCH_W), lambda b, i: (b, i, 0)),
        out_shape=jax.ShapeDtypeStruct((bsz, nq * TQ, BRANCH_W), jnp.bfloat16),
        scratch_shapes=[pltpu.VMEM((N_HEADS, t, LANE), jnp.bfloat16)],
        compiler_params=_params("arbitrary", "arbitrary"),
    )(proj, proj, proj, proj, bias, qg, kg)


def _attn_d_kernel(qn_ref, qr_ref, ckv_ref, kr_ref, g_ref, cq_ref, sq_ref, ck_ref, sk_ref,
                   qgn_ref, qgr_ref, kgn_ref, kgr_ref, kvg_ref, wuk_ref, wuv_ref,
                   o_ref, kp_ref, vp_ref, *, with_ctx, ctx_len):
    i = pl.program_id(1)
    tq = qn_ref.shape[1]
    t = ckv_ref.shape[1]
    scale = MLA_QK ** -0.5
    lane_grp = (lax.broadcasted_iota(jnp.int32, (1, LANE), 1) // (MLA_ROPE // 2)) % 2

    @pl.when(i == 0)
    def _():
        rc = 768
        for r in range(0, t, rc):
            c = [ckv_ref[j, r:r + rc, :].astype(jnp.float32) for j in range(4)]
            ms = sum(jnp.sum(cj * cj, axis=-1, keepdims=True) for cj in c) / MLA_KV_RANK
            inv = lax.rsqrt(ms + EPS)
            cn = jnp.concatenate([(c[j] * inv * kvg_ref[j:j + 1, :]) for j in range(4)],
                                 axis=-1).astype(jnp.bfloat16)
            kn = jnp.dot(cn, wuk_ref[...], preferred_element_type=jnp.float32)
            vv = jnp.dot(cn, wuv_ref[...], preferred_element_type=jnp.float32)
            kr = kr_ref[0, r:r + rc, :].astype(jnp.float32)
            kr_ss = 0.5 * jnp.sum(kr * kr, axis=-1, keepdims=True)
            cos, sin = ck_ref[r:r + rc, :], sk_ref[r:r + rc, :]
            for h in range(N_HEADS):
                kh = kn[:, h * LANE:(h + 1) * LANE]
                ms_h = (jnp.sum(kh * kh, axis=-1, keepdims=True) + kr_ss) / MLA_QK
                inv_h = lax.rsqrt(ms_h + EPS)
                kt = kr * inv_h * kgr_ref[...]
                kt = kt * cos + pltpu.roll(kt, 64, 1) * sin
                kp_ref[h, r:r + rc, :] = jnp.concatenate([kh * inv_h * kgn_ref[...], kt],
                                                         axis=-1).astype(jnp.bfloat16)
                vp_ref[h, r:r + rc, :] = vv[:, h * LANE:(h + 1) * LANE].astype(jnp.bfloat16)

    def attend(nk):
        for h in range(N_HEADS):
            qn = qn_ref[h].astype(jnp.float32)
            qt = jnp.where(lane_grp == h % 2, qr_ref[h // 2].astype(jnp.float32), 0.0)
            ms = (jnp.sum(qn * qn, axis=-1, keepdims=True) + jnp.sum(qt * qt, axis=-1, keepdims=True)) / MLA_QK
            inv = lax.rsqrt(ms + EPS)
            qt = qt * inv * qgr_ref[...]
            qt = qt * cq_ref[...] + pltpu.roll(qt, 64, 1) * sq_ref[...]
            q = (jnp.concatenate([qn * inv * qgn_ref[...], qt], axis=-1) * scale).astype(jnp.bfloat16)
            s = lax.dot_general(q, kp_ref[h, 0:nk, :], _NT, preferred_element_type=jnp.float32)
            p = jnp.exp(s - jnp.max(s, axis=-1, keepdims=True))
            l = jnp.sum(p, axis=-1, keepdims=True)
            o = jnp.dot(p.astype(jnp.bfloat16), vp_ref[h, 0:nk, :], preferred_element_type=jnp.float32) / l
            _gated_store(o_ref, h, o, g_ref)

    if with_ctx:
        pl.when(i == 0)(lambda: attend(ctx_len))
        pl.when(i > 0)(lambda: attend(t))
    else:
        attend(t)


def _attn_d(proj, tabs, gains, wuk, wuv, with_ctx, ctx_len):
    bsz, _, t, _ = proj.shape
    blk0 = 0 if with_ctx else ctx_len // TQ
    nq = t // TQ - blk0
    row = lambda i: i + blk0
    cos, sin = tabs
    qgn, qgr, kgn, kgr, kvg = gains
    vec = pl.BlockSpec((1, LANE), lambda b, i: (0, 0))
    return pl.pallas_call(
        functools.partial(_attn_d_kernel, with_ctx=with_ctx, ctx_len=ctx_len),
        grid=(bsz, nq),
        in_specs=[_chunk_spec(4, TQ, D_QN, row), _chunk_spec(2, TQ, D_QR, row),
                  _chunk_spec(4, t, D_CKV, lambda i: 0), _chunk_spec(1, t, D_KR, lambda i: 0),
                  _chunk_spec(4, TQ, D_G, row),
                  pl.BlockSpec((TQ, LANE), lambda b, i: (row(i), 0)),
                  pl.BlockSpec((TQ, LANE), lambda b, i: (row(i), 0)),
                  pl.BlockSpec((t, LANE), lambda b, i: (0, 0)),
                  pl.BlockSpec((t, LANE), lambda b, i: (0, 0)),
                  vec, vec, vec, vec,
                  pl.BlockSpec((4, LANE), lambda b, i: (0, 0)),
                  pl.BlockSpec(wuk.shape, lambda b, i: (0, 0)),
                  pl.BlockSpec(wuv.shape, lambda b, i: (0, 0))],
        out_specs=pl.BlockSpec((None, TQ, BRANCH_W), lambda b, i: (b, i, 0)),
        out_shape=jax.ShapeDtypeStruct((bsz, nq * TQ, BRANCH_W), jnp.bfloat16),
        scratch_shapes=[pltpu.VMEM((N_HEADS, t, 2 * LANE), jnp.bfloat16),
                        pltpu.VMEM((N_HEADS, t, LANE), jnp.bfloat16)],
        compiler_params=_params("arbitrary", "arbitrary"),
    )(proj, proj, proj, proj, proj, cos, sin, cos, sin, qgn, qgr, kgn, kgr, kvg, wuk, wuv)


def _outproj_kernel(s_ref, oa_ref, ob_ref, oc_ref, od_ref, w_ref, mod_ref, o_ref, *, blk0, ctx_blocks):
    b, i = pl.program_id(0), pl.program_id(1)
    d = s_ref.shape[1]
    acc = jnp.dot(oa_ref[...], w_ref[0], preferred_element_type=jnp.float32)
    acc += jnp.dot(ob_ref[...], w_ref[1], preferred_element_type=jnp.float32)
    acc += jnp.dot(oc_ref[...], w_ref[2], preferred_element_type=jnp.float32)
    acc += jnp.dot(od_ref[...], w_ref[3], preferred_element_type=jnp.float32)
    mod_row = jnp.where(i + blk0 < ctx_blocks, 8, b)
    gate = mod_ref[pl.ds(mod_row, 1), 2 * d:3 * d]
    o_ref[...] = s_ref[...] + gate * acc


def _outproj(stream, outs, w, mod, with_ctx, ctx_len):
    bsz, t, d = stream.shape
    blk0 = 0 if with_ctx else ctx_len // TQ
    nq = t // TQ - blk0
    mix = pl.BlockSpec((None, TQ, BRANCH_W), lambda b, i: (b, i, 0))
    return pl.pallas_call(
        functools.partial(_outproj_kernel, blk0=blk0, ctx_blocks=ctx_len // TQ),
        grid=(bsz, nq),
        in_specs=[pl.BlockSpec((None, TQ, d), lambda b, i: (b, i + blk0, 0)),
                  mix, mix, mix, mix,
                  pl.BlockSpec(w.shape, lambda b, i: (0, 0, 0)),
                  pl.BlockSpec(mod.shape, lambda b, i: (0, 0))],
        out_specs=pl.BlockSpec((None, TQ, d), lambda b, i: (b, i, 0)),
        out_shape=jax.ShapeDtypeStruct((bsz, nq * TQ, d), jnp.float32),
        compiler_params=_params("arbitrary", "arbitrary"),
    )(stream, *outs, w, mod)


def _permute_w_in(w_in):
    w = w_in.astype(jnp.bfloat16)
    sl = lambda a, n: w[..., a:a + n]
    qd, half = 5120, MLA_ROPE // 2
    pieces = [sl(0, qd)]
    pieces += [sl(qd + MLA_QK * h, MLA_NOPE) for h in range(N_HEADS)]
    pieces += [sl(6464, BRANCH_W), sl(5888, MLA_KV_RANK)]
    for pair in range(2):
        ha, hb = qd + MLA_QK * (2 * pair) + MLA_NOPE, qd + MLA_QK * (2 * pair + 1) + MLA_NOPE
        pieces += [sl(ha, half), sl(hb, half), sl(ha + half, half), sl(hb + half, half)]
    pieces += [sl(6400, half), sl(6400, half), sl(6400 + half, half), sl(6400 + half, half)]
    pieces += [jnp.zeros(w.shape[:-1] + (LANE,), w.dtype)]
    return jnp.concatenate(pieces, axis=-1)


def _rope_tables(n_lat, ctx_len, rot_dim):
    tpos = jnp.arange(n_lat)
    row = (tpos // GRID_W).astype(jnp.float32)
    col = (tpos % GRID_W).astype(jnp.float32)
    n_freq = rot_dim // 4
    inv_freq = ROPE_THETA ** (-jnp.arange(n_freq, dtype=jnp.float32) / n_freq)
    ang = jnp.concatenate([row[:, None] * inv_freq, col[:, None] * inv_freq], axis=-1)
    cos, sin = jnp.cos(ang), jnp.sin(ang)
    rep = LANE // rot_dim
    cos_l = jnp.concatenate([cos] * (2 * rep), axis=-1)
    sin_l = jnp.concatenate([-sin] * rep + [sin] * rep, axis=-1)
    cos_l = jnp.concatenate([jnp.ones((ctx_len, LANE), jnp.float32), cos_l], axis=0)
    sin_l = jnp.concatenate([jnp.zeros((ctx_len, LANE), jnp.float32), sin_l], axis=0)
    return cos_l, sin_l


def _dup_rope_gain(g):
    half = MLA_ROPE // 2
    r1, r2 = g[MLA_NOPE:MLA_NOPE + half], g[MLA_NOPE + half:]
    return g[None, :MLA_NOPE], jnp.concatenate([r1, r1, r2, r2])[None, :]


def kernel(x, c, ctx, c_ctx, norm_g, w_ada, b_ada, w_in, w_out, a_q_g, a_k_g, b_q_g, b_k_g, b_rpb,
           c_q_g, c_k_g, c_sink, d_q_g, d_k_g, d_kv_g, d_w_uk, d_w_uv):
    bsz, n_lat, d = x.shape
    ctx_len = ctx.shape[1]
    depth = w_in.shape[0]
    assert bsz <= 8 and ctx_len == TQ and n_lat % TQ == 0 and (ctx_len + n_lat) % TM_IN == 0

    cc = jnp.zeros((16, d), jnp.float32).at[:bsz].set(c).at[8].set(c_ctx)
    mod = _ada(cc, w_ada, b_ada)
    w_in_p = _permute_w_in(w_in)
    w_out_p = w_out.astype(jnp.bfloat16).reshape(depth, 4, BRANCH_W, d)
    wuk, wuv = d_w_uk.astype(jnp.bfloat16), d_w_uv.astype(jnp.bfloat16)
    tabs_h = _rope_tables(n_lat, ctx_len, HEAD_DIM)
    tabs_r = _rope_tables(n_lat, ctx_len, MLA_ROPE)
    na_bias = _na_bias_tables(b_rpb, n_lat)

    stream = jnp.concatenate([ctx.astype(x.dtype), x], axis=1)
    for l in range(depth):
        with_ctx = l < depth - 1
        proj = _inproj(stream, mod[l], norm_g[l][None, :], w_in_p[l], ctx_len)
        oa = _attn_gqa(proj, tabs_h, a_q_g[l][None, :], a_k_g[l][None, :], None,
                       (A_Q, A_K, A_V, A_G), with_ctx, ctx_len)
        ob = _attn_b(proj, na_bias[l], b_q_g[l][None, :], b_k_g[l][None, :], with_ctx, ctx_len)
        oc = _attn_gqa(proj, tabs_h, c_q_g[l][None, :], c_k_g[l][None, :], c_sink[l],
                       (C_Q, C_K, C_V, C_G), with_ctx, ctx_len)
        qgn, qgr = _dup_rope_gain(d_q_g[l])
        kgn, kgr = _dup_rope_gain(d_k_g[l])
        od = _attn_d(proj, tabs_r, (qgn, qgr, kgn, kgr, d_kv_g[l].reshape(4, LANE)), wuk[l], wuv[l],
                     with_ctx, ctx_len)
        stream = _outproj(stream, (oa, ob, oc, od), w_out_p[l], mod[l], with_ctx, ctx_len)
    return stream
```

```python
import functools

import jax
import jax.numpy as jnp
from jax import lax
from jax.experimental import pallas as pl
from jax.experimental.pallas import tpu as pltpu

D_MODEL = 2048
GRID_W = 64
HEAD_DIM = 128
BRANCH_W = 512
N_HEADS = 4
NA_KH = 8
NA_KW = 16
WINDOW = 128
MLA_KV_RANK = 512
MLA_NOPE = 128
MLA_ROPE = 64
MLA_QK = MLA_NOPE + MLA_ROPE
ROPE_THETA = 10000.0
EPS = 1e-6
NEG = -1e30
LOG2E = 1.4426950408889634

LANE = 128
TQ = 256
TM_IN = 1152
TN_IN = 1024
N_CHUNKS = 56
NA_STRIP = 12 * GRID_W
WIN_SPAN = TQ + 2 * WINDOW
VMEM_LIMIT = 48 * 1024 * 1024

A_Q, A_K, A_V, A_G = 0, 4, 6, 8
B_Q, B_K, B_V, B_G = 12, 16, 20, 24
C_Q, C_K, C_V, C_G = 28, 32, 34, 36
D_QN, D_G, D_CKV, D_QR, D_KR = 40, 44, 48, 52, 54

_NT = (((1,), (1,)), ((), ()))


def _params(*sem):
    return pltpu.CompilerParams(dimension_semantics=sem, vmem_limit_bytes=VMEM_LIMIT)


def _silu(x):
    return x * jax.nn.sigmoid(x)


def _ada_kernel(c_ref, w_ref, b_ref, o_ref):
    a = _silu(c_ref[...]).astype(jnp.bfloat16)
    o_ref[...] = jnp.dot(a, w_ref[...].astype(jnp.bfloat16),
                         preferred_element_type=jnp.float32) + b_ref[...]


def _ada(cc, w_ada, b_ada):
    depth, d, n = w_ada.shape
    tn = 512
    return pl.pallas_call(
        _ada_kernel,
        grid=(depth, n // tn),
        in_specs=[pl.BlockSpec((16, d), lambda l, j: (0, 0)),
                  pl.BlockSpec((None, d, tn), lambda l, j: (l, 0, j)),
                  pl.BlockSpec((None, 1, tn), lambda l, j: (l, 0, j))],
        out_specs=pl.BlockSpec((None, 16, tn), lambda l, j: (l, 0, j)),
        out_shape=jax.ShapeDtypeStruct((depth, 16, n), jnp.float32),
        name="ada",
        compiler_params=_params("arbitrary", "arbitrary"),
    )(cc, w_ada, b_ada.reshape(depth, 1, n))


def _inproj_kernel(x_ref, mod_ref, ng_ref, w_ref, o_ref, h_ref, *, ctx_len):
    b, t, j = pl.program_id(0), pl.program_id(1), pl.program_id(2)
    tm, d = x_ref.shape

    @pl.when(j == 0)
    def _():
        for lo, hi, mrow in ((0, ctx_len, jnp.where(t == 0, 8, b)), (ctx_len, tm, b)):
            x = x_ref[lo:hi, :]
            sh = mod_ref[pl.ds(mrow, 1), 0:d]
            gain = ng_ref[...] * (1.0 + mod_ref[pl.ds(mrow, 1), d:2 * d])
            inv = lax.rsqrt(jnp.mean(x * x, axis=-1, keepdims=True) + EPS)
            h_ref[lo:hi, :] = (x * inv * gain + sh).astype(jnp.bfloat16)

    acc = jnp.dot(h_ref[...], w_ref[...], preferred_element_type=jnp.float32)
    for c in range(acc.shape[1] // LANE):
        o_ref[c] = acc[:, c * LANE:(c + 1) * LANE].astype(jnp.bfloat16)


def _inproj(stream, mod, ng, w, ctx_len):
    bsz, t, d = stream.shape
    n = w.shape[1]
    cpt = TN_IN // LANE
    return pl.pallas_call(
        functools.partial(_inproj_kernel, ctx_len=ctx_len),
        grid=(bsz, t // TM_IN, n // TN_IN),
        in_specs=[pl.BlockSpec((None, TM_IN, d), lambda b, i, j: (b, i, 0)),
                  pl.BlockSpec(mod.shape, lambda b, i, j: (0, 0)),
                  pl.BlockSpec((1, d), lambda b, i, j: (0, 0)),
                  pl.BlockSpec((d, TN_IN), lambda b, i, j: (0, j))],
        out_specs=pl.BlockSpec((None, cpt, TM_IN, LANE), lambda b, i, j: (b, j, i, 0)),
        out_shape=jax.ShapeDtypeStruct((bsz, n // LANE, t, LANE), jnp.bfloat16),
        scratch_shapes=[pltpu.VMEM((TM_IN, d), jnp.bfloat16)],
        name="inproj",
        compiler_params=_params("arbitrary", "arbitrary", "arbitrary"),
    )(stream, mod, ng, w)


def _norm_rope(x, gain, cos=None, sin=None, scale=None):
    y = x * lax.rsqrt(jnp.mean(x * x, axis=-1, keepdims=True) + EPS) * gain
    if cos is not None:
        y = y * cos + pltpu.roll(y, 64, 1) * sin
    if scale is not None:
        y = y * scale
    return y


def _chunk_spec(n, rows, chunk0, row_fn):
    return pl.BlockSpec((None, n, rows, LANE), lambda b, i: (b, chunk0 // n, row_fn(i), 0))


def _gated_store(o_ref, h, o, g_ref):
    g = g_ref[h].astype(jnp.float32)
    o_ref[:, h * LANE:(h + 1) * LANE] = (o * _silu(g)).astype(o_ref.dtype)


def _store_values(vp_ref, h, v):
    vp_ref[h, :, 0:LANE] = v
    lane = lax.broadcasted_iota(jnp.int32, v.shape, 1)
    vp_ref[h, :, LANE:2 * LANE] = jnp.where(lane == 0, 1.0, 0.0).astype(vp_ref.dtype)


def _softmax_pv(q, pieces, m=None):
    acc = None
    for k, vp, bias in pieces:
        s = lax.dot_general(q, k, _NT, preferred_element_type=jnp.float32)
        if bias is not None:
            s = s + bias
        mc = jnp.max(s, axis=-1, keepdims=True)
        mn = mc if m is None else jnp.maximum(m, mc)
        p = jnp.exp2(s - mn).astype(jnp.bfloat16)
        t = jnp.dot(p, vp, preferred_element_type=jnp.float32)
        acc = t if acc is None else acc * jnp.exp2(m - mn) + t
        m = mn
    return acc, m


def _normalise(acc, extra=None):
    l = acc[:, LANE:LANE + 1]
    if extra is not None:
        l = l + extra
    return acc[:, 0:LANE] / l


def _key_spans(t):
    cut = (t // 2 + 255) // 256 * 256
    return [(0, cut), (cut, t)]


def _attn_a_kernel(q_ref, k_ref, v_ref, g_ref, cq_ref, sq_ref, ck_ref, sk_ref, qg_ref, kg_ref,
                   o_ref, kp_ref, vp_ref, *, with_ctx, ctx_len):
    i = pl.program_id(1)
    tq = q_ref.shape[1]
    t = k_ref.shape[1]
    scale = HEAD_DIM ** -0.5 * LOG2E

    @pl.when(i == 0)
    def _():
        for kv in range(2):
            kp_ref[kv] = _norm_rope(k_ref[kv].astype(jnp.float32), kg_ref[...],
                                    ck_ref[...], sk_ref[...]).astype(jnp.bfloat16)
            _store_values(vp_ref, kv, v_ref[kv])

    def attend(spans):
        for kv in range(2):
            q2 = jnp.concatenate(
                [_norm_rope(q_ref[2 * kv + j].astype(jnp.float32), qg_ref[...], cq_ref[...], sq_ref[...],
                            scale).astype(jnp.bfloat16) for j in range(2)], axis=0)
            acc, _ = _softmax_pv(q2, [(kp_ref[kv, lo:hi, :], vp_ref[kv, lo:hi, :], None) for lo, hi in spans])
            o = _normalise(acc)
            for j in range(2):
                _gated_store(o_ref, 2 * kv + j, o[j * tq:(j + 1) * tq], g_ref)

    lat_spans = _key_spans(t)
    if with_ctx:
        pl.when(i == 0)(lambda: attend([(0, ctx_len)]))
        pl.when(i > 0)(lambda: attend(lat_spans))
    else:
        attend(lat_spans)


def _attn_c_kernel(sink_ref, q_ref, k_ref, v_ref, g_ref, cq_ref, sq_ref, ck_ref, sk_ref, qg_ref, kg_ref,
                   o_ref, kp_ref, vp_ref, *, with_ctx, ctx_len):
    i = pl.program_id(1)
    tq = q_ref.shape[1]
    n_lat = k_ref.shape[1] - ctx_len
    scale = HEAD_DIM ** -0.5 * LOG2E

    @pl.when(i == 0)
    def _():
        for kv in range(2):
            kp_ref[kv] = _norm_rope(k_ref[kv].astype(jnp.float32), kg_ref[...],
                                    ck_ref[...], sk_ref[...]).astype(jnp.bfloat16)
            _store_values(vp_ref, kv, v_ref[kv])

    def attend(latent):
        if latent:
            q0 = (i - 1 if with_ctx else i) * tq
            ks = jnp.clip(q0 - WINDOW, 0, n_lat - WIN_SPAN)
            row0 = pl.multiple_of(ctx_len + ks, LANE)
            qi = lax.broadcasted_iota(jnp.int32, (2 * tq, WIN_SPAN), 0) & (tq - 1)
            ki = lax.broadcasted_iota(jnp.int32, (2 * tq, WIN_SPAN), 1)
            dist = (qi - ki) + (q0 - ks)
            wmask = jnp.where(jnp.abs(dist) <= WINDOW, 0.0, NEG)
        head_row = lax.broadcasted_iota(jnp.int32, (2 * tq, 1), 0) < tq
        for kv in range(2):
            q2 = jnp.concatenate(
                [_norm_rope(q_ref[2 * kv + j].astype(jnp.float32), qg_ref[...], cq_ref[...], sq_ref[...],
                            scale).astype(jnp.bfloat16) for j in range(2)], axis=0)
            sink = jnp.where(head_row, sink_ref[2 * kv], sink_ref[2 * kv + 1]) * LOG2E
            pieces = [(kp_ref[kv, 0:ctx_len, :], vp_ref[kv, 0:ctx_len, :], None)]
            if latent:
                pieces.append((kp_ref[kv, pl.ds(row0, WIN_SPAN), :], vp_ref[kv, pl.ds(row0, WIN_SPAN), :], wmask))
            acc, m = _softmax_pv(q2, pieces, m=sink)
            o = _normalise(acc, jnp.exp2(sink - m))
            for j in range(2):
                _gated_store(o_ref, 2 * kv + j, o[j * tq:(j + 1) * tq], g_ref)

    if with_ctx:
        pl.when(i == 0)(lambda: attend(False))
        pl.when(i > 0)(lambda: attend(True))
    else:
        attend(True)


def _attn_gqa(proj, tabs, qg, kg, sink, chunks, with_ctx, ctx_len):
    cq0, ck0, cv0, cg0 = chunks
    bsz, _, t, _ = proj.shape
    blk0 = 0 if with_ctx else ctx_len // TQ
    nq = t // TQ - blk0
    row = lambda i: i + blk0
    cos, sin = tabs
    in_specs = [_chunk_spec(4, TQ, cq0, row), _chunk_spec(2, t, ck0, lambda i: 0),
                _chunk_spec(2, t, cv0, lambda i: 0), _chunk_spec(4, TQ, cg0, row),
                pl.BlockSpec((TQ, LANE), lambda b, i: (row(i), 0)),
                pl.BlockSpec((TQ, LANE), lambda b, i: (row(i), 0)),
                pl.BlockSpec((t, LANE), lambda b, i: (0, 0)),
                pl.BlockSpec((t, LANE), lambda b, i: (0, 0)),
                pl.BlockSpec((1, LANE), lambda b, i: (0, 0)),
                pl.BlockSpec((1, LANE), lambda b, i: (0, 0))]
    args = [proj, proj, proj, proj, cos, sin, cos, sin, qg, kg]
    if sink is None:
        body = _attn_a_kernel
    else:
        body = _attn_c_kernel
        in_specs = [pl.BlockSpec(memory_space=pltpu.SMEM)] + in_specs
        args = [sink] + args
    return pl.pallas_call(
        functools.partial(body, with_ctx=with_ctx, ctx_len=ctx_len),
        grid=(bsz, nq),
        in_specs=in_specs,
        out_specs=pl.BlockSpec((None, TQ, BRANCH_W), lambda b, i: (b, i, 0)),
        out_shape=jax.ShapeDtypeStruct((bsz, nq * TQ, BRANCH_W), jnp.bfloat16),
        scratch_shapes=[pltpu.VMEM((2, t, LANE), jnp.bfloat16), pltpu.VMEM((2, t, 2 * LANE), jnp.bfloat16)],
        name="attn_a" if sink is None else "attn_c",
        compiler_params=_params("arbitrary", "arbitrary"),
    )(*args)


def _attn_b_kernel(q_ref, k_ref, v_ref, g_ref, bias_ref, qg_ref, kg_ref, o_ref, kp_ref, vp_ref,
                   *, with_ctx, ctx_len):
    i = pl.program_id(1)
    tq = q_ref.shape[1]
    rows = (k_ref.shape[1] - ctx_len) // GRID_W
    strip_rows = NA_STRIP // GRID_W
    scale = HEAD_DIM ** -0.5 * LOG2E

    @pl.when(i == 0)
    def _():
        for h in range(N_HEADS):
            kp_ref[h] = _norm_rope(k_ref[h].astype(jnp.float32), kg_ref[...]).astype(jnp.bfloat16)
            _store_values(vp_ref, h, v_ref[h])

    def attend(latent):
        if latent:
            r0 = (i - 1 if with_ctx else i) * (tq // GRID_W)
            ss = jnp.clip(r0 - NA_KH // 2, 0, rows - strip_rows)
            row0 = pl.multiple_of(ctx_len + ss * GRID_W, LANE)
        for h in range(N_HEADS):
            q = _norm_rope(q_ref[h].astype(jnp.float32), qg_ref[...], scale=scale).astype(jnp.bfloat16)
            pieces = [(kp_ref[h, 0:ctx_len, :], vp_ref[h, 0:ctx_len, :], None)]
            if latent:
                pieces.append((kp_ref[h, pl.ds(row0, NA_STRIP), :], vp_ref[h, pl.ds(row0, NA_STRIP), :],
                               bias_ref[h]))
            acc, _ = _softmax_pv(q, pieces)
            _gated_store(o_ref, h, _normalise(acc), g_ref)

    if with_ctx:
        pl.when(i == 0)(lambda: attend(False))
        pl.when(i > 0)(lambda: attend(True))
    else:
        attend(True)


def _na_bias_kernel(rp_ref, o_ref, *, rows):
    q_rows, strip_rows = TQ // GRID_W, NA_STRIP // GRID_W
    qc = lax.broadcasted_iota(jnp.int32, (GRID_W, LANE), 0)
    lane = lax.broadcasted_iota(jnp.int32, (GRID_W, LANE), 1)
    kc = lane & (GRID_W - 1)
    cs = jnp.clip(qc - NA_KW // 2, 0, GRID_W - NA_KW)
    col_ok = (kc >= cs) & (kc < cs + NA_KW)
    second = lane >= GRID_W
    for var, r0 in enumerate((0, q_rows, rows - q_rows)):
        ss = min(max(r0 - NA_KH // 2, 0), rows - strip_rows)
        for j in range(q_rows):
            qr = r0 + j
            rs = min(max(qr - NA_KH // 2, 0), rows - NA_KH)
            for p in range(strip_rows // 2):
                kr0 = ss + 2 * p
                ok0, ok1 = rs <= kr0 < rs + NA_KH, rs <= kr0 + 1 < rs + NA_KH
                if ok0 or ok1:
                    e = kr0 - qr + NA_KH - 1
                    x = jnp.broadcast_to(rp_ref[e + 1:e + 2, :], (GRID_W, LANE))
                    band = pltpu.roll(x, LANE - (NA_KW - 1), 1, stride=1, stride_axis=0)
                    row_ok = second if (ok1 and not ok0) else (~second if (ok0 and not ok1) else None)
                    valid = col_ok if row_ok is None else (col_ok & row_ok)
                    tile = jnp.where(valid, band * LOG2E, NEG)
                else:
                    tile = jnp.full((GRID_W, LANE), NEG, jnp.float32)
                o_ref[var, j * GRID_W:(j + 1) * GRID_W, p * LANE:(p + 1) * LANE] = tile


def _na_bias_tables(rpb, n_lat):
    depth, nh, nr, nc = rpb.shape
    z = jnp.zeros((depth, nh, nr + 2, GRID_W), jnp.float32).at[:, :, 1:nr + 1, :nc].set(rpb)
    rp = jnp.concatenate([z[:, :, :-1], z[:, :, 1:]], axis=-1)
    return pl.pallas_call(
        functools.partial(_na_bias_kernel, rows=n_lat // GRID_W),
        grid=(depth, nh),
        in_specs=[pl.BlockSpec((None, None, nr + 1, LANE), lambda l, h: (l, h, 0, 0))],
        out_specs=pl.BlockSpec((None, 3, None, TQ, NA_STRIP), lambda l, h: (l, 0, h, 0, 0)),
        out_shape=jax.ShapeDtypeStruct((depth, 3, nh, TQ, NA_STRIP), jnp.float32),
        name="na_bias",
        compiler_params=_params("arbitrary", "arbitrary"),
    )(rp)


def _attn_b(proj, bias, qg, kg, with_ctx, ctx_len):
    bsz, _, t, _ = proj.shape
    blk0 = 0 if with_ctx else ctx_len // TQ
    nq = t // TQ - blk0
    n_lat_blocks = (t - ctx_len) // TQ
    row = lambda i: i + blk0

    def variant(i):
        jl = i + blk0 - ctx_len // TQ
        return jnp.where(jl <= 0, 0, jnp.where(jl == n_lat_blocks - 1, 2, 1))

    return pl.pallas_call(
        functools.partial(_attn_b_kernel, with_ctx=with_ctx, ctx_len=ctx_len),
        grid=(bsz, nq),
        in_specs=[_chunk_spec(4, TQ, B_Q, row), _chunk_spec(4, t, B_K, lambda i: 0),
                  _chunk_spec(4, t, B_V, lambda i: 0), _chunk_spec(4, TQ, B_G, row),
                  pl.BlockSpec((None, N_HEADS, TQ, NA_STRIP), lambda b, i: (variant(i), 0, 0, 0)),
                  pl.BlockSpec((1, LANE), lambda b, i: (0, 0)),
                  pl.BlockSpec((1, LANE), lambda b, i: (0, 0))],
        out_specs=pl.BlockSpec((None, TQ, BRANCH_W), lambda b, i: (b, i, 0)),
        out_shape=jax.ShapeDtypeStruct((bsz, nq * TQ, BRANCH_W), jnp.bfloat16),
        scratch_shapes=[pltpu.VMEM((N_HEADS, t, LANE), jnp.bfloat16),
                        pltpu.VMEM((N_HEADS, t, 2 * LANE), jnp.bfloat16)],
        name="attn_b",
        compiler_params=_params("arbitrary", "arbitrary"),
    )(proj, proj, proj, proj, bias, qg, kg)


def _attn_d_kernel(qn_ref, qr_ref, ckv_ref, kr_ref, g_ref, cq_ref, sq_ref, ck_ref, sk_ref,
                   qgn_ref, qgr_ref, kgn_ref, kgr_ref, kvg_ref, wuk_ref, wuv_ref,
                   o_ref, kp_ref, vp_ref, *, with_ctx, ctx_len):
    i = pl.program_id(1)
    t = ckv_ref.shape[1]
    scale = MLA_QK ** -0.5 * LOG2E
    lane_grp = (lax.broadcasted_iota(jnp.int32, (1, LANE), 1) // (MLA_ROPE // 2)) % 2

    @pl.when(i == 0)
    def _():
        rc = 768
        ones_col = jnp.where(lax.broadcasted_iota(jnp.int32, (rc, LANE), 1) == 0, 1.0, 0.0).astype(jnp.bfloat16)
        for r in range(0, t, rc):
            c = [ckv_ref[j, r:r + rc, :].astype(jnp.float32) for j in range(4)]
            ms = sum(jnp.sum(cj * cj, axis=-1, keepdims=True) for cj in c) / MLA_KV_RANK
            inv = lax.rsqrt(ms + EPS)
            cn = jnp.concatenate([(c[j] * inv * kvg_ref[j:j + 1, :]) for j in range(4)],
                                 axis=-1).astype(jnp.bfloat16)
            kn = jnp.dot(cn, wuk_ref[...], preferred_element_type=jnp.float32)
            vv = jnp.dot(cn, wuv_ref[...], preferred_element_type=jnp.float32)
            kr = kr_ref[0, r:r + rc, :].astype(jnp.float32)
            kr_ss = 0.5 * jnp.sum(kr * kr, axis=-1, keepdims=True)
            cos, sin = ck_ref[r:r + rc, :], sk_ref[r:r + rc, :]
            for h in range(N_HEADS):
                kh = kn[:, h * LANE:(h + 1) * LANE]
                ms_h = (jnp.sum(kh * kh, axis=-1, keepdims=True) + kr_ss) / MLA_QK
                inv_h = lax.rsqrt(ms_h + EPS)
                kt = kr * inv_h * kgr_ref[...]
                kt = kt * cos + pltpu.roll(kt, 64, 1) * sin
                kp_ref[h, r:r + rc, :] = jnp.concatenate([kh * inv_h * kgn_ref[...], kt],
                                                         axis=-1).astype(jnp.bfloat16)
                vp_ref[h, r:r + rc, 0:LANE] = vv[:, h * LANE:(h + 1) * LANE].astype(jnp.bfloat16)
                vp_ref[h, r:r + rc, LANE:2 * LANE] = ones_col

    def attend(spans):
        for h in range(N_HEADS):
            qn = qn_ref[h].astype(jnp.float32)
            qt = jnp.where(lane_grp == h % 2, qr_ref[h // 2].astype(jnp.float32), 0.0)
            ms = (jnp.sum(qn * qn, axis=-1, keepdims=True) + jnp.sum(qt * qt, axis=-1, keepdims=True)) / MLA_QK
            inv = lax.rsqrt(ms + EPS)
            qt = qt * inv * qgr_ref[...]
            qt = qt * cq_ref[...] + pltpu.roll(qt, 64, 1) * sq_ref[...]
            q = (jnp.concatenate([qn * inv * qgn_ref[...], qt], axis=-1) * scale).astype(jnp.bfloat16)
            acc, _ = _softmax_pv(q, [(kp_ref[h, lo:hi, :], vp_ref[h, lo:hi, :], None) for lo, hi in spans])
            _gated_store(o_ref, h, _normalise(acc), g_ref)

    lat_spans = _key_spans(t)
    if with_ctx:
        pl.when(i == 0)(lambda: attend([(0, ctx_len)]))
        pl.when(i > 0)(lambda: attend(lat_spans))
    else:
        attend(lat_spans)


def _attn_d(proj, tabs, gains, wuk, wuv, with_ctx, ctx_len):
    bsz, _, t, _ = proj.shape
    blk0 = 0 if with_ctx else ctx_len // TQ
    nq = t // TQ - blk0
    row = lambda i: i + blk0
    cos, sin = tabs
    qgn, qgr, kgn, kgr, kvg = gains
    vec = pl.BlockSpec((1, LANE), lambda b, i: (0, 0))
    return pl.pallas_call(
        functools.partial(_attn_d_kernel, with_ctx=with_ctx, ctx_len=ctx_len),
        grid=(bsz, nq),
        in_specs=[_chunk_spec(4, TQ, D_QN, row), _chunk_spec(2, TQ, D_QR, row),
                  _chunk_spec(4, t, D_CKV, lambda i: 0), _chunk_spec(1, t, D_KR, lambda i: 0),
                  _chunk_spec(4, TQ, D_G, row),
                  pl.BlockSpec((TQ, LANE), lambda b, i: (row(i), 0)),
                  pl.BlockSpec((TQ, LANE), lambda b, i: (row(i), 0)),
                  pl.BlockSpec((t, LANE), lambda b, i: (0, 0)),
                  pl.BlockSpec((t, LANE), lambda b, i: (0, 0)),
                  vec, vec, vec, vec,
                  pl.BlockSpec((4, LANE), lambda b, i: (0, 0)),
                  pl.BlockSpec(wuk.shape, lambda b, i: (0, 0)),
                  pl.BlockSpec(wuv.shape, lambda b, i: (0, 0))],
        out_specs=pl.BlockSpec((None, TQ, BRANCH_W), lambda b, i: (b, i, 0)),
        out_shape=jax.ShapeDtypeStruct((bsz, nq * TQ, BRANCH_W), jnp.bfloat16),
        scratch_shapes=[pltpu.VMEM((N_HEADS, t, 2 * LANE), jnp.bfloat16),
                        pltpu.VMEM((N_HEADS, t, 2 * LANE), jnp.bfloat16)],
        name="attn_d",
        compiler_params=_params("arbitrary", "arbitrary"),
    )(proj, proj, proj, proj, proj, cos, sin, cos, sin, qgn, qgr, kgn, kgr, kvg, wuk, wuv)


def _outproj_kernel(s_ref, oa_ref, ob_ref, oc_ref, od_ref, w_ref, mod_ref, o_ref, *, blk0, ctx_blocks):
    b, i = pl.program_id(0), pl.program_id(1)
    d = s_ref.shape[1]
    acc = jnp.dot(oa_ref[...], w_ref[0], preferred_element_type=jnp.float32)
    acc += jnp.dot(ob_ref[...], w_ref[1], preferred_element_type=jnp.float32)
    acc += jnp.dot(oc_ref[...], w_ref[2], preferred_element_type=jnp.float32)
    acc += jnp.dot(od_ref[...], w_ref[3], preferred_element_type=jnp.float32)
    mod_row = jnp.where(i + blk0 < ctx_blocks, 8, b)
    gate = mod_ref[pl.ds(mod_row, 1), 2 * d:3 * d]
    o_ref[...] = s_ref[...] + gate * acc


def _outproj(stream, outs, w, mod, with_ctx, ctx_len):
    bsz, t, d = stream.shape
    blk0 = 0 if with_ctx else ctx_len // TQ
    nq = t // TQ - blk0
    mix = pl.BlockSpec((None, TQ, BRANCH_W), lambda b, i: (b, i, 0))
    return pl.pallas_call(
        functools.partial(_outproj_kernel, blk0=blk0, ctx_blocks=ctx_len // TQ),
        grid=(bsz, nq),
        in_specs=[pl.BlockSpec((None, TQ, d), lambda b, i: (b, i + blk0, 0)),
                  mix, mix, mix, mix,
                  pl.BlockSpec(w.shape, lambda b, i: (0, 0, 0)),
                  pl.BlockSpec(mod.shape, lambda b, i: (0, 0))],
        out_specs=pl.BlockSpec((None, TQ, d), lambda b, i: (b, i, 0)),
        out_shape=jax.ShapeDtypeStruct((bsz, nq * TQ, d), jnp.float32),
        name="outproj",
        compiler_params=_params("arbitrary", "arbitrary"),
    )(stream, *outs, w, mod)


def _permute_w_in(w_in):
    w = w_in.astype(jnp.bfloat16)
    sl = lambda a, n: w[..., a:a + n]
    qd, half = 5120, MLA_ROPE // 2
    pieces = [sl(0, qd)]
    pieces += [sl(qd + MLA_QK * h, MLA_NOPE) for h in range(N_HEADS)]
    pieces += [sl(6464, BRANCH_W), sl(5888, MLA_KV_RANK)]
    for pair in range(2):
        ha, hb = qd + MLA_QK * (2 * pair) + MLA_NOPE, qd + MLA_QK * (2 * pair + 1) + MLA_NOPE
        pieces += [sl(ha, half), sl(hb, half), sl(ha + half, half), sl(hb + half, half)]
    pieces += [sl(6400, half), sl(6400, half), sl(6400 + half, half), sl(6400 + half, half)]
    pieces += [jnp.zeros(w.shape[:-1] + (LANE,), w.dtype)]
    return jnp.concatenate(pieces, axis=-1)


def _rope_tables(n_lat, ctx_len, rot_dim):
    tpos = jnp.arange(n_lat)
    row = (tpos // GRID_W).astype(jnp.float32)
    col = (tpos % GRID_W).astype(jnp.float32)
    n_freq = rot_dim // 4
    inv_freq = ROPE_THETA ** (-jnp.arange(n_freq, dtype=jnp.float32) / n_freq)
    ang = jnp.concatenate([row[:, None] * inv_freq, col[:, None] * inv_freq], axis=-1)
    cos, sin = jnp.cos(ang), jnp.sin(ang)
    rep = LANE // rot_dim
    cos_l = jnp.concatenate([cos] * (2 * rep), axis=-1)
    sin_l = jnp.concatenate([-sin] * rep + [sin] * rep, axis=-1)
    cos_l = jnp.concatenate([jnp.ones((ctx_len, LANE), jnp.float32), cos_l], axis=0)
    sin_l = jnp.concatenate([jnp.zeros((ctx_len, LANE), jnp.float32), sin_l], axis=0)
    return cos_l, sin_l


def _dup_rope_gain(g):
    half = MLA_ROPE // 2
    r1, r2 = g[MLA_NOPE:MLA_NOPE + half], g[MLA_NOPE + half:]
    return g[None, :MLA_NOPE], jnp.concatenate([r1, r1, r2, r2])[None, :]


def kernel(x, c, ctx, c_ctx, norm_g, w_ada, b_ada, w_in, w_out, a_q_g, a_k_g, b_q_g, b_k_g, b_rpb,
           c_q_g, c_k_g, c_sink, d_q_g, d_k_g, d_kv_g, d_w_uk, d_w_uv):
    bsz, n_lat, d = x.shape
    ctx_len = ctx.shape[1]
    depth = w_in.shape[0]
    assert bsz <= 8 and ctx_len == TQ and n_lat % TQ == 0 and (ctx_len + n_lat) % TM_IN == 0

    cc = jnp.zeros((16, d), jnp.float32).at[:bsz].set(c).at[8].set(c_ctx)
    mod = _ada(cc, w_ada, b_ada)
    w_in_p = _permute_w_in(w_in)
    w_out_p = w_out.astype(jnp.bfloat16).reshape(depth, 4, BRANCH_W, d)
    wuk, wuv = d_w_uk.astype(jnp.bfloat16), d_w_uv.astype(jnp.bfloat16)
    tabs_h = _rope_tables(n_lat, ctx_len, HEAD_DIM)
    tabs_r = _rope_tables(n_lat, ctx_len, MLA_ROPE)
    na_bias = _na_bias_tables(b_rpb, n_lat)

    stream = jnp.concatenate([ctx.astype(x.dtype), x], axis=1)
    for l in range(depth):
        with_ctx = l < depth - 1
        proj = _inproj(stream, mod[l], norm_g[l][None, :], w_in_p[l], ctx_len)
        oa = _attn_gqa(proj, tabs_h, a_q_g[l][None, :], a_k_g[l][None, :], None,
                       (A_Q, A_K, A_V, A_G), with_ctx, ctx_len)
        ob = _attn_b(proj, na_bias[l], b_q_g[l][None, :], b_k_g[l][None, :], with_ctx, ctx_len)
        oc = _attn_gqa(proj, tabs_h, c_q_g[l][None, :], c_k_g[l][None, :], c_sink[l],
                       (C_Q, C_K, C_V, C_G), with_ctx, ctx_len)
        qgn, qgr = _dup_rope_gain(d_q_g[l])
        kgn, kgr = _dup_rope_gain(d_k_g[l])
        od = _attn_d(proj, tabs_r, (qgn, qgr, kgn, kgr, d_kv_g[l].reshape(4, LANE)), wuk[l], wuv[l],
                     with_ctx, ctx_len)
        stream = _outproj(stream, (oa, ob, oc, od), w_out_p[l], mod[l], with_ctx, ctx_len)
    return stream
```

```python
import functools

import jax
import jax.numpy as jnp
from jax import lax
from jax.experimental import pallas as pl
from jax.experimental.pallas import tpu as pltpu

D_MODEL = 2048
GRID_W = 64
HEAD_DIM = 128
BRANCH_W = 512
N_HEADS = 4
NA_KH = 8
NA_KW = 16
WINDOW = 128
MLA_KV_RANK = 512
MLA_NOPE = 128
MLA_ROPE = 64
MLA_QK = MLA_NOPE + MLA_ROPE
ROPE_THETA = 10000.0
EPS = 1e-6
NEG = -1e30
LOG2E = 1.4426950408889634

LANE = 128
TQ = 256
TM_IN = 1152
TN_IN = 1024
N_CHUNKS = 56
ABC_COLS = 5120
NA_STRIP = 12 * GRID_W
WIN_SPAN = TQ + 2 * WINDOW
VMEM_LIMIT = 48 * 1024 * 1024

A_Q, A_K, A_V, A_G = 0, 4, 6, 8
B_Q, B_K, B_V, B_G = 12, 16, 20, 24
C_Q, C_K, C_V, C_G = 28, 32, 34, 36
D_QN, D_G, D_CKV, D_QR, D_KR = 40, 44, 48, 52, 54

_NT = (((1,), (1,)), ((), ()))


def _params(*sem):
    return pltpu.CompilerParams(dimension_semantics=sem, vmem_limit_bytes=VMEM_LIMIT)


def _silu(x):
    return x * jax.nn.sigmoid(x)


def _ada_kernel(c_ref, w_ref, b_ref, o_ref):
    a = _silu(c_ref[...]).astype(jnp.bfloat16)
    o_ref[...] = jnp.dot(a, w_ref[...].astype(jnp.bfloat16),
                         preferred_element_type=jnp.float32) + b_ref[...]


def _ada(cc, w_ada, b_ada):
    depth, d, n = w_ada.shape
    tn = 512
    return pl.pallas_call(
        _ada_kernel,
        grid=(depth, n // tn),
        in_specs=[pl.BlockSpec((16, d), lambda l, j: (0, 0)),
                  pl.BlockSpec((None, d, tn), lambda l, j: (l, 0, j)),
                  pl.BlockSpec((None, 1, tn), lambda l, j: (l, 0, j))],
        out_specs=pl.BlockSpec((None, 16, tn), lambda l, j: (l, 0, j)),
        out_shape=jax.ShapeDtypeStruct((depth, 16, n), jnp.float32),
        name="ada",
        compiler_params=_params("arbitrary", "arbitrary"),
    )(cc, w_ada, b_ada.reshape(depth, 1, n))


def _inproj_kernel(x_ref, mod_ref, ng_ref, w_ref, o_ref, h_ref, *, ctx_len):
    b, t, j = pl.program_id(0), pl.program_id(1), pl.program_id(2)
    tm, d = x_ref.shape

    @pl.when(j == 0)
    def _():
        for lo, hi, mrow in ((0, ctx_len, jnp.where(t == 0, 8, b)), (ctx_len, tm, b)):
            x = x_ref[lo:hi, :]
            sh = mod_ref[pl.ds(mrow, 1), 0:d]
            gain = ng_ref[...] * (1.0 + mod_ref[pl.ds(mrow, 1), d:2 * d])
            inv = lax.rsqrt(jnp.mean(x * x, axis=-1, keepdims=True) + EPS)
            h_ref[lo:hi, :] = (x * inv * gain + sh).astype(jnp.bfloat16)

    acc = jnp.dot(h_ref[...], w_ref[...], preferred_element_type=jnp.float32)
    for c in range(acc.shape[1] // LANE):
        o_ref[c] = acc[:, c * LANE:(c + 1) * LANE].astype(jnp.bfloat16)


def _inproj(stream, mod, ng, w, layer, ctx_len):
    bsz, t, d = stream.shape
    n = w.shape[2]
    cpt = TN_IN // LANE
    return pl.pallas_call(
        functools.partial(_inproj_kernel, ctx_len=ctx_len),
        grid=(bsz, t // TM_IN, n // TN_IN),
        in_specs=[pl.BlockSpec((None, TM_IN, d), lambda b, i, j: (b, i, 0)),
                  pl.BlockSpec(mod.shape, lambda b, i, j: (0, 0)),
                  pl.BlockSpec((1, d), lambda b, i, j: (0, 0)),
                  pl.BlockSpec((None, d, TN_IN), lambda b, i, j: (layer, 0, j))],
        out_specs=pl.BlockSpec((None, cpt, TM_IN, LANE), lambda b, i, j: (b, j, i, 0)),
        out_shape=jax.ShapeDtypeStruct((bsz, n // LANE, t, LANE), jnp.bfloat16),
        scratch_shapes=[pltpu.VMEM((TM_IN, d), jnp.bfloat16)],
        name="inproj",
        compiler_params=_params("arbitrary", "arbitrary", "arbitrary"),
    )(stream, mod, ng, w)


def _norm_rope(x, gain, cos=None, sin=None, scale=None):
    y = x * lax.rsqrt(jnp.mean(x * x, axis=-1, keepdims=True) + EPS) * gain
    if cos is not None:
        y = y * cos + pltpu.roll(y, 64, 1) * sin
    if scale is not None:
        y = y * scale
    return y


def _chunk_spec(n, rows, chunk0, row_fn):
    return pl.BlockSpec((None, n, rows, LANE), lambda b, i: (b, chunk0 // n, row_fn(i), 0))


def _gated_store(o_ref, h, o, g_ref):
    g = g_ref[h].astype(jnp.float32)
    o_ref[:, h * LANE:(h + 1) * LANE] = (o * _silu(g)).astype(o_ref.dtype)


def _store_values(vp_ref, h, v):
    vp_ref[h, :, 0:LANE] = v
    lane = lax.broadcasted_iota(jnp.int32, v.shape, 1)
    vp_ref[h, :, LANE:2 * LANE] = jnp.where(lane == 0, 1.0, 0.0).astype(vp_ref.dtype)


def _softmax_pv(q, pieces, m=None):
    acc = None
    for k, vp, bias in pieces:
        s = lax.dot_general(q, k, _NT, preferred_element_type=jnp.float32)
        if bias is not None:
            s = s + bias
        mc = jnp.max(s, axis=-1, keepdims=True)
        mn = mc if m is None else jnp.maximum(m, mc)
        p = jnp.exp2(s - mn).astype(jnp.bfloat16)
        t = jnp.dot(p, vp, preferred_element_type=jnp.float32)
        acc = t if acc is None else acc * jnp.exp2(m - mn) + t
        m = mn
    return acc, m


def _normalise(acc, extra=None):
    l = acc[:, LANE:LANE + 1]
    if extra is not None:
        l = l + extra
    return acc[:, 0:LANE] / l


def _key_spans(t):
    cut = (t // 2 + 255) // 256 * 256
    return [(0, cut), (cut, t)]


def _attn_a_kernel(q_ref, k_ref, v_ref, g_ref, cq_ref, sq_ref, ck_ref, sk_ref, qg_ref, kg_ref,
                   o_ref, kp_ref, vp_ref, *, with_ctx, ctx_len):
    i = pl.program_id(1)
    tq = q_ref.shape[1]
    t = k_ref.shape[1]
    scale = HEAD_DIM ** -0.5 * LOG2E

    @pl.when(i == 0)
    def _():
        for kv in range(2):
            kp_ref[kv] = _norm_rope(k_ref[kv].astype(jnp.float32), kg_ref[...],
                                    ck_ref[...], sk_ref[...]).astype(jnp.bfloat16)
            _store_values(vp_ref, kv, v_ref[kv])

    def attend(spans):
        for kv in range(2):
            q2 = jnp.concatenate(
                [_norm_rope(q_ref[2 * kv + j].astype(jnp.float32), qg_ref[...], cq_ref[...], sq_ref[...],
                            scale).astype(jnp.bfloat16) for j in range(2)], axis=0)
            acc, _ = _softmax_pv(q2, [(kp_ref[kv, lo:hi, :], vp_ref[kv, lo:hi, :], None) for lo, hi in spans])
            o = _normalise(acc)
            for j in range(2):
                _gated_store(o_ref, 2 * kv + j, o[j * tq:(j + 1) * tq], g_ref)

    lat_spans = _key_spans(t)
    if with_ctx:
        pl.when(i == 0)(lambda: attend([(0, ctx_len)]))
        pl.when(i > 0)(lambda: attend(lat_spans))
    else:
        attend(lat_spans)


def _attn_c_kernel(sink_ref, q_ref, k_ref, v_ref, g_ref, cq_ref, sq_ref, ck_ref, sk_ref, qg_ref, kg_ref,
                   o_ref, kp_ref, vp_ref, *, with_ctx, ctx_len):
    i = pl.program_id(1)
    tq = q_ref.shape[1]
    n_lat = k_ref.shape[1] - ctx_len
    scale = HEAD_DIM ** -0.5 * LOG2E

    @pl.when(i == 0)
    def _():
        for kv in range(2):
            kp_ref[kv] = _norm_rope(k_ref[kv].astype(jnp.float32), kg_ref[...],
                                    ck_ref[...], sk_ref[...]).astype(jnp.bfloat16)
            _store_values(vp_ref, kv, v_ref[kv])

    def attend(latent):
        if latent:
            q0 = (i - 1 if with_ctx else i) * tq
            ks = jnp.clip(q0 - WINDOW, 0, n_lat - WIN_SPAN)
            row0 = pl.multiple_of(ctx_len + ks, LANE)
            qi = lax.broadcasted_iota(jnp.int32, (2 * tq, WIN_SPAN), 0) & (tq - 1)
            ki = lax.broadcasted_iota(jnp.int32, (2 * tq, WIN_SPAN), 1)
            dist = (qi - ki) + (q0 - ks)
            wmask = jnp.where(jnp.abs(dist) <= WINDOW, 0.0, NEG)
        head_row = lax.broadcasted_iota(jnp.int32, (2 * tq, 1), 0) < tq
        for kv in range(2):
            q2 = jnp.concatenate(
                [_norm_rope(q_ref[2 * kv + j].astype(jnp.float32), qg_ref[...], cq_ref[...], sq_ref[...],
                            scale).astype(jnp.bfloat16) for j in range(2)], axis=0)
            sink = jnp.where(head_row, sink_ref[2 * kv], sink_ref[2 * kv + 1]) * LOG2E
            pieces = [(kp_ref[kv, 0:ctx_len, :], vp_ref[kv, 0:ctx_len, :], None)]
            if latent:
                pieces.append((kp_ref[kv, pl.ds(row0, WIN_SPAN), :], vp_ref[kv, pl.ds(row0, WIN_SPAN), :], wmask))
            acc, m = _softmax_pv(q2, pieces, m=sink)
            o = _normalise(acc, jnp.exp2(sink - m))
            for j in range(2):
                _gated_store(o_ref, 2 * kv + j, o[j * tq:(j + 1) * tq], g_ref)

    if with_ctx:
        pl.when(i == 0)(lambda: attend(False))
        pl.when(i > 0)(lambda: attend(True))
    else:
        attend(True)


def _attn_gqa(proj, tabs, qg, kg, sink, chunks, with_ctx, ctx_len):
    cq0, ck0, cv0, cg0 = chunks
    bsz, _, t, _ = proj.shape
    blk0 = 0 if with_ctx else ctx_len // TQ
    nq = t // TQ - blk0
    row = lambda i: i + blk0
    cos, sin = tabs
    in_specs = [_chunk_spec(4, TQ, cq0, row), _chunk_spec(2, t, ck0, lambda i: 0),
                _chunk_spec(2, t, cv0, lambda i: 0), _chunk_spec(4, TQ, cg0, row),
                pl.BlockSpec((TQ, LANE), lambda b, i: (row(i), 0)),
                pl.BlockSpec((TQ, LANE), lambda b, i: (row(i), 0)),
                pl.BlockSpec((t, LANE), lambda b, i: (0, 0)),
                pl.BlockSpec((t, LANE), lambda b, i: (0, 0)),
                pl.BlockSpec((1, LANE), lambda b, i: (0, 0)),
                pl.BlockSpec((1, LANE), lambda b, i: (0, 0))]
    args = [proj, proj, proj, proj, cos, sin, cos, sin, qg, kg]
    if sink is None:
        body = _attn_a_kernel
    else:
        body = _attn_c_kernel
        in_specs = [pl.BlockSpec(memory_space=pltpu.SMEM)] + in_specs
        args = [sink] + args
    return pl.pallas_call(
        functools.partial(body, with_ctx=with_ctx, ctx_len=ctx_len),
        grid=(bsz, nq),
        in_specs=in_specs,
        out_specs=pl.BlockSpec((None, TQ, BRANCH_W), lambda b, i: (b, i, 0)),
        out_shape=jax.ShapeDtypeStruct((bsz, nq * TQ, BRANCH_W), jnp.bfloat16),
        scratch_shapes=[pltpu.VMEM((2, t, LANE), jnp.bfloat16), pltpu.VMEM((2, t, 2 * LANE), jnp.bfloat16)],
        name="attn_a" if sink is None else "attn_c",
        compiler_params=_params("arbitrary", "arbitrary"),
    )(*args)


def _attn_b_kernel(q_ref, k_ref, v_ref, g_ref, bias_ref, qg_ref, kg_ref, o_ref, kp_ref, vp_ref,
                   *, with_ctx, ctx_len):
    i = pl.program_id(1)
    tq = q_ref.shape[1]
    rows = (k_ref.shape[1] - ctx_len) // GRID_W
    strip_rows = NA_STRIP // GRID_W
    scale = HEAD_DIM ** -0.5 * LOG2E

    @pl.when(i == 0)
    def _():
        for h in range(N_HEADS):
            kp_ref[h] = _norm_rope(k_ref[h].astype(jnp.float32), kg_ref[...]).astype(jnp.bfloat16)
            _store_values(vp_ref, h, v_ref[h])

    def attend(latent):
        if latent:
            r0 = (i - 1 if with_ctx else i) * (tq // GRID_W)
            ss = jnp.clip(r0 - NA_KH // 2, 0, rows - strip_rows)
            row0 = pl.multiple_of(ctx_len + ss * GRID_W, LANE)
        for h in range(N_HEADS):
            q = _norm_rope(q_ref[h].astype(jnp.float32), qg_ref[...], scale=scale).astype(jnp.bfloat16)
            pieces = [(kp_ref[h, 0:ctx_len, :], vp_ref[h, 0:ctx_len, :], None)]
            if latent:
                pieces.append((kp_ref[h, pl.ds(row0, NA_STRIP), :], vp_ref[h, pl.ds(row0, NA_STRIP), :],
                               bias_ref[h]))
            acc, _ = _softmax_pv(q, pieces)
            _gated_store(o_ref, h, _normalise(acc), g_ref)

    if with_ctx:
        pl.when(i == 0)(lambda: attend(False))
        pl.when(i > 0)(lambda: attend(True))
    else:
        attend(True)


def _na_bias_kernel(rp_ref, o_ref, *, rows):
    q_rows, strip_rows = TQ // GRID_W, NA_STRIP // GRID_W
    qc = lax.broadcasted_iota(jnp.int32, (GRID_W, LANE), 0)
    lane = lax.broadcasted_iota(jnp.int32, (GRID_W, LANE), 1)
    kc = lane & (GRID_W - 1)
    cs = jnp.clip(qc - NA_KW // 2, 0, GRID_W - NA_KW)
    col_ok = (kc >= cs) & (kc < cs + NA_KW)
    second = lane >= GRID_W
    for var, r0 in enumerate((0, q_rows, rows - q_rows)):
        ss = min(max(r0 - NA_KH // 2, 0), rows - strip_rows)
        for j in range(q_rows):
            qr = r0 + j
            rs = min(max(qr - NA_KH // 2, 0), rows - NA_KH)
            for p in range(strip_rows // 2):
                kr0 = ss + 2 * p
                ok0, ok1 = rs <= kr0 < rs + NA_KH, rs <= kr0 + 1 < rs + NA_KH
                if ok0 or ok1:
                    e = kr0 - qr + NA_KH - 1
                    x = jnp.broadcast_to(rp_ref[e + 1:e + 2, :], (GRID_W, LANE))
                    band = pltpu.roll(x, LANE - (NA_KW - 1), 1, stride=1, stride_axis=0)
                    row_ok = second if (ok1 and not ok0) else (~second if (ok0 and not ok1) else None)
                    valid = col_ok if row_ok is None else (col_ok & row_ok)
                    tile = jnp.where(valid, band * LOG2E, NEG)
                else:
                    tile = jnp.full((GRID_W, LANE), NEG, jnp.float32)
                o_ref[var, j * GRID_W:(j + 1) * GRID_W, p * LANE:(p + 1) * LANE] = tile


def _na_bias_tables(rpb, n_lat):
    depth, nh, nr, nc = rpb.shape
    z = jnp.zeros((depth, nh, nr + 2, GRID_W), jnp.float32).at[:, :, 1:nr + 1, :nc].set(rpb)
    rp = jnp.concatenate([z[:, :, :-1], z[:, :, 1:]], axis=-1)
    return pl.pallas_call(
        functools.partial(_na_bias_kernel, rows=n_lat // GRID_W),
        grid=(depth, nh),
        in_specs=[pl.BlockSpec((None, None, nr + 1, LANE), lambda l, h: (l, h, 0, 0))],
        out_specs=pl.BlockSpec((None, 3, None, TQ, NA_STRIP), lambda l, h: (l, 0, h, 0, 0)),
        out_shape=jax.ShapeDtypeStruct((depth, 3, nh, TQ, NA_STRIP), jnp.float32),
        name="na_bias",
        compiler_params=_params("arbitrary", "arbitrary"),
    )(rp)


def _attn_b(proj, bias, layer, qg, kg, with_ctx, ctx_len):
    bsz, _, t, _ = proj.shape
    blk0 = 0 if with_ctx else ctx_len // TQ
    nq = t // TQ - blk0
    n_lat_blocks = (t - ctx_len) // TQ
    row = lambda i: i + blk0

    def variant(i):
        jl = i + blk0 - ctx_len // TQ
        return jnp.where(jl <= 0, 0, jnp.where(jl == n_lat_blocks - 1, 2, 1))

    return pl.pallas_call(
        functools.partial(_attn_b_kernel, with_ctx=with_ctx, ctx_len=ctx_len),
        grid=(bsz, nq),
        in_specs=[_chunk_spec(4, TQ, B_Q, row), _chunk_spec(4, t, B_K, lambda i: 0),
                  _chunk_spec(4, t, B_V, lambda i: 0), _chunk_spec(4, TQ, B_G, row),
                  pl.BlockSpec((None, None, N_HEADS, TQ, NA_STRIP), lambda b, i: (layer, variant(i), 0, 0, 0)),
                  pl.BlockSpec((1, LANE), lambda b, i: (0, 0)),
                  pl.BlockSpec((1, LANE), lambda b, i: (0, 0))],
        out_specs=pl.BlockSpec((None, TQ, BRANCH_W), lambda b, i: (b, i, 0)),
        out_shape=jax.ShapeDtypeStruct((bsz, nq * TQ, BRANCH_W), jnp.bfloat16),
        scratch_shapes=[pltpu.VMEM((N_HEADS, t, LANE), jnp.bfloat16),
                        pltpu.VMEM((N_HEADS, t, 2 * LANE), jnp.bfloat16)],
        name="attn_b",
        compiler_params=_params("arbitrary", "arbitrary"),
    )(proj, proj, proj, proj, bias, qg, kg)


def _attn_d_kernel(qn_ref, qr_ref, ckv_ref, kr_ref, g_ref, cq_ref, sq_ref, ck_ref, sk_ref,
                   qgn_ref, qgr_ref, kgn_ref, kgr_ref, kvg_ref, wuk_ref, wuv_ref,
                   o_ref, kp_ref, vp_ref, *, with_ctx, ctx_len):
    i = pl.program_id(1)
    t = ckv_ref.shape[1]
    scale = MLA_QK ** -0.5 * LOG2E
    lane_grp = (lax.broadcasted_iota(jnp.int32, (1, LANE), 1) // (MLA_ROPE // 2)) % 2

    @pl.when(i == 0)
    def _():
        rc = 768
        ones_col = jnp.where(lax.broadcasted_iota(jnp.int32, (rc, LANE), 1) == 0, 1.0, 0.0).astype(jnp.bfloat16)
        for r in range(0, t, rc):
            c = [ckv_ref[j, r:r + rc, :].astype(jnp.float32) for j in range(4)]
            ms = sum(jnp.sum(cj * cj, axis=-1, keepdims=True) for cj in c) / MLA_KV_RANK
            inv = lax.rsqrt(ms + EPS)
            cn = jnp.concatenate([(c[j] * inv * kvg_ref[j:j + 1, :]) for j in range(4)],
                                 axis=-1).astype(jnp.bfloat16)
            kn = jnp.dot(cn, wuk_ref[...], preferred_element_type=jnp.float32)
            vv = jnp.dot(cn, wuv_ref[...], preferred_element_type=jnp.float32)
            kr = kr_ref[0, r:r + rc, :].astype(jnp.float32)
            kr_ss = 0.5 * jnp.sum(kr * kr, axis=-1, keepdims=True)
            cos, sin = ck_ref[r:r + rc, :], sk_ref[r:r + rc, :]
            for h in range(N_HEADS):
                kh = kn[:, h * LANE:(h + 1) * LANE]
                ms_h = (jnp.sum(kh * kh, axis=-1, keepdims=True) + kr_ss) / MLA_QK
                inv_h = lax.rsqrt(ms_h + EPS)
                kt = kr * inv_h * kgr_ref[...]
                kt = kt * cos + pltpu.roll(kt, 64, 1) * sin
                kp_ref[h, r:r + rc, :] = jnp.concatenate([kh * inv_h * kgn_ref[...], kt],
                                                         axis=-1).astype(jnp.bfloat16)
                vp_ref[h, r:r + rc, 0:LANE] = vv[:, h * LANE:(h + 1) * LANE].astype(jnp.bfloat16)
                vp_ref[h, r:r + rc, LANE:2 * LANE] = ones_col

    def attend(spans):
        for h in range(N_HEADS):
            qn = qn_ref[h].astype(jnp.float32)
            qt = jnp.where(lane_grp == h % 2, qr_ref[h // 2].astype(jnp.float32), 0.0)
            ms = (jnp.sum(qn * qn, axis=-1, keepdims=True) + jnp.sum(qt * qt, axis=-1, keepdims=True)) / MLA_QK
            inv = lax.rsqrt(ms + EPS)
            qt = qt * inv * qgr_ref[...]
            qt = qt * cq_ref[...] + pltpu.roll(qt, 64, 1) * sq_ref[...]
            q = (jnp.concatenate([qn * inv * qgn_ref[...], qt], axis=-1) * scale).astype(jnp.bfloat16)
            acc, _ = _softmax_pv(q, [(kp_ref[h, lo:hi, :], vp_ref[h, lo:hi, :], None) for lo, hi in spans])
            _gated_store(o_ref, h, _normalise(acc), g_ref)

    lat_spans = _key_spans(t)
    if with_ctx:
        pl.when(i == 0)(lambda: attend([(0, ctx_len)]))
        pl.when(i > 0)(lambda: attend(lat_spans))
    else:
        attend(lat_spans)


def _attn_d(proj, tabs, gains, wuk, wuv, with_ctx, ctx_len):
    bsz, _, t, _ = proj.shape
    blk0 = 0 if with_ctx else ctx_len // TQ
    nq = t // TQ - blk0
    row = lambda i: i + blk0
    cos, sin = tabs
    qgn, qgr, kgn, kgr, kvg = gains
    vec = pl.BlockSpec((1, LANE), lambda b, i: (0, 0))
    return pl.pallas_call(
        functools.partial(_attn_d_kernel, with_ctx=with_ctx, ctx_len=ctx_len),
        grid=(bsz, nq),
        in_specs=[_chunk_spec(4, TQ, D_QN, row), _chunk_spec(2, TQ, D_QR, row),
                  _chunk_spec(4, t, D_CKV, lambda i: 0), _chunk_spec(1, t, D_KR, lambda i: 0),
                  _chunk_spec(4, TQ, D_G, row),
                  pl.BlockSpec((TQ, LANE), lambda b, i: (row(i), 0)),
                  pl.BlockSpec((TQ, LANE), lambda b, i: (row(i), 0)),
                  pl.BlockSpec((t, LANE), lambda b, i: (0, 0)),
                  pl.BlockSpec((t, LANE), lambda b, i: (0, 0)),
                  vec, vec, vec, vec,
                  pl.BlockSpec((4, LANE), lambda b, i: (0, 0)),
                  pl.BlockSpec(wuk.shape, lambda b, i: (0, 0)),
                  pl.BlockSpec(wuv.shape, lambda b, i: (0, 0))],
        out_specs=pl.BlockSpec((None, TQ, BRANCH_W), lambda b, i: (b, i, 0)),
        out_shape=jax.ShapeDtypeStruct((bsz, nq * TQ, BRANCH_W), jnp.bfloat16),
        scratch_shapes=[pltpu.VMEM((N_HEADS, t, 2 * LANE), jnp.bfloat16),
                        pltpu.VMEM((N_HEADS, t, 2 * LANE), jnp.bfloat16)],
        name="attn_d",
        compiler_params=_params("arbitrary", "arbitrary"),
    )(proj, proj, proj, proj, proj, cos, sin, cos, sin, qgn, qgr, kgn, kgr, kvg, wuk, wuv)


def _outproj_kernel(s_ref, oa_ref, ob_ref, oc_ref, od_ref, w_ref, mod_ref, o_ref, *, blk0, ctx_blocks):
    b, i = pl.program_id(0), pl.program_id(1)
    d = s_ref.shape[1]
    acc = jnp.dot(oa_ref[...], w_ref[0], preferred_element_type=jnp.float32)
    acc += jnp.dot(ob_ref[...], w_ref[1], preferred_element_type=jnp.float32)
    acc += jnp.dot(oc_ref[...], w_ref[2], preferred_element_type=jnp.float32)
    acc += jnp.dot(od_ref[...], w_ref[3], preferred_element_type=jnp.float32)
    mod_row = jnp.where(i + blk0 < ctx_blocks, 8, b)
    gate = mod_ref[pl.ds(mod_row, 1), 2 * d:3 * d]
    o_ref[...] = s_ref[...] + gate * acc


def _outproj(stream, outs, w, layer, mod, with_ctx, ctx_len):
    bsz, t, d = stream.shape
    blk0 = 0 if with_ctx else ctx_len // TQ
    nq = t // TQ - blk0
    mix = pl.BlockSpec((None, TQ, BRANCH_W), lambda b, i: (b, i, 0))
    return pl.pallas_call(
        functools.partial(_outproj_kernel, blk0=blk0, ctx_blocks=ctx_len // TQ),
        grid=(bsz, nq),
        in_specs=[pl.BlockSpec((None, TQ, d), lambda b, i: (b, i + blk0, 0)),
                  mix, mix, mix, mix,
                  pl.BlockSpec((None,) + w.shape[1:], lambda b, i: (layer, 0, 0, 0)),
                  pl.BlockSpec(mod.shape, lambda b, i: (0, 0))],
        out_specs=pl.BlockSpec((None, TQ, d), lambda b, i: (b, i, 0)),
        out_shape=jax.ShapeDtypeStruct((bsz, nq * TQ, d), jnp.float32),
        name="outproj",
        compiler_params=_params("arbitrary", "arbitrary"),
    )(stream, *outs, w, mod)


def _permute_w_d(w_in):
    sl = lambda a, n: w_in[..., a:a + n].astype(jnp.bfloat16)
    qd, half = ABC_COLS, MLA_ROPE // 2
    pieces = [sl(qd + MLA_QK * h, MLA_NOPE) for h in range(N_HEADS)]
    pieces += [sl(6464, BRANCH_W), sl(5888, MLA_KV_RANK)]
    for pair in range(2):
        ha, hb = qd + MLA_QK * (2 * pair) + MLA_NOPE, qd + MLA_QK * (2 * pair + 1) + MLA_NOPE
        pieces += [sl(ha, half), sl(hb, half), sl(ha + half, half), sl(hb + half, half)]
    pieces += [sl(6400, half), sl(6400, half), sl(6400 + half, half), sl(6400 + half, half)]
    pieces += [jnp.zeros(w_in.shape[:-1] + (LANE,), jnp.bfloat16)]
    return jnp.concatenate(pieces, axis=-1)


def _w_in_kernel(w_ref, wd_ref, o_ref, *, n_abc):
    j = pl.program_id(1)

    @pl.when(j < n_abc)
    def _():
        o_ref[...] = w_ref[...].astype(jnp.bfloat16)

    @pl.when(j >= n_abc)
    def _():
        o_ref[...] = wd_ref[...]


def _prep_w_in(w_in):
    depth, d, _ = w_in.shape
    wd = _permute_w_d(w_in)
    n_abc = ABC_COLS // TN_IN
    n_d = wd.shape[-1] // TN_IN
    return pl.pallas_call(
        functools.partial(_w_in_kernel, n_abc=n_abc),
        grid=(depth, n_abc + n_d),
        in_specs=[pl.BlockSpec((None, d, TN_IN), lambda l, j: (l, 0, jnp.minimum(j, n_abc - 1))),
                  pl.BlockSpec((None, d, TN_IN), lambda l, j: (l, 0, jnp.maximum(j - n_abc, 0)))],
        out_specs=pl.BlockSpec((None, d, TN_IN), lambda l, j: (l, 0, j)),
        out_shape=jax.ShapeDtypeStruct((depth, d, ABC_COLS + wd.shape[-1]), jnp.bfloat16),
        name="w_in_cast",
        compiler_params=_params("arbitrary", "arbitrary"),
    )(w_in, wd)


def _rope_tables(n_lat, ctx_len, rot_dim):
    tpos = jnp.arange(n_lat)
    row = (tpos // GRID_W).astype(jnp.float32)
    col = (tpos % GRID_W).astype(jnp.float32)
    n_freq = rot_dim // 4
    inv_freq = ROPE_THETA ** (-jnp.arange(n_freq, dtype=jnp.float32) / n_freq)
    ang = jnp.concatenate([row[:, None] * inv_freq, col[:, None] * inv_freq], axis=-1)
    cos, sin = jnp.cos(ang), jnp.sin(ang)
    rep = LANE // rot_dim
    cos_l = jnp.concatenate([cos] * (2 * rep), axis=-1)
    sin_l = jnp.concatenate([-sin] * rep + [sin] * rep, axis=-1)
    cos_l = jnp.concatenate([jnp.ones((ctx_len, LANE), jnp.float32), cos_l], axis=0)
    sin_l = jnp.concatenate([jnp.zeros((ctx_len, LANE), jnp.float32), sin_l], axis=0)
    return cos_l, sin_l


def _dup_rope_gain(g):
    half = MLA_ROPE // 2
    r1, r2 = g[MLA_NOPE:MLA_NOPE + half], g[MLA_NOPE + half:]
    return g[None, :MLA_NOPE], jnp.concatenate([r1, r1, r2, r2])[None, :]


def kernel(x, c, ctx, c_ctx, norm_g, w_ada, b_ada, w_in, w_out, a_q_g, a_k_g, b_q_g, b_k_g, b_rpb,
           c_q_g, c_k_g, c_sink, d_q_g, d_k_g, d_kv_g, d_w_uk, d_w_uv):
    bsz, n_lat, d = x.shape
    ctx_len = ctx.shape[1]
    depth = w_in.shape[0]
    assert bsz <= 8 and ctx_len == TQ and n_lat % TQ == 0 and (ctx_len + n_lat) % TM_IN == 0

    cc = jnp.zeros((16, d), jnp.float32).at[:bsz].set(c).at[8].set(c_ctx)
    mod = _ada(cc, w_ada, b_ada)
    w_in_p = _prep_w_in(w_in)
    w_out_p = w_out.astype(jnp.bfloat16).reshape(depth, 4, BRANCH_W, d)
    wuk, wuv = d_w_uk.astype(jnp.bfloat16), d_w_uv.astype(jnp.bfloat16)
    tabs_h = _rope_tables(n_lat, ctx_len, HEAD_DIM)
    tabs_r = _rope_tables(n_lat, ctx_len, MLA_ROPE)
    na_bias = _na_bias_tables(b_rpb, n_lat)

    stream = jnp.concatenate([ctx.astype(x.dtype), x], axis=1)
    for l in range(depth):
        with_ctx = l < depth - 1
        proj = _inproj(stream, mod[l], norm_g[l][None, :], w_in_p, l, ctx_len)
        oa = _attn_gqa(proj, tabs_h, a_q_g[l][None, :], a_k_g[l][None, :], None,
                       (A_Q, A_K, A_V, A_G), with_ctx, ctx_len)
        ob = _attn_b(proj, na_bias, l, b_q_g[l][None, :], b_k_g[l][None, :], with_ctx, ctx_len)
        oc = _attn_gqa(proj, tabs_h, c_q_g[l][None, :], c_k_g[l][None, :], c_sink[l],
                       (C_Q, C_K, C_V, C_G), with_ctx, ctx_len)
        qgn, qgr = _dup_rope_gain(d_q_g[l])
        kgn, kgr = _dup_rope_gain(d_k_g[l])
        od = _attn_d(proj, tabs_r, (qgn, qgr, kgn, kgr, d_kv_g[l].reshape(4, LANE)), wuk[l], wuv[l],
                     with_ctx, ctx_len)
        stream = _outproj(stream, (oa, ob, oc, od), w_out_p, l, mod[l], with_ctx, ctx_len)
    return stream
```

```python
import functools

import jax
import jax.numpy as jnp
from jax import lax
from jax.experimental import pallas as pl
from jax.experimental.pallas import tpu as pltpu

D_MODEL = 2048
GRID_W = 64
HEAD_DIM = 128
BRANCH_W = 512
N_HEADS = 4
NA_KH = 8
NA_KW = 16
WINDOW = 128
MLA_KV_RANK = 512
MLA_NOPE = 128
MLA_ROPE = 64
MLA_QK = MLA_NOPE + MLA_ROPE
ROPE_THETA = 10000.0
EPS = 1e-6
NEG = -1e30
LOG2E = 1.4426950408889634

LANE = 128
TQ = 256
TM_IN = 1152
TN_IN = 1024
N_CHUNKS = 56
ABC_COLS = 5120
NA_STRIP = 12 * GRID_W
WIN_SPAN = TQ + 2 * WINDOW
VMEM_LIMIT = 48 * 1024 * 1024

A_Q, A_K, A_V, A_G = 0, 4, 6, 8
B_Q, B_K, B_V, B_G = 12, 16, 20, 24
C_Q, C_K, C_V, C_G = 28, 32, 34, 36
D_QN, D_G, D_CKV, D_QR, D_KR = 40, 44, 48, 52, 54

_NT = (((1,), (1,)), ((), ()))


def _params(*sem):
    return pltpu.CompilerParams(dimension_semantics=sem, vmem_limit_bytes=VMEM_LIMIT)


def _silu(x):
    return x * jax.nn.sigmoid(x)


def _ada_kernel(c_ref, w_ref, b_ref, o_ref):
    a = _silu(c_ref[...]).astype(jnp.bfloat16)
    o_ref[...] = jnp.dot(a, w_ref[...].astype(jnp.bfloat16),
                         preferred_element_type=jnp.float32) + b_ref[...]


def _ada(cc, w_ada, b_ada):
    depth, d, n = w_ada.shape
    tn = 512
    return pl.pallas_call(
        _ada_kernel,
        grid=(depth, n // tn),
        in_specs=[pl.BlockSpec((16, d), lambda l, j: (0, 0)),
                  pl.BlockSpec((None, d, tn), lambda l, j: (l, 0, j)),
                  pl.BlockSpec((None, 1, tn), lambda l, j: (l, 0, j))],
        out_specs=pl.BlockSpec((None, 16, tn), lambda l, j: (l, 0, j)),
        out_shape=jax.ShapeDtypeStruct((depth, 16, n), jnp.float32),
        name="ada",
        compiler_params=_params("arbitrary", "arbitrary"),
    )(cc, w_ada, b_ada.reshape(depth, 1, n))


def _inproj_kernel(x_ref, mod_ref, ng_ref, w_ref, o_ref, h_ref, *, ctx_len):
    b, t, j = pl.program_id(0), pl.program_id(1), pl.program_id(2)
    tm, d = x_ref.shape

    @pl.when(j == 0)
    def _():
        for lo, hi, mrow in ((0, ctx_len, jnp.where(t == 0, 8, b)), (ctx_len, tm, b)):
            x = x_ref[lo:hi, :]
            sh = mod_ref[pl.ds(mrow, 1), 0:d]
            gain = ng_ref[...] * (1.0 + mod_ref[pl.ds(mrow, 1), d:2 * d])
            inv = lax.rsqrt(jnp.mean(x * x, axis=-1, keepdims=True) + EPS)
            h_ref[lo:hi, :] = (x * inv * gain + sh).astype(jnp.bfloat16)

    acc = lax.dot_general(h_ref[...], w_ref[...], _NT, preferred_element_type=jnp.float32)
    for c in range(acc.shape[1] // LANE):
        o_ref[c] = acc[:, c * LANE:(c + 1) * LANE].astype(jnp.bfloat16)


def _inproj(stream, mod, ng, w, layer, ctx_len):
    bsz, t, d = stream.shape
    n = w.shape[1]
    cpt = TN_IN // LANE
    return pl.pallas_call(
        functools.partial(_inproj_kernel, ctx_len=ctx_len),
        grid=(bsz, t // TM_IN, n // TN_IN),
        in_specs=[pl.BlockSpec((None, TM_IN, d), lambda b, i, j: (b, i, 0)),
                  pl.BlockSpec(mod.shape, lambda b, i, j: (0, 0)),
                  pl.BlockSpec((1, d), lambda b, i, j: (0, 0)),
                  pl.BlockSpec((None, TN_IN, d), lambda b, i, j: (layer, j, 0))],
        out_specs=pl.BlockSpec((None, cpt, TM_IN, LANE), lambda b, i, j: (b, j, i, 0)),
        out_shape=jax.ShapeDtypeStruct((bsz, n // LANE, t, LANE), jnp.bfloat16),
        scratch_shapes=[pltpu.VMEM((TM_IN, d), jnp.bfloat16)],
        name="inproj",
        compiler_params=_params("arbitrary", "arbitrary", "arbitrary"),
    )(stream, mod, ng, w)


def _norm_rope(x, gain, cos=None, sin=None, scale=None):
    y = x * lax.rsqrt(jnp.mean(x * x, axis=-1, keepdims=True) + EPS) * gain
    if cos is not None:
        y = y * cos + pltpu.roll(y, 64, 1) * sin
    if scale is not None:
        y = y * scale
    return y


def _chunk_spec(n, rows, chunk0, row_fn):
    return pl.BlockSpec((None, n, rows, LANE), lambda b, i: (b, chunk0 // n, row_fn(i), 0))


def _gated_store(o_ref, h, o, g_ref):
    g = g_ref[h].astype(jnp.float32)
    o_ref[:, h * LANE:(h + 1) * LANE] = (o * _silu(g)).astype(o_ref.dtype)


def _store_values(vp_ref, h, v):
    vp_ref[h, :, 0:LANE] = v
    lane = lax.broadcasted_iota(jnp.int32, v.shape, 1)
    vp_ref[h, :, LANE:2 * LANE] = jnp.where(lane == 0, 1.0, 0.0).astype(vp_ref.dtype)


def _softmax_pv(q, pieces, m=None):
    acc = None
    for k, vp, bias in pieces:
        s = lax.dot_general(q, k, _NT, preferred_element_type=jnp.float32)
        if bias is not None:
            s = s + bias
        mc = jnp.max(s, axis=-1, keepdims=True)
        mn = mc if m is None else jnp.maximum(m, mc)
        p = jnp.exp2(s - mn).astype(jnp.bfloat16)
        t = jnp.dot(p, vp, preferred_element_type=jnp.float32)
        acc = t if acc is None else acc * jnp.exp2(m - mn) + t
        m = mn
    return acc, m


def _normalise(acc, extra=None):
    l = acc[:, LANE:LANE + 1]
    if extra is not None:
        l = l + extra
    return acc[:, 0:LANE] / l


def _key_spans(t):
    cut = (t // 2 + 255) // 256 * 256
    return [(0, cut), (cut, t)]


def _attn_a_kernel(q_ref, k_ref, v_ref, g_ref, cq_ref, sq_ref, ck_ref, sk_ref, qg_ref, kg_ref,
                   o_ref, kp_ref, vp_ref, *, with_ctx, ctx_len):
    i = pl.program_id(1)
    tq = q_ref.shape[1]
    t = k_ref.shape[1]
    scale = HEAD_DIM ** -0.5 * LOG2E

    @pl.when(i == 0)
    def _():
        for kv in range(2):
            kp_ref[kv] = _norm_rope(k_ref[kv].astype(jnp.float32), kg_ref[...],
                                    ck_ref[...], sk_ref[...]).astype(jnp.bfloat16)
            _store_values(vp_ref, kv, v_ref[kv])

    def attend(spans):
        for kv in range(2):
            q2 = jnp.concatenate(
                [_norm_rope(q_ref[2 * kv + j].astype(jnp.float32), qg_ref[...], cq_ref[...], sq_ref[...],
                            scale).astype(jnp.bfloat16) for j in range(2)], axis=0)
            acc, _ = _softmax_pv(q2, [(kp_ref[kv, lo:hi, :], vp_ref[kv, lo:hi, :], None) for lo, hi in spans])
            o = _normalise(acc)
            for j in range(2):
                _gated_store(o_ref, 2 * kv + j, o[j * tq:(j + 1) * tq], g_ref)

    lat_spans = _key_spans(t)
    if with_ctx:
        pl.when(i == 0)(lambda: attend([(0, ctx_len)]))
        pl.when(i > 0)(lambda: attend(lat_spans))
    else:
        attend(lat_spans)


def _attn_c_kernel(sink_ref, q_ref, k_ref, v_ref, g_ref, cq_ref, sq_ref, ck_ref, sk_ref, qg_ref, kg_ref,
                   o_ref, kp_ref, vp_ref, *, with_ctx, ctx_len):
    i = pl.program_id(1)
    tq = q_ref.shape[1]
    n_lat = k_ref.shape[1] - ctx_len
    scale = HEAD_DIM ** -0.5 * LOG2E

    @pl.when(i == 0)
    def _():
        for kv in range(2):
            kp_ref[kv] = _norm_rope(k_ref[kv].astype(jnp.float32), kg_ref[...],
                                    ck_ref[...], sk_ref[...]).astype(jnp.bfloat16)
            _store_values(vp_ref, kv, v_ref[kv])

    def attend(latent):
        if latent:
            q0 = (i - 1 if with_ctx else i) * tq
            ks = jnp.clip(q0 - WINDOW, 0, n_lat - WIN_SPAN)
            row0 = pl.multiple_of(ctx_len + ks, LANE)
            qi = lax.broadcasted_iota(jnp.int32, (2 * tq, WIN_SPAN), 0) & (tq - 1)
            ki = lax.broadcasted_iota(jnp.int32, (2 * tq, WIN_SPAN), 1)
            dist = (qi - ki) + (q0 - ks)
            wmask = jnp.where(jnp.abs(dist) <= WINDOW, 0.0, NEG)
        head_row = lax.broadcasted_iota(jnp.int32, (2 * tq, 1), 0) < tq
        for kv in range(2):
            q2 = jnp.concatenate(
                [_norm_rope(q_ref[2 * kv + j].astype(jnp.float32), qg_ref[...], cq_ref[...], sq_ref[...],
                            scale).astype(jnp.bfloat16) for j in range(2)], axis=0)
            sink = jnp.where(head_row, sink_ref[2 * kv], sink_ref[2 * kv + 1]) * LOG2E
            pieces = [(kp_ref[kv, 0:ctx_len, :], vp_ref[kv, 0:ctx_len, :], None)]
            if latent:
                pieces.append((kp_ref[kv, pl.ds(row0, WIN_SPAN), :], vp_ref[kv, pl.ds(row0, WIN_SPAN), :], wmask))
            acc, m = _softmax_pv(q2, pieces, m=sink)
            o = _normalise(acc, jnp.exp2(sink - m))
            for j in range(2):
                _gated_store(o_ref, 2 * kv + j, o[j * tq:(j + 1) * tq], g_ref)

    if with_ctx:
        pl.when(i == 0)(lambda: attend(False))
        pl.when(i > 0)(lambda: attend(True))
    else:
        attend(True)


def _attn_gqa(proj, tabs, qg, kg, sink, chunks, with_ctx, ctx_len):
    cq0, ck0, cv0, cg0 = chunks
    bsz, _, t, _ = proj.shape
    blk0 = 0 if with_ctx else ctx_len // TQ
    nq = t // TQ - blk0
    row = lambda i: i + blk0
    cos, sin = tabs
    in_specs = [_chunk_spec(4, TQ, cq0, row), _chunk_spec(2, t, ck0, lambda i: 0),
                _chunk_spec(2, t, cv0, lambda i: 0), _chunk_spec(4, TQ, cg0, row),
                pl.BlockSpec((TQ, LANE), lambda b, i: (row(i), 0)),
                pl.BlockSpec((TQ, LANE), lambda b, i: (row(i), 0)),
                pl.BlockSpec((t, LANE), lambda b, i: (0, 0)),
                pl.BlockSpec((t, LANE), lambda b, i: (0, 0)),
                pl.BlockSpec((1, LANE), lambda b, i: (0, 0)),
                pl.BlockSpec((1, LANE), lambda b, i: (0, 0))]
    args = [proj, proj, proj, proj, cos, sin, cos, sin, qg, kg]
    if sink is None:
        body = _attn_a_kernel
    else:
        body = _attn_c_kernel
        in_specs = [pl.BlockSpec(memory_space=pltpu.SMEM)] + in_specs
        args = [sink] + args
    return pl.pallas_call(
        functools.partial(body, with_ctx=with_ctx, ctx_len=ctx_len),
        grid=(bsz, nq),
        in_specs=in_specs,
        out_specs=pl.BlockSpec((None, TQ, BRANCH_W), lambda b, i: (b, i, 0)),
        out_shape=jax.ShapeDtypeStruct((bsz, nq * TQ, BRANCH_W), jnp.bfloat16),
        scratch_shapes=[pltpu.VMEM((2, t, LANE), jnp.bfloat16), pltpu.VMEM((2, t, 2 * LANE), jnp.bfloat16)],
        name="attn_a" if sink is None else "attn_c",
        compiler_params=_params("arbitrary", "arbitrary"),
    )(*args)


def _attn_b_kernel(q_ref, k_ref, v_ref, g_ref, bias_ref, qg_ref, kg_ref, o_ref, kp_ref, vp_ref,
                   *, with_ctx, ctx_len):
    i = pl.program_id(1)
    tq = q_ref.shape[1]
    rows = (k_ref.shape[1] - ctx_len) // GRID_W
    strip_rows = NA_STRIP // GRID_W
    scale = HEAD_DIM ** -0.5 * LOG2E

    @pl.when(i == 0)
    def _():
        for h in range(N_HEADS):
            kp_ref[h] = _norm_rope(k_ref[h].astype(jnp.float32), kg_ref[...]).astype(jnp.bfloat16)
            _store_values(vp_ref, h, v_ref[h])

    def attend(latent):
        if latent:
            r0 = (i - 1 if with_ctx else i) * (tq // GRID_W)
            ss = jnp.clip(r0 - NA_KH // 2, 0, rows - strip_rows)
            row0 = pl.multiple_of(ctx_len + ss * GRID_W, LANE)
        for h in range(N_HEADS):
            q = _norm_rope(q_ref[h].astype(jnp.float32), qg_ref[...], scale=scale).astype(jnp.bfloat16)
            pieces = [(kp_ref[h, 0:ctx_len, :], vp_ref[h, 0:ctx_len, :], None)]
            if latent:
                pieces.append((kp_ref[h, pl.ds(row0, NA_STRIP), :], vp_ref[h, pl.ds(row0, NA_STRIP), :],
                               bias_ref[h]))
            acc, _ = _softmax_pv(q, pieces)
            _gated_store(o_ref, h, _normalise(acc), g_ref)

    if with_ctx:
        pl.when(i == 0)(lambda: attend(False))
        pl.when(i > 0)(lambda: attend(True))
    else:
        attend(True)


def _na_bias_kernel(rp_ref, o_ref, *, rows):
    q_rows, strip_rows = TQ // GRID_W, NA_STRIP // GRID_W
    qc = lax.broadcasted_iota(jnp.int32, (GRID_W, LANE), 0)
    lane = lax.broadcasted_iota(jnp.int32, (GRID_W, LANE), 1)
    kc = lane & (GRID_W - 1)
    cs = jnp.clip(qc - NA_KW // 2, 0, GRID_W - NA_KW)
    col_ok = (kc >= cs) & (kc < cs + NA_KW)
    second = lane >= GRID_W
    for var, r0 in enumerate((0, q_rows, rows - q_rows)):
        ss = min(max(r0 - NA_KH // 2, 0), rows - strip_rows)
        for j in range(q_rows):
            qr = r0 + j
            rs = min(max(qr - NA_KH // 2, 0), rows - NA_KH)
            for p in range(strip_rows // 2):
                kr0 = ss + 2 * p
                ok0, ok1 = rs <= kr0 < rs + NA_KH, rs <= kr0 + 1 < rs + NA_KH
                if ok0 or ok1:
                    e = kr0 - qr + NA_KH - 1
                    x = jnp.broadcast_to(rp_ref[e + 1:e + 2, :], (GRID_W, LANE))
                    band = pltpu.roll(x, LANE - (NA_KW - 1), 1, stride=1, stride_axis=0)
                    row_ok = second if (ok1 and not ok0) else (~second if (ok0 and not ok1) else None)
                    valid = col_ok if row_ok is None else (col_ok & row_ok)
                    tile = jnp.where(valid, band * LOG2E, NEG)
                else:
                    tile = jnp.full((GRID_W, LANE), NEG, jnp.float32)
                o_ref[var, j * GRID_W:(j + 1) * GRID_W, p * LANE:(p + 1) * LANE] = tile


def _na_bias_tables(rpb, n_lat):
    depth, nh, nr, nc = rpb.shape
    z = jnp.zeros((depth, nh, nr + 2, GRID_W), jnp.float32).at[:, :, 1:nr + 1, :nc].set(rpb)
    rp = jnp.concatenate([z[:, :, :-1], z[:, :, 1:]], axis=-1)
    return pl.pallas_call(
        functools.partial(_na_bias_kernel, rows=n_lat // GRID_W),
        grid=(depth, nh),
        in_specs=[pl.BlockSpec((None, None, nr + 1, LANE), lambda l, h: (l, h, 0, 0))],
        out_specs=pl.BlockSpec((None, 3, None, TQ, NA_STRIP), lambda l, h: (l, 0, h, 0, 0)),
        out_shape=jax.ShapeDtypeStruct((depth, 3, nh, TQ, NA_STRIP), jnp.float32),
        name="na_bias",
        compiler_params=_params("arbitrary", "arbitrary"),
    )(rp)


def _attn_b(proj, bias, layer, qg, kg, with_ctx, ctx_len):
    bsz, _, t, _ = proj.shape
    blk0 = 0 if with_ctx else ctx_len // TQ
    nq = t // TQ - blk0
    n_lat_blocks = (t - ctx_len) // TQ
    row = lambda i: i + blk0

    def variant(i):
        jl = i + blk0 - ctx_len // TQ
        return jnp.where(jl <= 0, 0, jnp.where(jl == n_lat_blocks - 1, 2, 1))

    return pl.pallas_call(
        functools.partial(_attn_b_kernel, with_ctx=with_ctx, ctx_len=ctx_len),
        grid=(bsz, nq),
        in_specs=[_chunk_spec(4, TQ, B_Q, row), _chunk_spec(4, t, B_K, lambda i: 0),
                  _chunk_spec(4, t, B_V, lambda i: 0), _chunk_spec(4, TQ, B_G, row),
                  pl.BlockSpec((None, None, N_HEADS, TQ, NA_STRIP), lambda b, i: (layer, variant(i), 0, 0, 0)),
                  pl.BlockSpec((1, LANE), lambda b, i: (0, 0)),
                  pl.BlockSpec((1, LANE), lambda b, i: (0, 0))],
        out_specs=pl.BlockSpec((None, TQ, BRANCH_W), lambda b, i: (b, i, 0)),
        out_shape=jax.ShapeDtypeStruct((bsz, nq * TQ, BRANCH_W), jnp.bfloat16),
        scratch_shapes=[pltpu.VMEM((N_HEADS, t, LANE), jnp.bfloat16),
                        pltpu.VMEM((N_HEADS, t, 2 * LANE), jnp.bfloat16)],
        name="attn_b",
        compiler_params=_params("arbitrary", "arbitrary"),
    )(proj, proj, proj, proj, bias, qg, kg)


def _attn_d_kernel(qn_ref, qr_ref, ckv_ref, kr_ref, g_ref, cq_ref, sq_ref, ck_ref, sk_ref,
                   qgn_ref, qgr_ref, kgn_ref, kgr_ref, kvg_ref, wuk_ref, wuv_ref,
                   o_ref, kp_ref, vp_ref, *, with_ctx, ctx_len):
    i = pl.program_id(1)
    t = ckv_ref.shape[1]
    scale = MLA_QK ** -0.5 * LOG2E
    lane_grp = (lax.broadcasted_iota(jnp.int32, (1, LANE), 1) // (MLA_ROPE // 2)) % 2

    @pl.when(i == 0)
    def _():
        rc = 768
        ones_col = jnp.where(lax.broadcasted_iota(jnp.int32, (rc, LANE), 1) == 0, 1.0, 0.0).astype(jnp.bfloat16)
        for r in range(0, t, rc):
            c = [ckv_ref[j, r:r + rc, :].astype(jnp.float32) for j in range(4)]
            ms = sum(jnp.sum(cj * cj, axis=-1, keepdims=True) for cj in c) / MLA_KV_RANK
            inv = lax.rsqrt(ms + EPS)
            cn = jnp.concatenate([(c[j] * inv * kvg_ref[j:j + 1, :]) for j in range(4)],
                                 axis=-1).astype(jnp.bfloat16)
            kn = jnp.dot(cn, wuk_ref[...], preferred_element_type=jnp.float32)
            vv = jnp.dot(cn, wuv_ref[...], preferred_element_type=jnp.float32)
            kr = kr_ref[0, r:r + rc, :].astype(jnp.float32)
            kr_ss = 0.5 * jnp.sum(kr * kr, axis=-1, keepdims=True)
            cos, sin = ck_ref[r:r + rc, :], sk_ref[r:r + rc, :]
            for h in range(N_HEADS):
                kh = kn[:, h * LANE:(h + 1) * LANE]
                ms_h = (jnp.sum(kh * kh, axis=-1, keepdims=True) + kr_ss) / MLA_QK
                inv_h = lax.rsqrt(ms_h + EPS)
                kt = kr * inv_h * kgr_ref[...]
                kt = kt * cos + pltpu.roll(kt, 64, 1) * sin
                kp_ref[h, r:r + rc, :] = jnp.concatenate([kh * inv_h * kgn_ref[...], kt],
                                                         axis=-1).astype(jnp.bfloat16)
                vp_ref[h, r:r + rc, 0:LANE] = vv[:, h * LANE:(h + 1) * LANE].astype(jnp.bfloat16)
                vp_ref[h, r:r + rc, LANE:2 * LANE] = ones_col

    def attend(spans):
        for h in range(N_HEADS):
            qn = qn_ref[h].astype(jnp.float32)
            qt = jnp.where(lane_grp == h % 2, qr_ref[h // 2].astype(jnp.float32), 0.0)
            ms = (jnp.sum(qn * qn, axis=-1, keepdims=True) + jnp.sum(qt * qt, axis=-1, keepdims=True)) / MLA_QK
            inv = lax.rsqrt(ms + EPS)
            qt = qt * inv * qgr_ref[...]
            qt = qt * cq_ref[...] + pltpu.roll(qt, 64, 1) * sq_ref[...]
            q = (jnp.concatenate([qn * inv * qgn_ref[...], qt], axis=-1) * scale).astype(jnp.bfloat16)
            acc, _ = _softmax_pv(q, [(kp_ref[h, lo:hi, :], vp_ref[h, lo:hi, :], None) for lo, hi in spans])
            _gated_store(o_ref, h, _normalise(acc), g_ref)

    lat_spans = _key_spans(t)
    if with_ctx:
        pl.when(i == 0)(lambda: attend([(0, ctx_len)]))
        pl.when(i > 0)(lambda: attend(lat_spans))
    else:
        attend(lat_spans)


def _attn_d(proj, tabs, gains, wuk, wuv, with_ctx, ctx_len):
    bsz, _, t, _ = proj.shape
    blk0 = 0 if with_ctx else ctx_len // TQ
    nq = t // TQ - blk0
    row = lambda i: i + blk0
    cos, sin = tabs
    qgn, qgr, kgn, kgr, kvg = gains
    vec = pl.BlockSpec((1, LANE), lambda b, i: (0, 0))
    return pl.pallas_call(
        functools.partial(_attn_d_kernel, with_ctx=with_ctx, ctx_len=ctx_len),
        grid=(bsz, nq),
        in_specs=[_chunk_spec(4, TQ, D_QN, row), _chunk_spec(2, TQ, D_QR, row),
                  _chunk_spec(4, t, D_CKV, lambda i: 0), _chunk_spec(1, t, D_KR, lambda i: 0),
                  _chunk_spec(4, TQ, D_G, row),
                  pl.BlockSpec((TQ, LANE), lambda b, i: (row(i), 0)),
                  pl.BlockSpec((TQ, LANE), lambda b, i: (row(i), 0)),
                  pl.BlockSpec((t, LANE), lambda b, i: (0, 0)),
                  pl.BlockSpec((t, LANE), lambda b, i: (0, 0)),
                  vec, vec, vec, vec,
                  pl.BlockSpec((4, LANE), lambda b, i: (0, 0)),
                  pl.BlockSpec(wuk.shape, lambda b, i: (0, 0)),
                  pl.BlockSpec(wuv.shape, lambda b, i: (0, 0))],
        out_specs=pl.BlockSpec((None, TQ, BRANCH_W), lambda b, i: (b, i, 0)),
        out_shape=jax.ShapeDtypeStruct((bsz, nq * TQ, BRANCH_W), jnp.bfloat16),
        scratch_shapes=[pltpu.VMEM((N_HEADS, t, 2 * LANE), jnp.bfloat16),
                        pltpu.VMEM((N_HEADS, t, 2 * LANE), jnp.bfloat16)],
        name="attn_d",
        compiler_params=_params("arbitrary", "arbitrary"),
    )(proj, proj, proj, proj, proj, cos, sin, cos, sin, qgn, qgr, kgn, kgr, kvg, wuk, wuv)


def _outproj_kernel(s_ref, oa_ref, ob_ref, oc_ref, od_ref, w_ref, mod_ref, o_ref, *, blk0, ctx_blocks):
    b, i = pl.program_id(0), pl.program_id(1)
    d = s_ref.shape[1]
    acc = jnp.dot(oa_ref[...], w_ref[0], preferred_element_type=jnp.float32)
    acc += jnp.dot(ob_ref[...], w_ref[1], preferred_element_type=jnp.float32)
    acc += jnp.dot(oc_ref[...], w_ref[2], preferred_element_type=jnp.float32)
    acc += jnp.dot(od_ref[...], w_ref[3], preferred_element_type=jnp.float32)
    mod_row = jnp.where(i + blk0 < ctx_blocks, 8, b)
    gate = mod_ref[pl.ds(mod_row, 1), 2 * d:3 * d]
    o_ref[...] = s_ref[...] + gate * acc


def _outproj(stream, outs, w, layer, mod, with_ctx, ctx_len):
    bsz, t, d = stream.shape
    blk0 = 0 if with_ctx else ctx_len // TQ
    nq = t // TQ - blk0
    mix = pl.BlockSpec((None, TQ, BRANCH_W), lambda b, i: (b, i, 0))
    return pl.pallas_call(
        functools.partial(_outproj_kernel, blk0=blk0, ctx_blocks=ctx_len // TQ),
        grid=(bsz, nq),
        in_specs=[pl.BlockSpec((None, TQ, d), lambda b, i: (b, i + blk0, 0)),
                  mix, mix, mix, mix,
                  pl.BlockSpec((None,) + w.shape[1:], lambda b, i: (layer, 0, 0, 0)),
                  pl.BlockSpec(mod.shape, lambda b, i: (0, 0))],
        out_specs=pl.BlockSpec((None, TQ, d), lambda b, i: (b, i, 0)),
        out_shape=jax.ShapeDtypeStruct((bsz, nq * TQ, d), jnp.float32),
        name="outproj",
        compiler_params=_params("arbitrary", "arbitrary"),
    )(stream, *outs, w, mod)


def _permute_w_d(w_t):
    sl = lambda a, n: w_t[:, a:a + n, :].astype(jnp.bfloat16)
    qd, half = ABC_COLS, MLA_ROPE // 2
    pieces = [sl(qd + MLA_QK * h, MLA_NOPE) for h in range(N_HEADS)]
    pieces += [sl(6464, BRANCH_W), sl(5888, MLA_KV_RANK)]
    for pair in range(2):
        ha, hb = qd + MLA_QK * (2 * pair) + MLA_NOPE, qd + MLA_QK * (2 * pair + 1) + MLA_NOPE
        pieces += [sl(ha, half), sl(hb, half), sl(ha + half, half), sl(hb + half, half)]
    pieces += [sl(6400, half), sl(6400, half), sl(6400 + half, half), sl(6400 + half, half)]
    pieces += [jnp.zeros((w_t.shape[0], LANE, w_t.shape[2]), jnp.bfloat16)]
    return jnp.concatenate(pieces, axis=1)


def _w_in_kernel(w_ref, wd_ref, o_ref, *, n_abc):
    j = pl.program_id(1)

    @pl.when(j < n_abc)
    def _():
        o_ref[...] = w_ref[...].astype(jnp.bfloat16)

    @pl.when(j >= n_abc)
    def _():
        o_ref[...] = wd_ref[...]


def _prep_w_in(w_in):
    depth, d, _ = w_in.shape
    w_t = jnp.swapaxes(w_in, 1, 2)
    wd = _permute_w_d(w_t)
    n_abc = ABC_COLS // TN_IN
    n_d = wd.shape[1] // TN_IN
    return pl.pallas_call(
        functools.partial(_w_in_kernel, n_abc=n_abc),
        grid=(depth, n_abc + n_d),
        in_specs=[pl.BlockSpec((None, TN_IN, d), lambda l, j: (l, jnp.minimum(j, n_abc - 1), 0)),
                  pl.BlockSpec((None, TN_IN, d), lambda l, j: (l, jnp.maximum(j - n_abc, 0), 0))],
        out_specs=pl.BlockSpec((None, TN_IN, d), lambda l, j: (l, j, 0)),
        out_shape=jax.ShapeDtypeStruct((depth, ABC_COLS + wd.shape[1], d), jnp.bfloat16),
        name="w_in_cast",
        compiler_params=_params("arbitrary", "arbitrary"),
    )(w_t, wd)


def _rope_tables(n_lat, ctx_len, rot_dim):
    tpos = jnp.arange(n_lat)
    row = (tpos // GRID_W).astype(jnp.float32)
    col = (tpos % GRID_W).astype(jnp.float32)
    n_freq = rot_dim // 4
    inv_freq = ROPE_THETA ** (-jnp.arange(n_freq, dtype=jnp.float32) / n_freq)
    ang = jnp.concatenate([row[:, None] * inv_freq, col[:, None] * inv_freq], axis=-1)
    cos, sin = jnp.cos(ang), jnp.sin(ang)
    rep = LANE // rot_dim
    cos_l = jnp.concatenate([cos] * (2 * rep), axis=-1)
    sin_l = jnp.concatenate([-sin] * rep + [sin] * rep, axis=-1)
    cos_l = jnp.concatenate([jnp.ones((ctx_len, LANE), jnp.float32), cos_l], axis=0)
    sin_l = jnp.concatenate([jnp.zeros((ctx_len, LANE), jnp.float32), sin_l], axis=0)
    return cos_l, sin_l


def _dup_rope_gain(g):
    half = MLA_ROPE // 2
    r1, r2 = g[MLA_NOPE:MLA_NOPE + half], g[MLA_NOPE + half:]
    return g[None, :MLA_NOPE], jnp.concatenate([r1, r1, r2, r2])[None, :]


def kernel(x, c, ctx, c_ctx, norm_g, w_ada, b_ada, w_in, w_out, a_q_g, a_k_g, b_q_g, b_k_g, b_rpb,
           c_q_g, c_k_g, c_sink, d_q_g, d_k_g, d_kv_g, d_w_uk, d_w_uv):
    bsz, n_lat, d = x.shape
    ctx_len = ctx.shape[1]
    depth = w_in.shape[0]
    assert bsz <= 8 and ctx_len == TQ and n_lat % TQ == 0 and (ctx_len + n_lat) % TM_IN == 0

    cc = jnp.zeros((16, d), jnp.float32).at[:bsz].set(c).at[8].set(c_ctx)
    mod = _ada(cc, w_ada, b_ada)
    w_in_p = _prep_w_in(w_in)
    w_out_p = w_out.astype(jnp.bfloat16).reshape(depth, 4, BRANCH_W, d)
    wuk, wuv = d_w_uk.astype(jnp.bfloat16), d_w_uv.astype(jnp.bfloat16)
    tabs_h = _rope_tables(n_lat, ctx_len, HEAD_DIM)
    tabs_r = _rope_tables(n_lat, ctx_len, MLA_ROPE)
    na_bias = _na_bias_tables(b_rpb, n_lat)

    stream = jnp.concatenate([ctx.astype(x.dtype), x], axis=1)
    for l in range(depth):
        with_ctx = l < depth - 1
        proj = _inproj(stream, mod[l], norm_g[l][None, :], w_in_p, l, ctx_len)
        oa = _attn_gqa(proj, tabs_h, a_q_g[l][None, :], a_k_g[l][None, :], None,
                       (A_Q, A_K, A_V, A_G), with_ctx, ctx_len)
        ob = _attn_b(proj, na_bias, l, b_q_g[l][None, :], b_k_g[l][None, :], with_ctx, ctx_len)
        oc = _attn_gqa(proj, tabs_h, c_q_g[l][None, :], c_k_g[l][None, :], c_sink[l],
                       (C_Q, C_K, C_V, C_G), with_ctx, ctx_len)
        qgn, qgr = _dup_rope_gain(d_q_g[l])
        kgn, kgr = _dup_rope_gain(d_k_g[l])
        od = _attn_d(proj, tabs_r, (qgn, qgr, kgn, kgr, d_kv_g[l].reshape(4, LANE)), wuk[l], wuv[l],
                     with_ctx, ctx_len)
        stream = _outproj(stream, (oa, ob, oc, od), w_out_p, l, mod[l], with_ctx, ctx_len)
    return stream
```

```python
import functools

import jax
import jax.numpy as jnp
from jax import lax
from jax.experimental import pallas as pl
from jax.experimental.pallas import tpu as pltpu

GRID_W = 64
HEAD_DIM = 128
BRANCH_W = 512
N_HEADS = 4
NA_KH = 8
NA_KW = 16
WINDOW = 128
MLA_KV_RANK = 512
MLA_NOPE = 128
MLA_ROPE = 64
MLA_QK = MLA_NOPE + MLA_ROPE
ROPE_THETA = 10000.0
EPS = 1e-6
NEG = -1e30
LOG2E = 1.4426950408889634

LANE = 128
TQ = 512
TM_IN = 1024
TN_IN = 1024
TM_OUT = 512
CPT = TN_IN // LANE
N_CHUNKS = 56
ABC_COLS = 5120
NA_STRIP = (NA_KH + TQ // GRID_W) * GRID_W
WIN_SPAN = TQ + 2 * WINDOW
VMEM_LIMIT = 48 * 1024 * 1024

A_Q, A_K, A_V, A_G = 0, 4, 6, 8
B_Q, B_K, B_V, B_G = 12, 16, 20, 24
C_Q, C_K, C_V, C_G = 28, 32, 34, 36
D_QN, D_G, D_CKV, D_QR, D_KR = 40, 44, 48, 52, 54

_NT = (((1,), (1,)), ((), ()))


def _params(*sem):
    return pltpu.CompilerParams(dimension_semantics=sem, vmem_limit_bytes=VMEM_LIMIT)


def _silu(x):
    return x * jax.nn.sigmoid(x)


def _ada_kernel(c_ref, w_ref, b_ref, o_ref):
    a = _silu(c_ref[...]).astype(jnp.bfloat16)
    o_ref[...] = jnp.dot(a, w_ref[...].astype(jnp.bfloat16),
                         preferred_element_type=jnp.float32) + b_ref[...]


def _ada(cc, w_ada, b_ada):
    depth, d, n = w_ada.shape
    tn = 512
    return pl.pallas_call(
        _ada_kernel,
        grid=(depth, n // tn),
        in_specs=[pl.BlockSpec((16, d), lambda l, j: (0, 0)),
                  pl.BlockSpec((None, d, tn), lambda l, j: (l, 0, j)),
                  pl.BlockSpec((None, 1, tn), lambda l, j: (l, 0, j))],
        out_specs=pl.BlockSpec((None, 16, tn), lambda l, j: (l, 0, j)),
        out_shape=jax.ShapeDtypeStruct((depth, 16, n), jnp.float32),
        name="ada",
        compiler_params=_params("arbitrary", "arbitrary"),
    )(cc, w_ada, b_ada.reshape(depth, 1, n))


def _inproj_kernel(x_ref, mod_ref, ng_ref, w_ref, o_ref, h_ref, *, mod_row):
    j = pl.program_id(2)
    nb, r, d = x_ref.shape
    mrow = pl.program_id(0) if mod_row is None else mod_row

    @pl.when(j == 0)
    def _():
        sh = mod_ref[pl.ds(mrow, 1), 0:d]
        gain = ng_ref[...] * (1.0 + mod_ref[pl.ds(mrow, 1), d:2 * d])
        for s in range(nb):
            x = x_ref[s]
            inv = lax.rsqrt(jnp.mean(x * x, axis=-1, keepdims=True) + EPS)
            h_ref[s * r:(s + 1) * r, :] = (x * inv * gain + sh).astype(jnp.bfloat16)

    acc = lax.dot_general(h_ref[...], w_ref[...], _NT, preferred_element_type=jnp.float32)
    for s in range(nb):
        for c in range(acc.shape[1] // LANE):
            o_ref[s, c] = acc[s * r:(s + 1) * r, c * LANE:(c + 1) * LANE].astype(jnp.bfloat16)


def _inproj(stream, mod, ng, w, layer, *, mod_row, tile_stride=1):
    bsz, r, d = stream.shape
    nb = min(bsz, max(1, TM_IN // r))
    rows = min(r, TM_IN)
    assert bsz % nb == 0 and r % rows == 0
    n_tiles = -(-(w.shape[1] // TN_IN) // tile_stride)
    return pl.pallas_call(
        functools.partial(_inproj_kernel, mod_row=mod_row),
        grid=(bsz // nb, r // rows, n_tiles),
        in_specs=[pl.BlockSpec((nb, rows, d), lambda b, i, j: (b, i, 0)),
                  pl.BlockSpec(mod.shape, lambda b, i, j: (0, 0)),
                  pl.BlockSpec((1, d), lambda b, i, j: (0, 0)),
                  pl.BlockSpec((None, TN_IN, d), lambda b, i, j: (layer, j * tile_stride, 0))],
        out_specs=pl.BlockSpec((nb, CPT, rows, LANE), lambda b, i, j: (b, j, i, 0)),
        out_shape=jax.ShapeDtypeStruct((bsz, n_tiles * CPT, r, LANE), jnp.bfloat16),
        scratch_shapes=[pltpu.VMEM((nb * rows, d), jnp.bfloat16)],
        name="inproj",
        compiler_params=_params("arbitrary", "arbitrary", "arbitrary"),
    )(stream, mod, ng, w)


def _norm_rope(x, gain, cos=None, sin=None, scale=None):
    y = x * lax.rsqrt(jnp.mean(x * x, axis=-1, keepdims=True) + EPS) * gain
    if cos is not None:
        y = y * cos + pltpu.roll(y, 64, 1) * sin
    if scale is not None:
        y = y * scale
    return y


def _chunk_spec(n, rows, chunk0, row_fn):
    return pl.BlockSpec((None, n, rows, LANE), lambda b, i: (b, chunk0 // n, row_fn(i), 0))


def _gated_store(o_ref, col, o, g):
    g = g.astype(jnp.float32)
    o_ref[:, col:col + LANE] = (o * _silu(g)).astype(o_ref.dtype)


def _with_ones(v):
    lane = lax.broadcasted_iota(jnp.int32, v.shape, 1)
    return jnp.concatenate([v, jnp.where(lane == 0, 1.0, 0.0).astype(v.dtype)], axis=-1)


def _softmax_pv(q, pieces, m=None):
    acc = None
    for k, vp, bias in pieces:
        s = lax.dot_general(q, k, _NT, preferred_element_type=jnp.float32)
        if bias is not None:
            rep, (rows, n) = s.shape[0] // bias.shape[0], bias.shape
            s = (s.reshape(rep, rows, n) + bias[None]).reshape(rep * rows, n)
        mc = jnp.max(s, axis=-1, keepdims=True)
        mn = mc if m is None else jnp.maximum(m, mc)
        p = jnp.exp2(s - mn).astype(jnp.bfloat16)
        t = jnp.dot(p, vp, preferred_element_type=jnp.float32)
        acc = t if acc is None else acc * jnp.exp2(m - mn) + t
        m = mn
    return acc, m


def _normalise(acc, extra=None):
    l = acc[:, LANE:LANE + 1]
    if extra is not None:
        l = l + extra
    return acc[:, 0:LANE] / l


def _key_spans(t):
    cut = (t // 2 + 255) // 256 * 256
    return [(0, cut), (cut, t)]


def _gqa_queries(q, kv, qg, cos, sin):
    scale = HEAD_DIM ** -0.5 * LOG2E
    return jnp.concatenate(
        [_norm_rope(q(2 * kv + j).astype(jnp.float32), qg, cos, sin, scale).astype(jnp.bfloat16)
         for j in range(2)], axis=0)


def _mla_queries(qn, qr, h, qgn, qgr, cos, sin):
    lane_grp = (lax.broadcasted_iota(jnp.int32, (1, LANE), 1) // (MLA_ROPE // 2)) % 2
    qn = qn.astype(jnp.float32)
    qt = jnp.where(lane_grp == h % 2, qr.astype(jnp.float32), 0.0)
    ms = (jnp.sum(qn * qn, axis=-1, keepdims=True) + jnp.sum(qt * qt, axis=-1, keepdims=True)) / MLA_QK
    inv = lax.rsqrt(ms + EPS)
    qt = qt * inv * qgr
    if cos is not None:
        qt = qt * cos + pltpu.roll(qt, 64, 1) * sin
    return (jnp.concatenate([qn * inv * qgn, qt], axis=-1) * (MLA_QK ** -0.5 * LOG2E)).astype(jnp.bfloat16)


def _mla_keys_values(ckv, kr, kvg, kgn, kgr, wuk, wuv, cos, sin):
    c = [cj.astype(jnp.float32) for cj in ckv]
    ms = sum(jnp.sum(cj * cj, axis=-1, keepdims=True) for cj in c) / MLA_KV_RANK
    inv = lax.rsqrt(ms + EPS)
    cn = jnp.concatenate([c[j] * inv * kvg[j:j + 1, :] for j in range(4)], axis=-1).astype(jnp.bfloat16)
    kn = jnp.dot(cn, wuk, preferred_element_type=jnp.float32)
    vv = jnp.dot(cn, wuv, preferred_element_type=jnp.float32)
    kr = kr.astype(jnp.float32)
    kr_ss = 0.5 * jnp.sum(kr * kr, axis=-1, keepdims=True)
    keys, vals = [], []
    for h in range(N_HEADS):
        kh = kn[:, h * LANE:(h + 1) * LANE]
        inv_h = lax.rsqrt((jnp.sum(kh * kh, axis=-1, keepdims=True) + kr_ss) / MLA_QK + EPS)
        kt = kr * inv_h * kgr
        if cos is not None:
            kt = kt * cos + pltpu.roll(kt, 64, 1) * sin
        keys.append(jnp.concatenate([kh * inv_h * kgn, kt], axis=-1).astype(jnp.bfloat16))
        vals.append(vv[:, h * LANE:(h + 1) * LANE].astype(jnp.bfloat16))
    return keys, vals


def _gqa_prep(kl_ref, kc_ref, vl_ref, vc_ref, ck_ref, sk_ref, kg_ref, kp_ref, vp_ref, ctx_len):
    for kv in range(2):
        kp_ref[kv, 0:ctx_len, :] = _norm_rope(kc_ref[kv].astype(jnp.float32), kg_ref[...]).astype(jnp.bfloat16)
        kp_ref[kv, ctx_len:, :] = _norm_rope(kl_ref[kv].astype(jnp.float32), kg_ref[...],
                                             ck_ref[...], sk_ref[...]).astype(jnp.bfloat16)
        vp_ref[kv, 0:ctx_len, :] = _with_ones(vc_ref[kv])
        vp_ref[kv, ctx_len:, :] = _with_ones(vl_ref[kv])


def _attn_a_kernel(q_ref, kl_ref, kc_ref, vl_ref, vc_ref, g_ref, cq_ref, sq_ref, ck_ref, sk_ref, qg_ref, kg_ref,
                   o_ref, kp_ref, vp_ref, *, ctx_len):
    tq = q_ref.shape[1]

    @pl.when(pl.program_id(1) == 0)
    def _():
        _gqa_prep(kl_ref, kc_ref, vl_ref, vc_ref, ck_ref, sk_ref, kg_ref, kp_ref, vp_ref, ctx_len)

    for kv in range(2):
        q2 = _gqa_queries(lambda h: q_ref[h], kv, qg_ref[...], cq_ref[...], sq_ref[...])
        acc, _ = _softmax_pv(q2, [(kp_ref[kv, lo:hi, :], vp_ref[kv, lo:hi, :], None)
                                  for lo, hi in _key_spans(kp_ref.shape[1])])
        o = _normalise(acc)
        for j in range(2):
            _gated_store(o_ref, (2 * kv + j) * LANE, o[j * tq:(j + 1) * tq], g_ref[2 * kv + j])


def _attn_c_kernel(sink_ref, q_ref, kl_ref, kc_ref, vl_ref, vc_ref, g_ref, cq_ref, sq_ref, ck_ref, sk_ref,
                   qg_ref, kg_ref, o_ref, kp_ref, vp_ref, *, ctx_len):
    i = pl.program_id(1)
    tq = q_ref.shape[1]
    n_lat = kl_ref.shape[1]

    @pl.when(i == 0)
    def _():
        _gqa_prep(kl_ref, kc_ref, vl_ref, vc_ref, ck_ref, sk_ref, kg_ref, kp_ref, vp_ref, ctx_len)

    q0 = i * tq
    ks = jnp.clip(q0 - WINDOW, 0, n_lat - WIN_SPAN)
    row0 = pl.multiple_of(ctx_len + ks, LANE)
    qi = lax.broadcasted_iota(jnp.int32, (tq, WIN_SPAN), 0)
    ki = lax.broadcasted_iota(jnp.int32, (tq, WIN_SPAN), 1)
    wmask = jnp.where(jnp.abs((qi - ki) + (q0 - ks)) <= WINDOW, 0.0, NEG)
    head_row = lax.broadcasted_iota(jnp.int32, (2 * tq, 1), 0) < tq
    for kv in range(2):
        q2 = _gqa_queries(lambda h: q_ref[h], kv, qg_ref[...], cq_ref[...], sq_ref[...])
        sink = jnp.where(head_row, sink_ref[2 * kv], sink_ref[2 * kv + 1]) * LOG2E
        pieces = [(kp_ref[kv, 0:ctx_len, :], vp_ref[kv, 0:ctx_len, :], None),
                  (kp_ref[kv, pl.ds(row0, WIN_SPAN), :], vp_ref[kv, pl.ds(row0, WIN_SPAN), :], wmask)]
        acc, m = _softmax_pv(q2, pieces, m=sink)
        o = _normalise(acc, jnp.exp2(sink - m))
        for j in range(2):
            _gated_store(o_ref, (2 * kv + j) * LANE, o[j * tq:(j + 1) * tq], g_ref[2 * kv + j])


def _attn_gqa(proj_x, proj_c, cmap, tabs, qg, kg, sink, chunks):
    cq0, ck0, cv0, cg0 = chunks
    bsz, _, n_lat, _ = proj_x.shape
    ctx_len = proj_c.shape[2]
    t = ctx_len + n_lat
    cos, sin = tabs
    row = lambda i: i
    zero = lambda i: 0
    in_specs = [_chunk_spec(4, TQ, cq0, row),
                _chunk_spec(2, n_lat, ck0, zero), _chunk_spec(2, ctx_len, cmap(ck0), zero),
                _chunk_spec(2, n_lat, cv0, zero), _chunk_spec(2, ctx_len, cmap(cv0), zero),
                _chunk_spec(4, TQ, cg0, row),
                pl.BlockSpec((TQ, LANE), lambda b, i: (i, 0)),
                pl.BlockSpec((TQ, LANE), lambda b, i: (i, 0)),
                pl.BlockSpec((n_lat, LANE), lambda b, i: (0, 0)),
                pl.BlockSpec((n_lat, LANE), lambda b, i: (0, 0)),
                pl.BlockSpec((1, LANE), lambda b, i: (0, 0)),
                pl.BlockSpec((1, LANE), lambda b, i: (0, 0))]
    args = [proj_x, proj_x, proj_c, proj_x, proj_c, proj_x, cos, sin, cos, sin, qg, kg]
    if sink is None:
        body = _attn_a_kernel
    else:
        body = _attn_c_kernel
        in_specs = [pl.BlockSpec(memory_space=pltpu.SMEM)] + in_specs
        args = [sink] + args
    return pl.pallas_call(
        functools.partial(body, ctx_len=ctx_len),
        grid=(bsz, n_lat // TQ),
        in_specs=in_specs,
        out_specs=pl.BlockSpec((None, TQ, BRANCH_W), lambda b, i: (b, i, 0)),
        out_shape=jax.ShapeDtypeStruct((bsz, n_lat, BRANCH_W), jnp.bfloat16),
        scratch_shapes=[pltpu.VMEM((2, t, LANE), jnp.bfloat16), pltpu.VMEM((2, t, 2 * LANE), jnp.bfloat16)],
        name="attn_a" if sink is None else "attn_c",
        compiler_params=_params("arbitrary", "arbitrary"),
    )(*args)


def _attn_b_kernel(q_ref, kl_ref, kc_ref, vl_ref, vc_ref, g_ref, bias_ref, qg_ref, kg_ref, o_ref, kp_ref, vp_ref,
                   *, ctx_len):
    i = pl.program_id(1)
    tq = q_ref.shape[1]
    rows = kl_ref.shape[1] // GRID_W
    strip_rows = NA_STRIP // GRID_W
    scale = HEAD_DIM ** -0.5 * LOG2E

    @pl.when(i == 0)
    def _():
        for h in range(N_HEADS):
            kp_ref[h, 0:ctx_len, :] = _norm_rope(kc_ref[h].astype(jnp.float32), kg_ref[...]).astype(jnp.bfloat16)
            kp_ref[h, ctx_len:, :] = _norm_rope(kl_ref[h].astype(jnp.float32), kg_ref[...]).astype(jnp.bfloat16)
            vp_ref[h, 0:ctx_len, :] = _with_ones(vc_ref[h])
            vp_ref[h, ctx_len:, :] = _with_ones(vl_ref[h])

    r0 = i * (tq // GRID_W)
    ss = jnp.clip(r0 - NA_KH // 2, 0, rows - strip_rows)
    row0 = pl.multiple_of(ctx_len + ss * GRID_W, LANE)
    for h in range(N_HEADS):
        q = _norm_rope(q_ref[h].astype(jnp.float32), qg_ref[...], scale=scale).astype(jnp.bfloat16)
        pieces = [(kp_ref[h, 0:ctx_len, :], vp_ref[h, 0:ctx_len, :], None),
                  (kp_ref[h, pl.ds(row0, NA_STRIP), :], vp_ref[h, pl.ds(row0, NA_STRIP), :], bias_ref[h])]
        acc, _ = _softmax_pv(q, pieces)
        _gated_store(o_ref, h * LANE, _normalise(acc), g_ref[h])


def _na_bias_kernel(rp_ref, o_ref, *, rows):
    q_rows, strip_rows = TQ // GRID_W, NA_STRIP // GRID_W
    qc = lax.broadcasted_iota(jnp.int32, (GRID_W, LANE), 0)
    lane = lax.broadcasted_iota(jnp.int32, (GRID_W, LANE), 1)
    kc = lane & (GRID_W - 1)
    cs = jnp.clip(qc - NA_KW // 2, 0, GRID_W - NA_KW)
    col_ok = (kc >= cs) & (kc < cs + NA_KW)
    second = lane >= GRID_W
    for var, r0 in enumerate((0, q_rows, rows - q_rows)):
        ss = min(max(r0 - NA_KH // 2, 0), rows - strip_rows)
        for j in range(q_rows):
            qr = r0 + j
            rs = min(max(qr - NA_KH // 2, 0), rows - NA_KH)
            for p in range(strip_rows // 2):
                kr0 = ss + 2 * p
                ok0, ok1 = rs <= kr0 < rs + NA_KH, rs <= kr0 + 1 < rs + NA_KH
                if ok0 or ok1:
                    e = kr0 - qr + NA_KH - 1
                    x = jnp.broadcast_to(rp_ref[e + 1:e + 2, :], (GRID_W, LANE))
                    band = pltpu.roll(x, LANE - (NA_KW - 1), 1, stride=1, stride_axis=0)
                    row_ok = second if (ok1 and not ok0) else (~second if (ok0 and not ok1) else None)
                    valid = col_ok if row_ok is None else (col_ok & row_ok)
                    tile = jnp.where(valid, band * LOG2E, NEG)
                else:
                    tile = jnp.full((GRID_W, LANE), NEG, jnp.float32)
                o_ref[var, j * GRID_W:(j + 1) * GRID_W, p * LANE:(p + 1) * LANE] = tile


def _na_bias_tables(rpb, n_lat):
    depth, nh, nr, nc = rpb.shape
    z = jnp.zeros((depth, nh, nr + 2, GRID_W), jnp.float32).at[:, :, 1:nr + 1, :nc].set(rpb)
    rp = jnp.concatenate([z[:, :, :-1], z[:, :, 1:]], axis=-1)
    return pl.pallas_call(
        functools.partial(_na_bias_kernel, rows=n_lat // GRID_W),
        grid=(depth, nh),
        in_specs=[pl.BlockSpec((None, None, nr + 1, LANE), lambda l, h: (l, h, 0, 0))],
        out_specs=pl.BlockSpec((None, 3, None, TQ, NA_STRIP), lambda l, h: (l, 0, h, 0, 0)),
        out_shape=jax.ShapeDtypeStruct((depth, 3, nh, TQ, NA_STRIP), jnp.float32),
        name="na_bias",
        compiler_params=_params("arbitrary", "arbitrary"),
    )(rp)


def _attn_b(proj_x, proj_c, cmap, bias, layer, qg, kg):
    bsz, _, n_lat, _ = proj_x.shape
    ctx_len = proj_c.shape[2]
    t = ctx_len + n_lat
    nq = n_lat // TQ
    row = lambda i: i
    zero = lambda i: 0

    def variant(i):
        return jnp.where(i == 0, 0, jnp.where(i == nq - 1, 2, 1))

    return pl.pallas_call(
        functools.partial(_attn_b_kernel, ctx_len=ctx_len),
        grid=(bsz, nq),
        in_specs=[_chunk_spec(4, TQ, B_Q, row),
                  _chunk_spec(4, n_lat, B_K, zero), _chunk_spec(4, ctx_len, cmap(B_K), zero),
                  _chunk_spec(4, n_lat, B_V, zero), _chunk_spec(4, ctx_len, cmap(B_V), zero),
                  _chunk_spec(4, TQ, B_G, row),
                  pl.BlockSpec((None, None, N_HEADS, TQ, NA_STRIP), lambda b, i: (layer, variant(i), 0, 0, 0)),
                  pl.BlockSpec((1, LANE), lambda b, i: (0, 0)),
                  pl.BlockSpec((1, LANE), lambda b, i: (0, 0))],
        out_specs=pl.BlockSpec((None, TQ, BRANCH_W), lambda b, i: (b, i, 0)),
        out_shape=jax.ShapeDtypeStruct((bsz, n_lat, BRANCH_W), jnp.bfloat16),
        scratch_shapes=[pltpu.VMEM((N_HEADS, t, LANE), jnp.bfloat16),
                        pltpu.VMEM((N_HEADS, t, 2 * LANE), jnp.bfloat16)],
        name="attn_b",
        compiler_params=_params("arbitrary", "arbitrary"),
    )(proj_x, proj_x, proj_c, proj_x, proj_c, proj_x, bias, qg, kg)


def _attn_d_kernel(qn_ref, qr_ref, ckvl_ref, ckvc_ref, krl_ref, krc_ref, g_ref, cq_ref, sq_ref, ck_ref, sk_ref,
                   qgn_ref, qgr_ref, kgn_ref, kgr_ref, kvg_ref, wuk_ref, wuv_ref,
                   o_ref, kp_ref, vp_ref, *, ctx_len):
    n_lat = ckvl_ref.shape[1]

    @pl.when(pl.program_id(1) == 0)
    def _():
        def fill(dst, ckv_ref, kr_ref, src, n, cos, sin):
            keys, vals = _mla_keys_values([ckv_ref[j, src:src + n, :] for j in range(4)], kr_ref[0, src:src + n, :],
                                          kvg_ref[...], kgn_ref[...], kgr_ref[...], wuk_ref[...], wuv_ref[...],
                                          cos, sin)
            for h in range(N_HEADS):
                kp_ref[h, dst:dst + n, :] = keys[h]
                vp_ref[h, dst:dst + n, :] = _with_ones(vals[h])

        fill(0, ckvc_ref, krc_ref, 0, ctx_len, None, None)
        rc = 512
        for r in range(0, n_lat, rc):
            fill(ctx_len + r, ckvl_ref, krl_ref, r, rc, ck_ref[r:r + rc, :], sk_ref[r:r + rc, :])

    for h in range(N_HEADS):
        q = _mla_queries(qn_ref[h], qr_ref[h // 2], h, qgn_ref[...], qgr_ref[...], cq_ref[...], sq_ref[...])
        acc, _ = _softmax_pv(q, [(kp_ref[h, lo:hi, :], vp_ref[h, lo:hi, :], None)
                                 for lo, hi in _key_spans(kp_ref.shape[1])])
        _gated_store(o_ref, h * LANE, _normalise(acc), g_ref[h])


def _attn_d(proj_x, proj_c, cmap, tabs, gains, wuk, wuv, layer):
    bsz, _, n_lat, _ = proj_x.shape
    ctx_len = proj_c.shape[2]
    t = ctx_len + n_lat
    row = lambda i: i
    zero = lambda i: 0
    cos, sin = tabs
    qgn, qgr, kgn, kgr, kvg = gains
    vec = pl.BlockSpec((1, LANE), lambda b, i: (0, 0))
    wspec = pl.BlockSpec((None,) + wuk.shape[1:], lambda b, i: (layer, 0, 0))
    return pl.pallas_call(
        functools.partial(_attn_d_kernel, ctx_len=ctx_len),
        grid=(bsz, n_lat // TQ),
        in_specs=[_chunk_spec(4, TQ, D_QN, row), _chunk_spec(2, TQ, D_QR, row),
                  _chunk_spec(4, n_lat, D_CKV, zero), _chunk_spec(4, ctx_len, cmap(D_CKV), zero),
                  _chunk_spec(1, n_lat, D_KR, zero), _chunk_spec(1, ctx_len, cmap(D_KR), zero),
                  _chunk_spec(4, TQ, D_G, row),
                  pl.BlockSpec((TQ, LANE), lambda b, i: (i, 0)),
                  pl.BlockSpec((TQ, LANE), lambda b, i: (i, 0)),
                  pl.BlockSpec((n_lat, LANE), lambda b, i: (0, 0)),
                  pl.BlockSpec((n_lat, LANE), lambda b, i: (0, 0)),
                  vec, vec, vec, vec,
                  pl.BlockSpec((4, LANE), lambda b, i: (0, 0)),
                  wspec, wspec],
        out_specs=pl.BlockSpec((None, TQ, BRANCH_W), lambda b, i: (b, i, 0)),
        out_shape=jax.ShapeDtypeStruct((bsz, n_lat, BRANCH_W), jnp.bfloat16),
        scratch_shapes=[pltpu.VMEM((N_HEADS, t, 2 * LANE), jnp.bfloat16),
                        pltpu.VMEM((N_HEADS, t, 2 * LANE), jnp.bfloat16)],
        name="attn_d",
        compiler_params=_params("arbitrary", "arbitrary"),
    )(proj_x, proj_x, proj_x, proj_c, proj_x, proj_c, proj_x, cos, sin, cos, sin,
      qgn, qgr, kgn, kgr, kvg, wuk, wuv)


def _attn_ctx_kernel(sink_ref, pc_ref, gq_ref, gk_ref, dg_ref, kvg_ref, wuk_ref, wuv_ref, o_ref):
    tq = pc_ref.shape[1]
    scale = HEAD_DIM ** -0.5 * LOG2E

    for br, (cq0, ck0, cv0, cg0) in enumerate(((A_Q, A_K, A_V, A_G), (C_Q, C_K, C_V, C_G))):
        gi = 2 * br
        head_row = lax.broadcasted_iota(jnp.int32, (2 * tq, 1), 0) < tq
        for kv in range(2):
            q2 = _gqa_queries(lambda h: pc_ref[cq0 + h], kv, gq_ref[gi:gi + 1, :], None, None)
            k = _norm_rope(pc_ref[ck0 + kv].astype(jnp.float32), gk_ref[gi:gi + 1, :]).astype(jnp.bfloat16)
            vp = _with_ones(pc_ref[cv0 + kv])
            if br == 0:
                acc, _ = _softmax_pv(q2, [(k, vp, None)])
                o = _normalise(acc)
            else:
                sink = jnp.where(head_row, sink_ref[2 * kv], sink_ref[2 * kv + 1]) * LOG2E
                acc, m = _softmax_pv(q2, [(k, vp, None)], m=sink)
                o = _normalise(acc, jnp.exp2(sink - m))
            for j in range(2):
                h = 2 * kv + j
                _gated_store(o_ref, gi * BRANCH_W + h * LANE, o[j * tq:(j + 1) * tq], pc_ref[cg0 + h])

    for h in range(N_HEADS):
        q = _norm_rope(pc_ref[B_Q + h].astype(jnp.float32), gq_ref[1:2, :], scale=scale).astype(jnp.bfloat16)
        k = _norm_rope(pc_ref[B_K + h].astype(jnp.float32), gk_ref[1:2, :]).astype(jnp.bfloat16)
        acc, _ = _softmax_pv(q, [(k, _with_ones(pc_ref[B_V + h]), None)])
        _gated_store(o_ref, BRANCH_W + h * LANE, _normalise(acc), pc_ref[B_G + h])

    keys, vals = _mla_keys_values([pc_ref[D_CKV + j] for j in range(4)], pc_ref[D_KR], kvg_ref[...],
                                  dg_ref[2:3, :], dg_ref[3:4, :], wuk_ref[...], wuv_ref[...], None, None)
    for h in range(N_HEADS):
        q = _mla_queries(pc_ref[D_QN + h], pc_ref[D_QR + h // 2], h, dg_ref[0:1, :], dg_ref[1:2, :], None, None)
        acc, _ = _softmax_pv(q, [(keys[h], _with_ones(vals[h]), None)])
        _gated_store(o_ref, 3 * BRANCH_W + h * LANE, _normalise(acc), pc_ref[D_G + h])


def _attn_ctx(proj_c, sink, gq, gk, dg, kvg, wuk, wuv, layer):
    bsz, nc, ctx_len, _ = proj_c.shape
    full = lambda a: pl.BlockSpec(a.shape, lambda b: (0,) * a.ndim)
    wspec = pl.BlockSpec((None,) + wuk.shape[1:], lambda b: (layer, 0, 0))
    return pl.pallas_call(
        _attn_ctx_kernel,
        grid=(bsz,),
        in_specs=[pl.BlockSpec(memory_space=pltpu.SMEM),
                  pl.BlockSpec((None, nc, ctx_len, LANE), lambda b: (b, 0, 0, 0)),
                  full(gq), full(gk), full(dg), full(kvg), wspec, wspec],
        out_specs=pl.BlockSpec((None, ctx_len, 4 * BRANCH_W), lambda b: (b, 0, 0)),
        out_shape=jax.ShapeDtypeStruct((bsz, ctx_len, 4 * BRANCH_W), jnp.bfloat16),
        name="attn_ctx",
        compiler_params=_params("arbitrary"),
    )(sink, proj_c, gq, gk, dg, kvg, wuk, wuv)


def _outproj_kernel(*refs, n_in, mod_row):
    s_ref, mix_refs, (w_ref, mod_ref, o_ref) = refs[0], refs[1:1 + n_in], refs[1 + n_in:]
    d = s_ref.shape[1]
    acc = None
    for k in range(n_in):
        part = jnp.dot(mix_refs[k][...], w_ref[k], preferred_element_type=jnp.float32)
        acc = part if acc is None else acc + part
    mrow = pl.program_id(0) if mod_row is None else mod_row
    gate = mod_ref[pl.ds(mrow, 1), 2 * d:3 * d]
    o_ref[...] = s_ref[...] + gate * acc


def _outproj(stream, mixes, w, layer, mod, *, mod_row, tm):
    bsz, r, d = stream.shape
    n_in = len(mixes)
    wk = w.reshape(w.shape[0], n_in, w.shape[1] // n_in, d)
    mix_specs = [pl.BlockSpec((None, tm, m.shape[2]), lambda b, i: (b, i, 0)) for m in mixes]
    return pl.pallas_call(
        functools.partial(_outproj_kernel, n_in=n_in, mod_row=mod_row),
        grid=(bsz, r // tm),
        in_specs=[pl.BlockSpec((None, tm, d), lambda b, i: (b, i, 0))] + mix_specs + [
            pl.BlockSpec((None,) + wk.shape[1:], lambda b, i: (layer, 0, 0, 0)),
            pl.BlockSpec(mod.shape, lambda b, i: (0, 0))],
        out_specs=pl.BlockSpec((None, tm, d), lambda b, i: (b, i, 0)),
        out_shape=jax.ShapeDtypeStruct((bsz, r, d), jnp.float32),
        name="outproj",
        compiler_params=_params("arbitrary", "arbitrary"),
    )(stream, *mixes, wk, mod)


def _permute_w_d(w_t):
    sl = lambda a, n: w_t[:, a:a + n, :]
    qd, half = ABC_COLS, MLA_ROPE // 2
    pieces = [sl(qd + MLA_QK * h, MLA_NOPE) for h in range(N_HEADS)]
    pieces += [sl(6464, BRANCH_W), sl(5888, MLA_KV_RANK)]
    for pair in range(2):
        ha, hb = qd + MLA_QK * (2 * pair) + MLA_NOPE, qd + MLA_QK * (2 * pair + 1) + MLA_NOPE
        pieces += [sl(ha, half), sl(hb, half), sl(ha + half, half), sl(hb + half, half)]
    pieces += [sl(6400, half), sl(6400, half), sl(6400 + half, half), sl(6400 + half, half)]
    pieces += [jnp.zeros((w_t.shape[0], LANE, w_t.shape[2]), w_t.dtype)]
    return jnp.concatenate(pieces, axis=1)


def _w_in_kernel(w_ref, wd_ref, o_ref, *, n_abc):
    j = pl.program_id(1)

    @pl.when(j < n_abc)
    def _():
        o_ref[...] = w_ref[...].astype(jnp.bfloat16)

    @pl.when(j >= n_abc)
    def _():
        o_ref[...] = wd_ref[...].astype(jnp.bfloat16)


def _prep_w_in(w_in):
    depth, d, _ = w_in.shape
    w_t = jnp.swapaxes(w_in, 1, 2)
    wd = _permute_w_d(w_t)
    n_abc = ABC_COLS // TN_IN
    n_d = wd.shape[1] // TN_IN
    return pl.pallas_call(
        functools.partial(_w_in_kernel, n_abc=n_abc),
        grid=(depth, n_abc + n_d),
        in_specs=[pl.BlockSpec((None, TN_IN, d), lambda l, j: (l, jnp.minimum(j, n_abc - 1), 0)),
                  pl.BlockSpec((None, TN_IN, d), lambda l, j: (l, jnp.maximum(j - n_abc, 0), 0))],
        out_specs=pl.BlockSpec((None, TN_IN, d), lambda l, j: (l, j, 0)),
        out_shape=jax.ShapeDtypeStruct((depth, ABC_COLS + wd.shape[1], d), jnp.bfloat16),
        name="w_in_cast",
        compiler_params=_params("arbitrary", "arbitrary"),
    )(w_t, wd)


def _rope_tables(n_lat, rot_dim):
    tpos = jnp.arange(n_lat)
    row = (tpos // GRID_W).astype(jnp.float32)
    col = (tpos % GRID_W).astype(jnp.float32)
    n_freq = rot_dim // 4
    inv_freq = ROPE_THETA ** (-jnp.arange(n_freq, dtype=jnp.float32) / n_freq)
    ang = jnp.concatenate([row[:, None] * inv_freq, col[:, None] * inv_freq], axis=-1)
    cos, sin = jnp.cos(ang), jnp.sin(ang)
    rep = LANE // rot_dim
    return (jnp.concatenate([cos] * (2 * rep), axis=-1),
            jnp.concatenate([-sin] * rep + [sin] * rep, axis=-1))


def _dup_rope_gain(g):
    half = MLA_ROPE // 2
    r1, r2 = g[MLA_NOPE:MLA_NOPE + half], g[MLA_NOPE + half:]
    return g[None, :MLA_NOPE], jnp.concatenate([r1, r1, r2, r2])[None, :]


def kernel(x, c, ctx, c_ctx, norm_g, w_ada, b_ada, w_in, w_out, a_q_g, a_k_g, b_q_g, b_k_g, b_rpb,
           c_q_g, c_k_g, c_sink, d_q_g, d_k_g, d_kv_g, d_w_uk, d_w_uv):
    bsz, n_lat, d = x.shape
    ctx_len = ctx.shape[1]
    depth = w_in.shape[0]
    assert bsz <= 8 and n_lat % TM_IN == 0 and TM_IN % ctx_len == 0 and ctx_len % LANE == 0

    cc = jnp.zeros((16, d), jnp.float32).at[:bsz].set(c).at[8].set(c_ctx)
    mod = _ada(cc, w_ada, b_ada)
    w_in_p = _prep_w_in(w_in)
    w_out_p = w_out.astype(jnp.bfloat16)
    wuk, wuv = d_w_uk.astype(jnp.bfloat16), d_w_uv.astype(jnp.bfloat16)
    tabs_h = _rope_tables(n_lat, HEAD_DIM)
    tabs_r = _rope_tables(n_lat, MLA_ROPE)
    na_bias = _na_bias_tables(b_rpb, n_lat)

    hc = ctx.astype(x.dtype)
    for l in range(depth):
        with_ctx = l < depth - 1
        ng = norm_g[l][None, :]
        d_gains = _dup_rope_gain(d_q_g[l]) + _dup_rope_gain(d_k_g[l])
        kvg = d_kv_g[l].reshape(4, LANE)
        proj_x = _inproj(x, mod[l], ng, w_in_p, l, mod_row=None)
        if with_ctx:
            proj_c, cmap = _inproj(hc, mod[l], ng, w_in_p, l, mod_row=8), (lambda ch: ch)
        else:
            proj_c = _inproj(hc, mod[l], ng, w_in_p, l, mod_row=8, tile_stride=2)
            cmap = lambda ch: (ch // (2 * CPT)) * CPT + ch % CPT
        oa = _attn_gqa(proj_x, proj_c, cmap, tabs_h, a_q_g[l][None, :], a_k_g[l][None, :], None,
                       (A_Q, A_K, A_V, A_G))
        ob = _attn_b(proj_x, proj_c, cmap, na_bias, l, b_q_g[l][None, :], b_k_g[l][None, :])
        oc = _attn_gqa(proj_x, proj_c, cmap, tabs_h, c_q_g[l][None, :], c_k_g[l][None, :], c_sink[l],
                       (C_Q, C_K, C_V, C_G))
        od = _attn_d(proj_x, proj_c, cmap, tabs_r, d_gains + (kvg,), wuk, wuv, l)
        if with_ctx:
            o_ctx = _attn_ctx(proj_c, c_sink[l], jnp.stack([a_q_g[l], b_q_g[l], c_q_g[l]]),
                              jnp.stack([a_k_g[l], b_k_g[l], c_k_g[l]]), jnp.concatenate(d_gains, axis=0),
                              kvg, wuk, wuv, l)
            hc = _outproj(hc, (o_ctx,), w_out_p, l, mod[l], mod_row=8, tm=ctx_len)
        x = _outproj(x, (oa, ob, oc, od), w_out_p, l, mod[l], mod_row=None, tm=TM_OUT)
    return x
```

```python
import functools

import jax
import jax.numpy as jnp
from jax import lax
from jax.experimental import pallas as pl
from jax.experimental.pallas import tpu as pltpu

GRID_W = 64
HEAD_DIM = 128
BRANCH_W = 512
N_HEADS = 4
NA_KH = 8
NA_KW = 16
WINDOW = 128
MLA_KV_RANK = 512
MLA_NOPE = 128
MLA_ROPE = 64
MLA_QK = MLA_NOPE + MLA_ROPE
ROPE_THETA = 10000.0
EPS = 1e-6
NEG = -1e30
LOG2E = 1.4426950408889634

LANE = 128
TQ = 512
TM_IN = 1024
TN_IN = 1024
TM_OUT = 512
CPT = TN_IN // LANE
N_CHUNKS = 56
ABC_COLS = 5120
NA_STRIP = (NA_KH + TQ // GRID_W) * GRID_W
WIN_SPAN = TQ + 2 * WINDOW
VMEM_LIMIT = 48 * 1024 * 1024

A_Q, A_K, A_V, A_G = 0, 4, 6, 8
B_Q, B_K, B_V, B_G = 12, 16, 20, 24
C_Q, C_K, C_V, C_G = 28, 32, 34, 36
D_QN, D_G, D_CKV, D_QR, D_KR = 40, 44, 48, 52, 54

_NT = (((1,), (1,)), ((), ()))


def _params(*sem):
    return pltpu.CompilerParams(dimension_semantics=sem, vmem_limit_bytes=VMEM_LIMIT)


def _silu(x):
    return x * jax.nn.sigmoid(x)


def _ada_kernel(c_ref, w_ref, b_ref, o_ref):
    a = _silu(c_ref[...]).astype(jnp.bfloat16)
    o_ref[...] = jnp.dot(a, w_ref[...].astype(jnp.bfloat16),
                         preferred_element_type=jnp.float32) + b_ref[...]


def _ada(cc, w_ada, b_ada):
    depth, d, n = w_ada.shape
    tn = 512
    return pl.pallas_call(
        _ada_kernel,
        grid=(depth, n // tn),
        in_specs=[pl.BlockSpec((16, d), lambda l, j: (0, 0)),
                  pl.BlockSpec((None, d, tn), lambda l, j: (l, 0, j)),
                  pl.BlockSpec((None, 1, tn), lambda l, j: (l, 0, j))],
        out_specs=pl.BlockSpec((None, 16, tn), lambda l, j: (l, 0, j)),
        out_shape=jax.ShapeDtypeStruct((depth, 16, n), jnp.float32),
        name="ada",
        compiler_params=_params("arbitrary", "arbitrary"),
    )(cc, w_ada, b_ada.reshape(depth, 1, n))


def _inproj_kernel(x_ref, mod_ref, ng_ref, w_ref, o_ref, h_ref, *, mod_row):
    j = pl.program_id(2)
    nb, r, d = x_ref.shape
    mrow = pl.program_id(0) if mod_row is None else mod_row

    @pl.when(j == 0)
    def _():
        sh = mod_ref[pl.ds(mrow, 1), 0:d]
        gain = ng_ref[...] * (1.0 + mod_ref[pl.ds(mrow, 1), d:2 * d])
        for s in range(nb):
            x = x_ref[s]
            inv = lax.rsqrt(jnp.mean(x * x, axis=-1, keepdims=True) + EPS)
            h_ref[s * r:(s + 1) * r, :] = (x * inv * gain + sh).astype(jnp.bfloat16)

    acc = lax.dot_general(h_ref[...], w_ref[...], _NT, preferred_element_type=jnp.float32)
    for s in range(nb):
        for c in range(acc.shape[1] // LANE):
            o_ref[s, c] = acc[s * r:(s + 1) * r, c * LANE:(c + 1) * LANE].astype(jnp.bfloat16)


def _inproj(stream, mod, ng, w, layer, *, mod_row, tile_stride=1):
    bsz, r, d = stream.shape
    nb = min(bsz, max(1, TM_IN // r))
    rows = min(r, TM_IN)
    assert bsz % nb == 0 and r % rows == 0
    n_tiles = -(-(w.shape[1] // TN_IN) // tile_stride)
    return pl.pallas_call(
        functools.partial(_inproj_kernel, mod_row=mod_row),
        grid=(bsz // nb, r // rows, n_tiles),
        in_specs=[pl.BlockSpec((nb, rows, d), lambda b, i, j: (b, i, 0)),
                  pl.BlockSpec(mod.shape, lambda b, i, j: (0, 0)),
                  pl.BlockSpec((1, d), lambda b, i, j: (0, 0)),
                  pl.BlockSpec((None, TN_IN, d), lambda b, i, j: (layer, j * tile_stride, 0))],
        out_specs=pl.BlockSpec((nb, CPT, rows, LANE), lambda b, i, j: (b, j, i, 0)),
        out_shape=jax.ShapeDtypeStruct((bsz, n_tiles * CPT, r, LANE), jnp.bfloat16),
        scratch_shapes=[pltpu.VMEM((nb * rows, d), jnp.bfloat16)],
        name="inproj",
        compiler_params=_params("arbitrary", "arbitrary", "arbitrary"),
    )(stream, mod, ng, w)


def _norm_rope(x, gain, cos=None, sin=None, scale=None):
    y = x * lax.rsqrt(jnp.mean(x * x, axis=-1, keepdims=True) + EPS) * gain
    if cos is not None:
        y = y * cos + pltpu.roll(y, 64, 1) * sin
    if scale is not None:
        y = y * scale
    return y


def _chunk_spec(n, rows, chunk0, row_fn):
    return pl.BlockSpec((None, n, rows, LANE), lambda b, i: (b, chunk0 // n, row_fn(i), 0))


def _gated_store(o_ref, col, o, g):
    g = g.astype(jnp.float32)
    o_ref[:, col:col + LANE] = (o * _silu(g)).astype(o_ref.dtype)


def _with_ones(v):
    lane = lax.broadcasted_iota(jnp.int32, v.shape, 1)
    return jnp.concatenate([v, jnp.where(lane == 0, 1.0, 0.0).astype(v.dtype)], axis=-1)


def _softmax_pv(q, pieces, m=None):
    acc = None
    for k, vp, bias in pieces:
        s = lax.dot_general(q, k, _NT, preferred_element_type=jnp.float32)
        if bias is not None:
            rep, (rows, n) = s.shape[0] // bias.shape[0], bias.shape
            s = (s.reshape(rep, rows, n) + bias[None]).reshape(rep * rows, n)
        mc = jnp.max(s, axis=-1, keepdims=True)
        mn = mc if m is None else jnp.maximum(m, mc)
        p = jnp.exp2(s - mn).astype(jnp.bfloat16)
        t = jnp.dot(p, vp, preferred_element_type=jnp.float32)
        acc = t if acc is None else acc * jnp.exp2(m - mn) + t
        m = mn
    return acc, m


def _normalise(acc, extra=None):
    l = acc[:, LANE:LANE + 1]
    if extra is not None:
        l = l + extra
    return acc[:, 0:LANE] / l


def _key_spans(t):
    cut = (t // 2 + 255) // 256 * 256
    return [(0, cut), (cut, t)]


def _prepare_queries(src_ref, qp_ref, slot, blk, qg_ref, cos_ref=None, sin_ref=None):
    tq = src_ref.shape[1]
    cos = sin = None
    if cos_ref is not None:
        r0 = pl.multiple_of(blk * tq, tq)
        cos, sin = cos_ref[pl.ds(r0, tq), :], sin_ref[pl.ds(r0, tq), :]
    for h in range(N_HEADS):
        qp_ref[slot, h] = _norm_rope(src_ref[h].astype(jnp.float32), qg_ref[...], cos, sin,
                                     HEAD_DIM ** -0.5 * LOG2E).astype(jnp.bfloat16)


def _next_block(i):
    return jnp.minimum(i + 1, pl.num_programs(1) - 1)


def _load_stacked_queries(qp_ref, slot):
    return [jnp.concatenate([qp_ref[slot, 2 * kv], qp_ref[slot, 2 * kv + 1]], axis=0) for kv in range(2)]


def _mla_queries(qn, qr, h, qgn, qgr, cos, sin):
    lane_grp = (lax.broadcasted_iota(jnp.int32, (1, LANE), 1) // (MLA_ROPE // 2)) % 2
    qn = qn.astype(jnp.float32)
    qt = jnp.where(lane_grp == h % 2, qr.astype(jnp.float32), 0.0)
    ms = (jnp.sum(qn * qn, axis=-1, keepdims=True) + jnp.sum(qt * qt, axis=-1, keepdims=True)) / MLA_QK
    inv = lax.rsqrt(ms + EPS)
    qt = qt * inv * qgr
    if cos is not None:
        qt = qt * cos + pltpu.roll(qt, 64, 1) * sin
    return (jnp.concatenate([qn * inv * qgn, qt], axis=-1) * (MLA_QK ** -0.5 * LOG2E)).astype(jnp.bfloat16)


def _mla_keys_values(ckv, kr, kvg, kgn, kgr, wuk, wuv, cos, sin):
    c = [cj.astype(jnp.float32) for cj in ckv]
    ms = sum(jnp.sum(cj * cj, axis=-1, keepdims=True) for cj in c) / MLA_KV_RANK
    inv = lax.rsqrt(ms + EPS)
    cn = jnp.concatenate([c[j] * inv * kvg[j:j + 1, :] for j in range(4)], axis=-1).astype(jnp.bfloat16)
    kn = jnp.dot(cn, wuk, preferred_element_type=jnp.float32)
    vv = jnp.dot(cn, wuv, preferred_element_type=jnp.float32)
    kr = kr.astype(jnp.float32)
    kr_ss = 0.5 * jnp.sum(kr * kr, axis=-1, keepdims=True)
    keys, vals = [], []
    for h in range(N_HEADS):
        kh = kn[:, h * LANE:(h + 1) * LANE]
        inv_h = lax.rsqrt((jnp.sum(kh * kh, axis=-1, keepdims=True) + kr_ss) / MLA_QK + EPS)
        kt = kr * inv_h * kgr
        if cos is not None:
            kt = kt * cos + pltpu.roll(kt, 64, 1) * sin
        keys.append(jnp.concatenate([kh * inv_h * kgn, kt], axis=-1).astype(jnp.bfloat16))
        vals.append(vv[:, h * LANE:(h + 1) * LANE].astype(jnp.bfloat16))
    return keys, vals


def _gqa_prep(kl_ref, kc_ref, vl_ref, vc_ref, ck_ref, sk_ref, kg_ref, kp_ref, vp_ref, ctx_len):
    for kv in range(2):
        kp_ref[kv, 0:ctx_len, :] = _norm_rope(kc_ref[kv].astype(jnp.float32), kg_ref[...]).astype(jnp.bfloat16)
        kp_ref[kv, ctx_len:, :] = _norm_rope(kl_ref[kv].astype(jnp.float32), kg_ref[...],
                                             ck_ref[...], sk_ref[...]).astype(jnp.bfloat16)
        vp_ref[kv, 0:ctx_len, :] = _with_ones(vc_ref[kv])
        vp_ref[kv, ctx_len:, :] = _with_ones(vl_ref[kv])


def _attn_a_kernel(q_ref, qn_ref, kl_ref, kc_ref, vl_ref, vc_ref, g_ref, ck_ref, sk_ref, qg_ref, kg_ref,
                   o_ref, kp_ref, vp_ref, qp_ref, *, ctx_len):
    i = pl.program_id(1)
    tq = q_ref.shape[1]

    @pl.when(i == 0)
    def _():
        _gqa_prep(kl_ref, kc_ref, vl_ref, vc_ref, ck_ref, sk_ref, kg_ref, kp_ref, vp_ref, ctx_len)
        _prepare_queries(q_ref, qp_ref, 0, 0, qg_ref, ck_ref, sk_ref)

    qs = _load_stacked_queries(qp_ref, i % 2)
    _prepare_queries(qn_ref, qp_ref, 1 - i % 2, _next_block(i), qg_ref, ck_ref, sk_ref)
    for kv in range(2):
        q2 = qs[kv]
        acc, _ = _softmax_pv(q2, [(kp_ref[kv, lo:hi, :], vp_ref[kv, lo:hi, :], None)
                                  for lo, hi in _key_spans(kp_ref.shape[1])])
        o = _normalise(acc)
        for j in range(2):
            _gated_store(o_ref, (2 * kv + j) * LANE, o[j * tq:(j + 1) * tq], g_ref[2 * kv + j])


def _attn_c_kernel(sink_ref, q_ref, qn_ref, kl_ref, kc_ref, vl_ref, vc_ref, g_ref, ck_ref, sk_ref,
                   qg_ref, kg_ref, o_ref, kp_ref, vp_ref, qp_ref, *, ctx_len):
    i = pl.program_id(1)
    tq = q_ref.shape[1]
    n_lat = kl_ref.shape[1]

    @pl.when(i == 0)
    def _():
        _gqa_prep(kl_ref, kc_ref, vl_ref, vc_ref, ck_ref, sk_ref, kg_ref, kp_ref, vp_ref, ctx_len)
        _prepare_queries(q_ref, qp_ref, 0, 0, qg_ref, ck_ref, sk_ref)

    qs = _load_stacked_queries(qp_ref, i % 2)
    _prepare_queries(qn_ref, qp_ref, 1 - i % 2, _next_block(i), qg_ref, ck_ref, sk_ref)

    q0 = i * tq
    ks = jnp.clip(q0 - WINDOW, 0, n_lat - WIN_SPAN)
    row0 = pl.multiple_of(ctx_len + ks, LANE)
    qi = lax.broadcasted_iota(jnp.int32, (tq, WIN_SPAN), 0)
    ki = lax.broadcasted_iota(jnp.int32, (tq, WIN_SPAN), 1)
    wmask = jnp.where(jnp.abs((qi - ki) + (q0 - ks)) <= WINDOW, 0.0, NEG)
    head_row = lax.broadcasted_iota(jnp.int32, (2 * tq, 1), 0) < tq
    for kv in range(2):
        q2 = qs[kv]
        sink = jnp.where(head_row, sink_ref[2 * kv], sink_ref[2 * kv + 1]) * LOG2E
        pieces = [(kp_ref[kv, 0:ctx_len, :], vp_ref[kv, 0:ctx_len, :], None),
                  (kp_ref[kv, pl.ds(row0, WIN_SPAN), :], vp_ref[kv, pl.ds(row0, WIN_SPAN), :], wmask)]
        acc, m = _softmax_pv(q2, pieces, m=sink)
        o = _normalise(acc, jnp.exp2(sink - m))
        for j in range(2):
            _gated_store(o_ref, (2 * kv + j) * LANE, o[j * tq:(j + 1) * tq], g_ref[2 * kv + j])


def _attn_gqa(proj_x, proj_c, cmap, tabs, qg, kg, sink, chunks):
    cq0, ck0, cv0, cg0 = chunks
    bsz, _, n_lat, _ = proj_x.shape
    ctx_len = proj_c.shape[2]
    t = ctx_len + n_lat
    cos, sin = tabs
    nq = n_lat // TQ
    row = lambda i: i
    nxt = lambda i: jnp.minimum(i + 1, nq - 1)
    zero = lambda i: 0
    in_specs = [_chunk_spec(4, TQ, cq0, row), _chunk_spec(4, TQ, cq0, nxt),
                _chunk_spec(2, n_lat, ck0, zero), _chunk_spec(2, ctx_len, cmap(ck0), zero),
                _chunk_spec(2, n_lat, cv0, zero), _chunk_spec(2, ctx_len, cmap(cv0), zero),
                _chunk_spec(4, TQ, cg0, row),
                pl.BlockSpec((n_lat, LANE), lambda b, i: (0, 0)),
                pl.BlockSpec((n_lat, LANE), lambda b, i: (0, 0)),
                pl.BlockSpec((1, LANE), lambda b, i: (0, 0)),
                pl.BlockSpec((1, LANE), lambda b, i: (0, 0))]
    args = [proj_x, proj_x, proj_x, proj_c, proj_x, proj_c, proj_x, cos, sin, qg, kg]
    if sink is None:
        body = _attn_a_kernel
    else:
        body = _attn_c_kernel
        in_specs = [pl.BlockSpec(memory_space=pltpu.SMEM)] + in_specs
        args = [sink] + args
    return pl.pallas_call(
        functools.partial(body, ctx_len=ctx_len),
        grid=(bsz, n_lat // TQ),
        in_specs=in_specs,
        out_specs=pl.BlockSpec((None, TQ, BRANCH_W), lambda b, i: (b, i, 0)),
        out_shape=jax.ShapeDtypeStruct((bsz, n_lat, BRANCH_W), jnp.bfloat16),
        scratch_shapes=[pltpu.VMEM((2, t, LANE), jnp.bfloat16), pltpu.VMEM((2, t, 2 * LANE), jnp.bfloat16),
                        pltpu.VMEM((2, N_HEADS, TQ, LANE), jnp.bfloat16)],
        name="attn_a" if sink is None else "attn_c",
        compiler_params=_params("arbitrary", "arbitrary"),
    )(*args)


def _attn_b_kernel(q_ref, qn_ref, kl_ref, kc_ref, vl_ref, vc_ref, g_ref, bias_ref, qg_ref, kg_ref,
                   o_ref, kp_ref, vp_ref, qp_ref, *, ctx_len):
    i = pl.program_id(1)
    tq = q_ref.shape[1]
    rows = kl_ref.shape[1] // GRID_W
    strip_rows = NA_STRIP // GRID_W

    @pl.when(i == 0)
    def _():
        for h in range(N_HEADS):
            kp_ref[h, 0:ctx_len, :] = _norm_rope(kc_ref[h].astype(jnp.float32), kg_ref[...]).astype(jnp.bfloat16)
            kp_ref[h, ctx_len:, :] = _norm_rope(kl_ref[h].astype(jnp.float32), kg_ref[...]).astype(jnp.bfloat16)
            vp_ref[h, 0:ctx_len, :] = _with_ones(vc_ref[h])
            vp_ref[h, ctx_len:, :] = _with_ones(vl_ref[h])
        _prepare_queries(q_ref, qp_ref, 0, 0, qg_ref)

    qs = [qp_ref[i % 2, h] for h in range(N_HEADS)]
    _prepare_queries(qn_ref, qp_ref, 1 - i % 2, _next_block(i), qg_ref)
    r0 = i * (tq // GRID_W)
    ss = jnp.clip(r0 - NA_KH // 2, 0, rows - strip_rows)
    row0 = pl.multiple_of(ctx_len + ss * GRID_W, LANE)
    for h in range(N_HEADS):
        pieces = [(kp_ref[h, 0:ctx_len, :], vp_ref[h, 0:ctx_len, :], None),
                  (kp_ref[h, pl.ds(row0, NA_STRIP), :], vp_ref[h, pl.ds(row0, NA_STRIP), :], bias_ref[h])]
        acc, _ = _softmax_pv(qs[h], pieces)
        _gated_store(o_ref, h * LANE, _normalise(acc), g_ref[h])


def _na_bias_kernel(rp_ref, o_ref, *, rows):
    q_rows, strip_rows = TQ // GRID_W, NA_STRIP // GRID_W
    qc = lax.broadcasted_iota(jnp.int32, (GRID_W, LANE), 0)
    lane = lax.broadcasted_iota(jnp.int32, (GRID_W, LANE), 1)
    kc = lane & (GRID_W - 1)
    cs = jnp.clip(qc - NA_KW // 2, 0, GRID_W - NA_KW)
    col_ok = (kc >= cs) & (kc < cs + NA_KW)
    second = lane >= GRID_W
    for var, r0 in enumerate((0, q_rows, rows - q_rows)):
        ss = min(max(r0 - NA_KH // 2, 0), rows - strip_rows)
        for j in range(q_rows):
            qr = r0 + j
            rs = min(max(qr - NA_KH // 2, 0), rows - NA_KH)
            for p in range(strip_rows // 2):
                kr0 = ss + 2 * p
                ok0, ok1 = rs <= kr0 < rs + NA_KH, rs <= kr0 + 1 < rs + NA_KH
                if ok0 or ok1:
                    e = kr0 - qr + NA_KH - 1
                    x = jnp.broadcast_to(rp_ref[e + 1:e + 2, :], (GRID_W, LANE))
                    band = pltpu.roll(x, LANE - (NA_KW - 1), 1, stride=1, stride_axis=0)
                    row_ok = second if (ok1 and not ok0) else (~second if (ok0 and not ok1) else None)
                    valid = col_ok if row_ok is None else (col_ok & row_ok)
                    tile = jnp.where(valid, band * LOG2E, NEG)
                else:
                    tile = jnp.full((GRID_W, LANE), NEG, jnp.float32)
                o_ref[var, j * GRID_W:(j + 1) * GRID_W, p * LANE:(p + 1) * LANE] = tile


def _na_bias_tables(rpb, n_lat):
    depth, nh, nr, nc = rpb.shape
    z = jnp.zeros((depth, nh, nr + 2, GRID_W), jnp.float32).at[:, :, 1:nr + 1, :nc].set(rpb)
    rp = jnp.concatenate([z[:, :, :-1], z[:, :, 1:]], axis=-1)
    return pl.pallas_call(
        functools.partial(_na_bias_kernel, rows=n_lat // GRID_W),
        grid=(depth, nh),
        in_specs=[pl.BlockSpec((None, None, nr + 1, LANE), lambda l, h: (l, h, 0, 0))],
        out_specs=pl.BlockSpec((None, 3, None, TQ, NA_STRIP), lambda l, h: (l, 0, h, 0, 0)),
        out_shape=jax.ShapeDtypeStruct((depth, 3, nh, TQ, NA_STRIP), jnp.float32),
        name="na_bias",
        compiler_params=_params("arbitrary", "arbitrary"),
    )(rp)


def _attn_b(proj_x, proj_c, cmap, bias, layer, qg, kg):
    bsz, _, n_lat, _ = proj_x.shape
    ctx_len = proj_c.shape[2]
    t = ctx_len + n_lat
    nq = n_lat // TQ
    row = lambda i: i
    nxt = lambda i: jnp.minimum(i + 1, nq - 1)
    zero = lambda i: 0

    def variant(i):
        return jnp.where(i == 0, 0, jnp.where(i == nq - 1, 2, 1))

    return pl.pallas_call(
        functools.partial(_attn_b_kernel, ctx_len=ctx_len),
        grid=(bsz, nq),
        in_specs=[_chunk_spec(4, TQ, B_Q, row), _chunk_spec(4, TQ, B_Q, nxt),
                  _chunk_spec(4, n_lat, B_K, zero), _chunk_spec(4, ctx_len, cmap(B_K), zero),
                  _chunk_spec(4, n_lat, B_V, zero), _chunk_spec(4, ctx_len, cmap(B_V), zero),
                  _chunk_spec(4, TQ, B_G, row),
                  pl.BlockSpec((None, None, N_HEADS, TQ, NA_STRIP), lambda b, i: (layer, variant(i), 0, 0, 0)),
                  pl.BlockSpec((1, LANE), lambda b, i: (0, 0)),
                  pl.BlockSpec((1, LANE), lambda b, i: (0, 0))],
        out_specs=pl.BlockSpec((None, TQ, BRANCH_W), lambda b, i: (b, i, 0)),
        out_shape=jax.ShapeDtypeStruct((bsz, n_lat, BRANCH_W), jnp.bfloat16),
        scratch_shapes=[pltpu.VMEM((N_HEADS, t, LANE), jnp.bfloat16),
                        pltpu.VMEM((N_HEADS, t, 2 * LANE), jnp.bfloat16),
                        pltpu.VMEM((2, N_HEADS, TQ, LANE), jnp.bfloat16)],
        name="attn_b",
        compiler_params=_params("arbitrary", "arbitrary"),
    )(proj_x, proj_x, proj_x, proj_c, proj_x, proj_c, proj_x, bias, qg, kg)


def _attn_d_kernel(qn_ref, qr_ref, qn2_ref, qr2_ref, ckvl_ref, ckvc_ref, krl_ref, krc_ref, g_ref, ck_ref, sk_ref,
                   qgn_ref, qgr_ref, kgn_ref, kgr_ref, kvg_ref, wuk_ref, wuv_ref,
                   o_ref, kp_ref, vp_ref, qp_ref, *, ctx_len):
    i = pl.program_id(1)
    n_lat = ckvl_ref.shape[1]
    tq = qn_ref.shape[1]

    def prepare_queries(nope_ref, rope_ref, slot, blk):
        r0 = pl.multiple_of(blk * tq, tq)
        cos, sin = ck_ref[pl.ds(r0, tq), :], sk_ref[pl.ds(r0, tq), :]
        for h in range(N_HEADS):
            qp_ref[slot, h] = _mla_queries(nope_ref[h], rope_ref[h // 2], h, qgn_ref[...], qgr_ref[...], cos, sin)

    @pl.when(i == 0)
    def _():
        prepare_queries(qn_ref, qr_ref, 0, 0)

        def fill(dst, ckv_ref, kr_ref, src, n, cos, sin):
            keys, vals = _mla_keys_values([ckv_ref[j, src:src + n, :] for j in range(4)], kr_ref[0, src:src + n, :],
                                          kvg_ref[...], kgn_ref[...], kgr_ref[...], wuk_ref[...], wuv_ref[...],
                                          cos, sin)
            for h in range(N_HEADS):
                kp_ref[h, dst:dst + n, :] = keys[h]
                vp_ref[h, dst:dst + n, :] = _with_ones(vals[h])

        fill(0, ckvc_ref, krc_ref, 0, ctx_len, None, None)
        rc = 512
        for r in range(0, n_lat, rc):
            fill(ctx_len + r, ckvl_ref, krl_ref, r, rc, ck_ref[r:r + rc, :], sk_ref[r:r + rc, :])

    qs = [qp_ref[i % 2, h] for h in range(N_HEADS)]
    prepare_queries(qn2_ref, qr2_ref, 1 - i % 2, _next_block(i))
    for h in range(N_HEADS):
        acc, _ = _softmax_pv(qs[h], [(kp_ref[h, lo:hi, :], vp_ref[h, lo:hi, :], None)
                                     for lo, hi in _key_spans(kp_ref.shape[1])])
        _gated_store(o_ref, h * LANE, _normalise(acc), g_ref[h])


def _attn_d(proj_x, proj_c, cmap, tabs, gains, wuk, wuv, layer):
    bsz, _, n_lat, _ = proj_x.shape
    ctx_len = proj_c.shape[2]
    t = ctx_len + n_lat
    nq = n_lat // TQ
    row = lambda i: i
    nxt = lambda i: jnp.minimum(i + 1, nq - 1)
    zero = lambda i: 0
    cos, sin = tabs
    qgn, qgr, kgn, kgr, kvg = gains
    vec = pl.BlockSpec((1, LANE), lambda b, i: (0, 0))
    wspec = pl.BlockSpec((None,) + wuk.shape[1:], lambda b, i: (layer, 0, 0))
    return pl.pallas_call(
        functools.partial(_attn_d_kernel, ctx_len=ctx_len),
        grid=(bsz, n_lat // TQ),
        in_specs=[_chunk_spec(4, TQ, D_QN, row), _chunk_spec(2, TQ, D_QR, row),
                  _chunk_spec(4, TQ, D_QN, nxt), _chunk_spec(2, TQ, D_QR, nxt),
                  _chunk_spec(4, n_lat, D_CKV, zero), _chunk_spec(4, ctx_len, cmap(D_CKV), zero),
                  _chunk_spec(1, n_lat, D_KR, zero), _chunk_spec(1, ctx_len, cmap(D_KR), zero),
                  _chunk_spec(4, TQ, D_G, row),
                  pl.BlockSpec((n_lat, LANE), lambda b, i: (0, 0)),
                  pl.BlockSpec((n_lat, LANE), lambda b, i: (0, 0)),
                  vec, vec, vec, vec,
                  pl.BlockSpec((4, LANE), lambda b, i: (0, 0)),
                  wspec, wspec],
        out_specs=pl.BlockSpec((None, TQ, BRANCH_W), lambda b, i: (b, i, 0)),
        out_shape=jax.ShapeDtypeStruct((bsz, n_lat, BRANCH_W), jnp.bfloat16),
        scratch_shapes=[pltpu.VMEM((N_HEADS, t, 2 * LANE), jnp.bfloat16),
                        pltpu.VMEM((N_HEADS, t, 2 * LANE), jnp.bfloat16),
                        pltpu.VMEM((2, N_HEADS, TQ, 2 * LANE), jnp.bfloat16)],
        name="attn_d",
        compiler_params=_params("arbitrary", "arbitrary"),
    )(proj_x, proj_x, proj_x, proj_x, proj_x, proj_c, proj_x, proj_c, proj_x, cos, sin,
      qgn, qgr, kgn, kgr, kvg, wuk, wuv)


def _attn_ctx_kernel(sink_ref, pc_ref, gq_ref, gk_ref, dg_ref, kvg_ref, wuk_ref, wuv_ref, o_ref):
    tq = pc_ref.shape[1]
    scale = HEAD_DIM ** -0.5 * LOG2E

    for br, (cq0, ck0, cv0, cg0) in enumerate(((A_Q, A_K, A_V, A_G), (C_Q, C_K, C_V, C_G))):
        gi = 2 * br
        head_row = lax.broadcasted_iota(jnp.int32, (2 * tq, 1), 0) < tq
        for kv in range(2):
            q2 = jnp.concatenate(
                [_norm_rope(pc_ref[cq0 + 2 * kv + j].astype(jnp.float32), gq_ref[gi:gi + 1, :],
                            scale=scale).astype(jnp.bfloat16) for j in range(2)], axis=0)
            k = _norm_rope(pc_ref[ck0 + kv].astype(jnp.float32), gk_ref[gi:gi + 1, :]).astype(jnp.bfloat16)
            vp = _with_ones(pc_ref[cv0 + kv])
            if br == 0:
                acc, _ = _softmax_pv(q2, [(k, vp, None)])
                o = _normalise(acc)
            else:
                sink = jnp.where(head_row, sink_ref[2 * kv], sink_ref[2 * kv + 1]) * LOG2E
                acc, m = _softmax_pv(q2, [(k, vp, None)], m=sink)
                o = _normalise(acc, jnp.exp2(sink - m))
            for j in range(2):
                h = 2 * kv + j
                _gated_store(o_ref, gi * BRANCH_W + h * LANE, o[j * tq:(j + 1) * tq], pc_ref[cg0 + h])

    for h in range(N_HEADS):
        q = _norm_rope(pc_ref[B_Q + h].astype(jnp.float32), gq_ref[1:2, :], scale=scale).astype(jnp.bfloat16)
        k = _norm_rope(pc_ref[B_K + h].astype(jnp.float32), gk_ref[1:2, :]).astype(jnp.bfloat16)
        acc, _ = _softmax_pv(q, [(k, _with_ones(pc_ref[B_V + h]), None)])
        _gated_store(o_ref, BRANCH_W + h * LANE, _normalise(acc), pc_ref[B_G + h])

    keys, vals = _mla_keys_values([pc_ref[D_CKV + j] for j in range(4)], pc_ref[D_KR], kvg_ref[...],
                                  dg_ref[2:3, :], dg_ref[3:4, :], wuk_ref[...], wuv_ref[...], None, None)
    for h in range(N_HEADS):
        q = _mla_queries(pc_ref[D_QN + h], pc_ref[D_QR + h // 2], h, dg_ref[0:1, :], dg_ref[1:2, :], None, None)
        acc, _ = _softmax_pv(q, [(keys[h], _with_ones(vals[h]), None)])
        _gated_store(o_ref, 3 * BRANCH_W + h * LANE, _normalise(acc), pc_ref[D_G + h])


def _attn_ctx(proj_c, sink, gq, gk, dg, kvg, wuk, wuv, layer):
    bsz, nc, ctx_len, _ = proj_c.shape
    full = lambda a: pl.BlockSpec(a.shape, lambda b: (0,) * a.ndim)
    wspec = pl.BlockSpec((None,) + wuk.shape[1:], lambda b: (layer, 0, 0))
    return pl.pallas_call(
        _attn_ctx_kernel,
        grid=(bsz,),
        in_specs=[pl.BlockSpec(memory_space=pltpu.SMEM),
                  pl.BlockSpec((None, nc, ctx_len, LANE), lambda b: (b, 0, 0, 0)),
                  full(gq), full(gk), full(dg), full(kvg), wspec, wspec],
        out_specs=pl.BlockSpec((None, ctx_len, 4 * BRANCH_W), lambda b: (b, 0, 0)),
        out_shape=jax.ShapeDtypeStruct((bsz, ctx_len, 4 * BRANCH_W), jnp.bfloat16),
        name="attn_ctx",
        compiler_params=_params("arbitrary"),
    )(sink, proj_c, gq, gk, dg, kvg, wuk, wuv)


def _outproj_kernel(*refs, n_in, mod_row):
    s_ref, mix_refs, (w_ref, mod_ref, o_ref) = refs[0], refs[1:1 + n_in], refs[1 + n_in:]
    d = s_ref.shape[1]
    acc = None
    for k in range(n_in):
        part = jnp.dot(mix_refs[k][...], w_ref[k], preferred_element_type=jnp.float32)
        acc = part if acc is None else acc + part
    mrow = pl.program_id(0) if mod_row is None else mod_row
    gate = mod_ref[pl.ds(mrow, 1), 2 * d:3 * d]
    o_ref[...] = s_ref[...] + gate * acc


def _outproj(stream, mixes, w, layer, mod, *, mod_row, tm):
    bsz, r, d = stream.shape
    n_in = len(mixes)
    wk = w.reshape(w.shape[0], n_in, w.shape[1] // n_in, d)
    mix_specs = [pl.BlockSpec((None, tm, m.shape[2]), lambda b, i: (b, i, 0)) for m in mixes]
    return pl.pallas_call(
        functools.partial(_outproj_kernel, n_in=n_in, mod_row=mod_row),
        grid=(bsz, r // tm),
        in_specs=[pl.BlockSpec((None, tm, d), lambda b, i: (b, i, 0))] + mix_specs + [
            pl.BlockSpec((None,) + wk.shape[1:], lambda b, i: (layer, 0, 0, 0)),
            pl.BlockSpec(mod.shape, lambda b, i: (0, 0))],
        out_specs=pl.BlockSpec((None, tm, d), lambda b, i: (b, i, 0)),
        out_shape=jax.ShapeDtypeStruct((bsz, r, d), jnp.float32),
        name="outproj",
        compiler_params=_params("arbitrary", "arbitrary"),
    )(stream, *mixes, wk, mod)


def _permute_w_d(w_t):
    sl = lambda a, n: w_t[:, a:a + n, :]
    qd, half = ABC_COLS, MLA_ROPE // 2
    pieces = [sl(qd + MLA_QK * h, MLA_NOPE) for h in range(N_HEADS)]
    pieces += [sl(6464, BRANCH_W), sl(5888, MLA_KV_RANK)]
    for pair in range(2):
        ha, hb = qd + MLA_QK * (2 * pair) + MLA_NOPE, qd + MLA_QK * (2 * pair + 1) + MLA_NOPE
        pieces += [sl(ha, half), sl(hb, half), sl(ha + half, half), sl(hb + half, half)]
    pieces += [sl(6400, half), sl(6400, half), sl(6400 + half, half), sl(6400 + half, half)]
    pieces += [jnp.zeros((w_t.shape[0], LANE, w_t.shape[2]), w_t.dtype)]
    return jnp.concatenate(pieces, axis=1)


def _w_in_kernel(w_ref, wd_ref, o_ref, *, n_abc):
    j = pl.program_id(1)

    @pl.when(j < n_abc)
    def _():
        o_ref[...] = w_ref[...].astype(jnp.bfloat16)

    @pl.when(j >= n_abc)
    def _():
        o_ref[...] = wd_ref[...].astype(jnp.bfloat16)


def _prep_w_in(w_in):
    depth, d, _ = w_in.shape
    w_t = jnp.swapaxes(w_in, 1, 2)
    wd = _permute_w_d(w_t)
    n_abc = ABC_COLS // TN_IN
    n_d = wd.shape[1] // TN_IN
    return pl.pallas_call(
        functools.partial(_w_in_kernel, n_abc=n_abc),
        grid=(depth, n_abc + n_d),
        in_specs=[pl.BlockSpec((None, TN_IN, d), lambda l, j: (l, jnp.minimum(j, n_abc - 1), 0)),
                  pl.BlockSpec((None, TN_IN, d), lambda l, j: (l, jnp.maximum(j - n_abc, 0), 0))],
        out_specs=pl.BlockSpec((None, TN_IN, d), lambda l, j: (l, j, 0)),
        out_shape=jax.ShapeDtypeStruct((depth, ABC_COLS + wd.shape[1], d), jnp.bfloat16),
        name="w_in_cast",
        compiler_params=_params("arbitrary", "arbitrary"),
    )(w_t, wd)


def _rope_tables(n_lat, rot_dim):
    tpos = jnp.arange(n_lat)
    row = (tpos // GRID_W).astype(jnp.float32)
    col = (tpos % GRID_W).astype(jnp.float32)
    n_freq = rot_dim // 4
    inv_freq = ROPE_THETA ** (-jnp.arange(n_freq, dtype=jnp.float32) / n_freq)
    ang = jnp.concatenate([row[:, None] * inv_freq, col[:, None] * inv_freq], axis=-1)
    cos, sin = jnp.cos(ang), jnp.sin(ang)
    rep = LANE // rot_dim
    return (jnp.concatenate([cos] * (2 * rep), axis=-1),
            jnp.concatenate([-sin] * rep + [sin] * rep, axis=-1))


def _dup_rope_gain(g):
    half = MLA_ROPE // 2
    r1, r2 = g[MLA_NOPE:MLA_NOPE + half], g[MLA_NOPE + half:]
    return g[None, :MLA_NOPE], jnp.concatenate([r1, r1, r2, r2])[None, :]


def kernel(x, c, ctx, c_ctx, norm_g, w_ada, b_ada, w_in, w_out, a_q_g, a_k_g, b_q_g, b_k_g, b_rpb,
           c_q_g, c_k_g, c_sink, d_q_g, d_k_g, d_kv_g, d_w_uk, d_w_uv):
    bsz, n_lat, d = x.shape
    ctx_len = ctx.shape[1]
    depth = w_in.shape[0]
    assert bsz <= 8 and n_lat % TM_IN == 0 and TM_IN % ctx_len == 0 and ctx_len % LANE == 0

    cc = jnp.zeros((16, d), jnp.float32).at[:bsz].set(c).at[8].set(c_ctx)
    mod = _ada(cc, w_ada, b_ada)
    w_in_p = _prep_w_in(w_in)
    w_out_p = w_out.astype(jnp.bfloat16)
    wuk, wuv = d_w_uk.astype(jnp.bfloat16), d_w_uv.astype(jnp.bfloat16)
    tabs_h = _rope_tables(n_lat, HEAD_DIM)
    tabs_r = _rope_tables(n_lat, MLA_ROPE)
    na_bias = _na_bias_tables(b_rpb, n_lat)

    hc = ctx.astype(x.dtype)
    for l in range(depth):
        with_ctx = l < depth - 1
        ng = norm_g[l][None, :]
        d_gains = _dup_rope_gain(d_q_g[l]) + _dup_rope_gain(d_k_g[l])
        kvg = d_kv_g[l].reshape(4, LANE)
        proj_x = _inproj(x, mod[l], ng, w_in_p, l, mod_row=None)
        if with_ctx:
            proj_c, cmap = _inproj(hc, mod[l], ng, w_in_p, l, mod_row=8), (lambda ch: ch)
        else:
            proj_c = _inproj(hc, mod[l], ng, w_in_p, l, mod_row=8, tile_stride=2)
            cmap = lambda ch: (ch // (2 * CPT)) * CPT + ch % CPT
        oa = _attn_gqa(proj_x, proj_c, cmap, tabs_h, a_q_g[l][None, :], a_k_g[l][None, :], None,
                       (A_Q, A_K, A_V, A_G))
        ob = _attn_b(proj_x, proj_c, cmap, na_bias, l, b_q_g[l][None, :], b_k_g[l][None, :])
        oc = _attn_gqa(proj_x, proj_c, cmap, tabs_h, c_q_g[l][None, :], c_k_g[l][None, :], c_sink[l],
                       (C_Q, C_K, C_V, C_G))
        od = _attn_d(proj_x, proj_c, cmap, tabs_r, d_gains + (kvg,), wuk, wuv, l)
        if with_ctx:
            o_ctx = _attn_ctx(proj_c, c_sink[l], jnp.stack([a_q_g[l], b_q_g[l], c_q_g[l]]),
                              jnp.stack([a_k_g[l], b_k_g[l], c_k_g[l]]), jnp.concatenate(d_gains, axis=0),
                              kvg, wuk, wuv, l)
            hc = _outproj(hc, (o_ctx,), w_out_p, l, mod[l], mod_row=8, tm=ctx_len)
        x = _outproj(x, (oa, ob, oc, od), w_out_p, l, mod[l], mod_row=None, tm=TM_OUT)
    return x
```

```python
import functools

import jax
import jax.numpy as jnp
from jax import lax
from jax.experimental import pallas as pl
from jax.experimental.pallas import tpu as pltpu

GRID_W = 64
HEAD_DIM = 128
BRANCH_W = 512
N_HEADS = 4
NA_KH = 8
NA_KW = 16
WINDOW = 128
MLA_KV_RANK = 512
MLA_NOPE = 128
MLA_ROPE = 64
MLA_QK = MLA_NOPE + MLA_ROPE
ROPE_THETA = 10000.0
EPS = 1e-6
NEG = -1e30
LOG2E = 1.4426950408889634

LANE = 128
TQ = 512
TM_IN = 1024
TN_IN = 1024
TM_OUT = 512
CPT = TN_IN // LANE
N_CHUNKS = 56
ABC_COLS = 5120
NA_STRIP = (NA_KH + TQ // GRID_W) * GRID_W
WIN_SPAN = TQ + 2 * WINDOW
VMEM_LIMIT = 48 * 1024 * 1024

A_Q, A_K, A_V, A_G = 0, 4, 6, 8
B_Q, B_K, B_V, B_G = 12, 16, 20, 24
C_Q, C_K, C_V, C_G = 28, 32, 34, 36
D_QN, D_G, D_CKV, D_QR, D_KR = 40, 44, 48, 52, 54

_NT = (((1,), (1,)), ((), ()))


def _params(*sem):
    return pltpu.CompilerParams(dimension_semantics=sem, vmem_limit_bytes=VMEM_LIMIT)


def _silu(x):
    return x * jax.nn.sigmoid(x)


def _ada_kernel(c_ref, w_ref, b_ref, o_ref):
    a = _silu(c_ref[...]).astype(jnp.bfloat16)
    o_ref[...] = jnp.dot(a, w_ref[...].astype(jnp.bfloat16),
                         preferred_element_type=jnp.float32) + b_ref[...]


def _ada(cc, w_ada, b_ada):
    depth, d, n = w_ada.shape
    tn = 512
    return pl.pallas_call(
        _ada_kernel,
        grid=(depth, n // tn),
        in_specs=[pl.BlockSpec((16, d), lambda l, j: (0, 0)),
                  pl.BlockSpec((None, d, tn), lambda l, j: (l, 0, j)),
                  pl.BlockSpec((None, 1, tn), lambda l, j: (l, 0, j))],
        out_specs=pl.BlockSpec((None, 16, tn), lambda l, j: (l, 0, j)),
        out_shape=jax.ShapeDtypeStruct((depth, 16, n), jnp.float32),
        name="ada",
        compiler_params=_params("arbitrary", "arbitrary"),
    )(cc, w_ada, b_ada.reshape(depth, 1, n))


def _inproj_kernel(x_ref, mod_ref, ng_ref, w_ref, o_ref, h_ref, *, mod_row):
    j = pl.program_id(2)
    nb, r, d = x_ref.shape
    mrow = pl.program_id(0) if mod_row is None else mod_row

    @pl.when(j == 0)
    def _():
        sh = mod_ref[pl.ds(mrow, 1), 0:d]
        gain = ng_ref[...] * (1.0 + mod_ref[pl.ds(mrow, 1), d:2 * d])
        for s in range(nb):
            x = x_ref[s]
            inv = lax.rsqrt(jnp.mean(x * x, axis=-1, keepdims=True) + EPS)
            h_ref[s * r:(s + 1) * r, :] = (x * inv * gain + sh).astype(jnp.bfloat16)

    acc = lax.dot_general(h_ref[...], w_ref[...], _NT, preferred_element_type=jnp.float32)
    for s in range(nb):
        for c in range(acc.shape[1] // LANE):
            o_ref[s, c] = acc[s * r:(s + 1) * r, c * LANE:(c + 1) * LANE].astype(jnp.bfloat16)


def _inproj(stream, mod, ng, w, layer, *, mod_row, tile_stride=1):
    bsz, r, d = stream.shape
    nb = min(bsz, max(1, TM_IN // r))
    rows = min(r, TM_IN)
    assert bsz % nb == 0 and r % rows == 0
    n_tiles = -(-(w.shape[1] // TN_IN) // tile_stride)
    return pl.pallas_call(
        functools.partial(_inproj_kernel, mod_row=mod_row),
        grid=(bsz // nb, r // rows, n_tiles),
        in_specs=[pl.BlockSpec((nb, rows, d), lambda b, i, j: (b, i, 0)),
                  pl.BlockSpec(mod.shape, lambda b, i, j: (0, 0)),
                  pl.BlockSpec((1, d), lambda b, i, j: (0, 0)),
                  pl.BlockSpec((None, TN_IN, d), lambda b, i, j: (layer, j * tile_stride, 0))],
        out_specs=pl.BlockSpec((nb, CPT, rows, LANE), lambda b, i, j: (b, j, i, 0)),
        out_shape=jax.ShapeDtypeStruct((bsz, n_tiles * CPT, r, LANE), jnp.bfloat16),
        scratch_shapes=[pltpu.VMEM((nb * rows, d), jnp.bfloat16)],
        name="inproj",
        compiler_params=_params("arbitrary", "arbitrary", "arbitrary"),
    )(stream, mod, ng, w)


def _norm_rope(x, gain, cos=None, sin=None, scale=None):
    y = x * lax.rsqrt(jnp.mean(x * x, axis=-1, keepdims=True) + EPS) * gain
    if cos is not None:
        y = y * cos + pltpu.roll(y, 64, 1) * sin
    if scale is not None:
        y = y * scale
    return y


def _chunk_spec(n, rows, chunk0, row_fn):
    return pl.BlockSpec((None, n, rows, LANE), lambda b, i: (b, chunk0 // n, row_fn(i), 0))


def _gated_store(o_ref, col, o, g):
    g = g.astype(jnp.float32)
    o_ref[:, col:col + LANE] = (o * _silu(g)).astype(o_ref.dtype)


def _with_ones(v):
    lane = lax.broadcasted_iota(jnp.int32, v.shape, 1)
    return jnp.concatenate([v, jnp.where(lane == 0, 1.0, 0.0).astype(v.dtype)], axis=-1)


def _softmax_pv(q, pieces, m=None):
    acc = None
    for k, vp, bias in pieces:
        s = lax.dot_general(q, k, _NT, preferred_element_type=jnp.float32)
        if bias is not None:
            rep, (rows, n) = s.shape[0] // bias.shape[0], bias.shape
            s = (s.reshape(rep, rows, n) + bias[None]).reshape(rep * rows, n)
        sb = s.astype(jnp.bfloat16)
        mc = jnp.max(sb, axis=-1, keepdims=True).astype(jnp.float32)
        mn = mc if m is None else jnp.maximum(m.astype(jnp.bfloat16).astype(jnp.float32), mc)
        p = jnp.exp2(sb - mn.astype(jnp.bfloat16))
        t = jnp.dot(p, vp, preferred_element_type=jnp.float32)
        acc = t if acc is None else acc * jnp.exp2(m - mn) + t
        m = mn
    return acc, m


def _normalise(acc, extra=None):
    l = acc[:, LANE:LANE + 1]
    if extra is not None:
        l = l + extra
    return acc[:, 0:LANE] / l


def _key_spans(t):
    cut = (t // 2 + 255) // 256 * 256
    return [(0, cut), (cut, t)]


def _prepare_queries(src_ref, qp_ref, slot, blk, qg_ref, cos_ref=None, sin_ref=None):
    tq = src_ref.shape[1]
    cos = sin = None
    if cos_ref is not None:
        r0 = pl.multiple_of(blk * tq, tq)
        cos, sin = cos_ref[pl.ds(r0, tq), :], sin_ref[pl.ds(r0, tq), :]
    for h in range(N_HEADS):
        qp_ref[slot, h] = _norm_rope(src_ref[h].astype(jnp.float32), qg_ref[...], cos, sin,
                                     HEAD_DIM ** -0.5 * LOG2E).astype(jnp.bfloat16)


def _next_block(i):
    return jnp.minimum(i + 1, pl.num_programs(1) - 1)


def _load_stacked_queries(qp_ref, slot):
    return [jnp.concatenate([qp_ref[slot, 2 * kv], qp_ref[slot, 2 * kv + 1]], axis=0) for kv in range(2)]


def _mla_queries(qn, qr, h, qgn, qgr, cos, sin):
    lane_grp = (lax.broadcasted_iota(jnp.int32, (1, LANE), 1) // (MLA_ROPE // 2)) % 2
    qn = qn.astype(jnp.float32)
    qt = jnp.where(lane_grp == h % 2, qr.astype(jnp.float32), 0.0)
    ms = jnp.sum(qn * qn + qt * qt, axis=-1, keepdims=True) / MLA_QK
    inv = lax.rsqrt(ms + EPS)
    qt = qt * inv * qgr
    if cos is not None:
        qt = qt * cos + pltpu.roll(qt, 64, 1) * sin
    return (jnp.concatenate([qn * inv * qgn, qt], axis=-1) * (MLA_QK ** -0.5 * LOG2E)).astype(jnp.bfloat16)


def _mla_keys_values(ckv, kr, kvg, kgn, kgr, wuk, wuv, cos, sin):
    c = [cj.astype(jnp.float32) for cj in ckv]
    ms = jnp.sum(sum(cj * cj for cj in c), axis=-1, keepdims=True) / MLA_KV_RANK
    inv = lax.rsqrt(ms + EPS)
    cn = jnp.concatenate([c[j] * inv * kvg[j:j + 1, :] for j in range(4)], axis=-1).astype(jnp.bfloat16)
    kn = jnp.dot(cn, wuk, preferred_element_type=jnp.float32)
    vv = jnp.dot(cn, wuv, preferred_element_type=jnp.float32)
    kr = kr.astype(jnp.float32)
    kr_sq = 0.5 * (kr * kr)
    kt = kr * kgr
    if cos is not None:
        kt = kt * cos + pltpu.roll(kt, 64, 1) * sin
    keys, vals = [], []
    for h in range(N_HEADS):
        kh = kn[:, h * LANE:(h + 1) * LANE]
        inv_h = lax.rsqrt(jnp.sum(kh * kh + kr_sq, axis=-1, keepdims=True) / MLA_QK + EPS)
        keys.append(jnp.concatenate([kh * inv_h * kgn, kt * inv_h], axis=-1).astype(jnp.bfloat16))
        vals.append(vv[:, h * LANE:(h + 1) * LANE].astype(jnp.bfloat16))
    return keys, vals


def _gqa_prep(kl_ref, kc_ref, vl_ref, vc_ref, ck_ref, sk_ref, kg_ref, kp_ref, vp_ref, ctx_len):
    for kv in range(2):
        kp_ref[kv, 0:ctx_len, :] = _norm_rope(kc_ref[kv].astype(jnp.float32), kg_ref[...]).astype(jnp.bfloat16)
        kp_ref[kv, ctx_len:, :] = _norm_rope(kl_ref[kv].astype(jnp.float32), kg_ref[...],
                                             ck_ref[...], sk_ref[...]).astype(jnp.bfloat16)
        vp_ref[kv, 0:ctx_len, :] = _with_ones(vc_ref[kv])
        vp_ref[kv, ctx_len:, :] = _with_ones(vl_ref[kv])


def _attn_a_kernel(q_ref, qn_ref, kl_ref, kc_ref, vl_ref, vc_ref, g_ref, ck_ref, sk_ref, qg_ref, kg_ref,
                   o_ref, kp_ref, vp_ref, qp_ref, *, ctx_len):
    i = pl.program_id(1)
    tq = q_ref.shape[1]

    @pl.when(i == 0)
    def _():
        _gqa_prep(kl_ref, kc_ref, vl_ref, vc_ref, ck_ref, sk_ref, kg_ref, kp_ref, vp_ref, ctx_len)
        _prepare_queries(q_ref, qp_ref, 0, 0, qg_ref, ck_ref, sk_ref)

    qs = _load_stacked_queries(qp_ref, i % 2)
    _prepare_queries(qn_ref, qp_ref, 1 - i % 2, _next_block(i), qg_ref, ck_ref, sk_ref)
    for kv in range(2):
        q2 = qs[kv]
        acc, _ = _softmax_pv(q2, [(kp_ref[kv, lo:hi, :], vp_ref[kv, lo:hi, :], None)
                                  for lo, hi in _key_spans(kp_ref.shape[1])])
        o = _normalise(acc)
        for j in range(2):
            _gated_store(o_ref, (2 * kv + j) * LANE, o[j * tq:(j + 1) * tq], g_ref[2 * kv + j])


def _attn_c_kernel(sink_ref, q_ref, qn_ref, kl_ref, kc_ref, vl_ref, vc_ref, g_ref, ck_ref, sk_ref,
                   qg_ref, kg_ref, o_ref, kp_ref, vp_ref, qp_ref, *, ctx_len):
    i = pl.program_id(1)
    tq = q_ref.shape[1]
    n_lat = kl_ref.shape[1]

    @pl.when(i == 0)
    def _():
        _gqa_prep(kl_ref, kc_ref, vl_ref, vc_ref, ck_ref, sk_ref, kg_ref, kp_ref, vp_ref, ctx_len)
        _prepare_queries(q_ref, qp_ref, 0, 0, qg_ref, ck_ref, sk_ref)

    qs = _load_stacked_queries(qp_ref, i % 2)
    _prepare_queries(qn_ref, qp_ref, 1 - i % 2, _next_block(i), qg_ref, ck_ref, sk_ref)

    q0 = i * tq
    ks = jnp.clip(q0 - WINDOW, 0, n_lat - WIN_SPAN)
    row0 = pl.multiple_of(ctx_len + ks, LANE)
    qi = lax.broadcasted_iota(jnp.int32, (tq, WIN_SPAN), 0)
    ki = lax.broadcasted_iota(jnp.int32, (tq, WIN_SPAN), 1)
    wmask = jnp.where(jnp.abs((qi - ki) + (q0 - ks)) <= WINDOW, 0.0, NEG)
    head_row = lax.broadcasted_iota(jnp.int32, (2 * tq, 1), 0) < tq
    for kv in range(2):
        q2 = qs[kv]
        sink = jnp.where(head_row, sink_ref[2 * kv], sink_ref[2 * kv + 1]) * LOG2E
        pieces = [(kp_ref[kv, 0:ctx_len, :], vp_ref[kv, 0:ctx_len, :], None),
                  (kp_ref[kv, pl.ds(row0, WIN_SPAN), :], vp_ref[kv, pl.ds(row0, WIN_SPAN), :], wmask)]
        acc, m = _softmax_pv(q2, pieces, m=sink)
        o = _normalise(acc, jnp.exp2(sink - m))
        for j in range(2):
            _gated_store(o_ref, (2 * kv + j) * LANE, o[j * tq:(j + 1) * tq], g_ref[2 * kv + j])


def _attn_gqa(proj_x, proj_c, cmap, tabs, qg, kg, sink, chunks):
    cq0, ck0, cv0, cg0 = chunks
    bsz, _, n_lat, _ = proj_x.shape
    ctx_len = proj_c.shape[2]
    t = ctx_len + n_lat
    cos, sin = tabs
    nq = n_lat // TQ
    row = lambda i: i
    nxt = lambda i: jnp.minimum(i + 1, nq - 1)
    zero = lambda i: 0
    in_specs = [_chunk_spec(4, TQ, cq0, row), _chunk_spec(4, TQ, cq0, nxt),
                _chunk_spec(2, n_lat, ck0, zero), _chunk_spec(2, ctx_len, cmap(ck0), zero),
                _chunk_spec(2, n_lat, cv0, zero), _chunk_spec(2, ctx_len, cmap(cv0), zero),
                _chunk_spec(4, TQ, cg0, row),
                pl.BlockSpec((n_lat, LANE), lambda b, i: (0, 0)),
                pl.BlockSpec((n_lat, LANE), lambda b, i: (0, 0)),
                pl.BlockSpec((1, LANE), lambda b, i: (0, 0)),
                pl.BlockSpec((1, LANE), lambda b, i: (0, 0))]
    args = [proj_x, proj_x, proj_x, proj_c, proj_x, proj_c, proj_x, cos, sin, qg, kg]
    if sink is None:
        body = _attn_a_kernel
    else:
        body = _attn_c_kernel
        in_specs = [pl.BlockSpec(memory_space=pltpu.SMEM)] + in_specs
        args = [sink] + args
    return pl.pallas_call(
        functools.partial(body, ctx_len=ctx_len),
        grid=(bsz, n_lat // TQ),
        in_specs=in_specs,
        out_specs=pl.BlockSpec((None, TQ, BRANCH_W), lambda b, i: (b, i, 0)),
        out_shape=jax.ShapeDtypeStruct((bsz, n_lat, BRANCH_W), jnp.bfloat16),
        scratch_shapes=[pltpu.VMEM((2, t, LANE), jnp.bfloat16), pltpu.VMEM((2, t, 2 * LANE), jnp.bfloat16),
                        pltpu.VMEM((2, N_HEADS, TQ, LANE), jnp.bfloat16)],
        name="attn_a" if sink is None else "attn_c",
        compiler_params=_params("arbitrary", "arbitrary"),
    )(*args)


def _attn_b_kernel(q_ref, qn_ref, kl_ref, kc_ref, vl_ref, vc_ref, g_ref, bias_ref, qg_ref, kg_ref,
                   o_ref, kp_ref, vp_ref, qp_ref, *, ctx_len):
    i = pl.program_id(1)
    tq = q_ref.shape[1]
    rows = kl_ref.shape[1] // GRID_W
    strip_rows = NA_STRIP // GRID_W

    @pl.when(i == 0)
    def _():
        for h in range(N_HEADS):
            kp_ref[h, 0:ctx_len, :] = _norm_rope(kc_ref[h].astype(jnp.float32), kg_ref[...]).astype(jnp.bfloat16)
            kp_ref[h, ctx_len:, :] = _norm_rope(kl_ref[h].astype(jnp.float32), kg_ref[...]).astype(jnp.bfloat16)
            vp_ref[h, 0:ctx_len, :] = _with_ones(vc_ref[h])
            vp_ref[h, ctx_len:, :] = _with_ones(vl_ref[h])
        _prepare_queries(q_ref, qp_ref, 0, 0, qg_ref)

    qs = [qp_ref[i % 2, h] for h in range(N_HEADS)]
    _prepare_queries(qn_ref, qp_ref, 1 - i % 2, _next_block(i), qg_ref)
    r0 = i * (tq // GRID_W)
    ss = jnp.clip(r0 - NA_KH // 2, 0, rows - strip_rows)
    row0 = pl.multiple_of(ctx_len + ss * GRID_W, LANE)
    for h in range(N_HEADS):
        pieces = [(kp_ref[h, 0:ctx_len, :], vp_ref[h, 0:ctx_len, :], None),
                  (kp_ref[h, pl.ds(row0, NA_STRIP), :], vp_ref[h, pl.ds(row0, NA_STRIP), :], bias_ref[h])]
        acc, _ = _softmax_pv(qs[h], pieces)
        _gated_store(o_ref, h * LANE, _normalise(acc), g_ref[h])


def _na_bias_kernel(rp_ref, o_ref, *, rows):
    q_rows, strip_rows = TQ // GRID_W, NA_STRIP // GRID_W
    qc = lax.broadcasted_iota(jnp.int32, (GRID_W, LANE), 0)
    lane = lax.broadcasted_iota(jnp.int32, (GRID_W, LANE), 1)
    kc = lane & (GRID_W - 1)
    cs = jnp.clip(qc - NA_KW // 2, 0, GRID_W - NA_KW)
    col_ok = (kc >= cs) & (kc < cs + NA_KW)
    second = lane >= GRID_W
    for var, r0 in enumerate((0, q_rows, rows - q_rows)):
        ss = min(max(r0 - NA_KH // 2, 0), rows - strip_rows)
        for j in range(q_rows):
            qr = r0 + j
            rs = min(max(qr - NA_KH // 2, 0), rows - NA_KH)
            for p in range(strip_rows // 2):
                kr0 = ss + 2 * p
                ok0, ok1 = rs <= kr0 < rs + NA_KH, rs <= kr0 + 1 < rs + NA_KH
                if ok0 or ok1:
                    e = kr0 - qr + NA_KH - 1
                    x = jnp.broadcast_to(rp_ref[e + 1:e + 2, :], (GRID_W, LANE))
                    band = pltpu.roll(x, LANE - (NA_KW - 1), 1, stride=1, stride_axis=0)
                    row_ok = second if (ok1 and not ok0) else (~second if (ok0 and not ok1) else None)
                    valid = col_ok if row_ok is None else (col_ok & row_ok)
                    tile = jnp.where(valid, band * LOG2E, NEG)
                else:
                    tile = jnp.full((GRID_W, LANE), NEG, jnp.float32)
                o_ref[var, j * GRID_W:(j + 1) * GRID_W, p * LANE:(p + 1) * LANE] = tile


def _na_bias_tables(rpb, n_lat):
    depth, nh, nr, nc = rpb.shape
    z = jnp.zeros((depth, nh, nr + 2, GRID_W), jnp.float32).at[:, :, 1:nr + 1, :nc].set(rpb)
    rp = jnp.concatenate([z[:, :, :-1], z[:, :, 1:]], axis=-1)
    return pl.pallas_call(
        functools.partial(_na_bias_kernel, rows=n_lat // GRID_W),
        grid=(depth, nh),
        in_specs=[pl.BlockSpec((None, None, nr + 1, LANE), lambda l, h: (l, h, 0, 0))],
        out_specs=pl.BlockSpec((None, 3, None, TQ, NA_STRIP), lambda l, h: (l, 0, h, 0, 0)),
        out_shape=jax.ShapeDtypeStruct((depth, 3, nh, TQ, NA_STRIP), jnp.float32),
        name="na_bias",
        compiler_params=_params("arbitrary", "arbitrary"),
    )(rp)


def _attn_b(proj_x, proj_c, cmap, bias, layer, qg, kg):
    bsz, _, n_lat, _ = proj_x.shape
    ctx_len = proj_c.shape[2]
    t = ctx_len + n_lat
    nq = n_lat // TQ
    row = lambda i: i
    nxt = lambda i: jnp.minimum(i + 1, nq - 1)
    zero = lambda i: 0

    def variant(i):
        return jnp.where(i == 0, 0, jnp.where(i == nq - 1, 2, 1))

    return pl.pallas_call(
        functools.partial(_attn_b_kernel, ctx_len=ctx_len),
        grid=(bsz, nq),
        in_specs=[_chunk_spec(4, TQ, B_Q, row), _chunk_spec(4, TQ, B_Q, nxt),
                  _chunk_spec(4, n_lat, B_K, zero), _chunk_spec(4, ctx_len, cmap(B_K), zero),
                  _chunk_spec(4, n_lat, B_V, zero), _chunk_spec(4, ctx_len, cmap(B_V), zero),
                  _chunk_spec(4, TQ, B_G, row),
                  pl.BlockSpec((None, None, N_HEADS, TQ, NA_STRIP), lambda b, i: (layer, variant(i), 0, 0, 0)),
                  pl.BlockSpec((1, LANE), lambda b, i: (0, 0)),
                  pl.BlockSpec((1, LANE), lambda b, i: (0, 0))],
        out_specs=pl.BlockSpec((None, TQ, BRANCH_W), lambda b, i: (b, i, 0)),
        out_shape=jax.ShapeDtypeStruct((bsz, n_lat, BRANCH_W), jnp.bfloat16),
        scratch_shapes=[pltpu.VMEM((N_HEADS, t, LANE), jnp.bfloat16),
                        pltpu.VMEM((N_HEADS, t, 2 * LANE), jnp.bfloat16),
                        pltpu.VMEM((2, N_HEADS, TQ, LANE), jnp.bfloat16)],
        name="attn_b",
        compiler_params=_params("arbitrary", "arbitrary"),
    )(proj_x, proj_x, proj_x, proj_c, proj_x, proj_c, proj_x, bias, qg, kg)


def _attn_d_kernel(qn_ref, qr_ref, qn2_ref, qr2_ref, ckvl_ref, ckvc_ref, krl_ref, krc_ref, g_ref, ck_ref, sk_ref,
                   qgn_ref, qgr_ref, kgn_ref, kgr_ref, kvg_ref, wuk_ref, wuv_ref,
                   o_ref, kp_ref, vp_ref, qp_ref, *, ctx_len):
    i = pl.program_id(1)
    n_lat = ckvl_ref.shape[1]
    tq = qn_ref.shape[1]

    def prepare_queries(nope_ref, rope_ref, slot, blk):
        r0 = pl.multiple_of(blk * tq, tq)
        cos, sin = ck_ref[pl.ds(r0, tq), :], sk_ref[pl.ds(r0, tq), :]
        for h in range(N_HEADS):
            qp_ref[slot, h] = _mla_queries(nope_ref[h], rope_ref[h // 2], h, qgn_ref[...], qgr_ref[...], cos, sin)

    @pl.when(i == 0)
    def _():
        prepare_queries(qn_ref, qr_ref, 0, 0)

        def fill(dst, ckv_ref, kr_ref, src, n, cos, sin):
            keys, vals = _mla_keys_values([ckv_ref[j, src:src + n, :] for j in range(4)], kr_ref[0, src:src + n, :],
                                          kvg_ref[...], kgn_ref[...], kgr_ref[...], wuk_ref[...], wuv_ref[...],
                                          cos, sin)
            for h in range(N_HEADS):
                kp_ref[h, dst:dst + n, :] = keys[h]
                vp_ref[h, dst:dst + n, :] = _with_ones(vals[h])

        fill(0, ckvc_ref, krc_ref, 0, ctx_len, None, None)
        rc = 512
        for r in range(0, n_lat, rc):
            fill(ctx_len + r, ckvl_ref, krl_ref, r, rc, ck_ref[r:r + rc, :], sk_ref[r:r + rc, :])

    qs = [qp_ref[i % 2, h] for h in range(N_HEADS)]
    prepare_queries(qn2_ref, qr2_ref, 1 - i % 2, _next_block(i))
    for h in range(N_HEADS):
        acc, _ = _softmax_pv(qs[h], [(kp_ref[h, lo:hi, :], vp_ref[h, lo:hi, :], None)
                                     for lo, hi in _key_spans(kp_ref.shape[1])])
        _gated_store(o_ref, h * LANE, _normalise(acc), g_ref[h])


def _attn_d(proj_x, proj_c, cmap, tabs, gains, wuk, wuv, layer):
    bsz, _, n_lat, _ = proj_x.shape
    ctx_len = proj_c.shape[2]
    t = ctx_len + n_lat
    nq = n_lat // TQ
    row = lambda i: i
    nxt = lambda i: jnp.minimum(i + 1, nq - 1)
    zero = lambda i: 0
    cos, sin = tabs
    qgn, qgr, kgn, kgr, kvg = gains
    vec = pl.BlockSpec((1, LANE), lambda b, i: (0, 0))
    wspec = pl.BlockSpec((None,) + wuk.shape[1:], lambda b, i: (layer, 0, 0))
    return pl.pallas_call(
        functools.partial(_attn_d_kernel, ctx_len=ctx_len),
        grid=(bsz, n_lat // TQ),
        in_specs=[_chunk_spec(4, TQ, D_QN, row), _chunk_spec(2, TQ, D_QR, row),
                  _chunk_spec(4, TQ, D_QN, nxt), _chunk_spec(2, TQ, D_QR, nxt),
                  _chunk_spec(4, n_lat, D_CKV, zero), _chunk_spec(4, ctx_len, cmap(D_CKV), zero),
                  _chunk_spec(1, n_lat, D_KR, zero), _chunk_spec(1, ctx_len, cmap(D_KR), zero),
                  _chunk_spec(4, TQ, D_G, row),
                  pl.BlockSpec((n_lat, LANE), lambda b, i: (0, 0)),
                  pl.BlockSpec((n_lat, LANE), lambda b, i: (0, 0)),
                  vec, vec, vec, vec,
                  pl.BlockSpec((4, LANE), lambda b, i: (0, 0)),
                  wspec, wspec],
        out_specs=pl.BlockSpec((None, TQ, BRANCH_W), lambda b, i: (b, i, 0)),
        out_shape=jax.ShapeDtypeStruct((bsz, n_lat, BRANCH_W), jnp.bfloat16),
        scratch_shapes=[pltpu.VMEM((N_HEADS, t, 2 * LANE), jnp.bfloat16),
                        pltpu.VMEM((N_HEADS, t, 2 * LANE), jnp.bfloat16),
                        pltpu.VMEM((2, N_HEADS, TQ, 2 * LANE), jnp.bfloat16)],
        name="attn_d",
        compiler_params=_params("arbitrary", "arbitrary"),
    )(proj_x, proj_x, proj_x, proj_x, proj_x, proj_c, proj_x, proj_c, proj_x, cos, sin,
      qgn, qgr, kgn, kgr, kvg, wuk, wuv)


def _attn_ctx_kernel(sink_ref, pc_ref, gq_ref, gk_ref, dg_ref, kvg_ref, wuk_ref, wuv_ref, o_ref):
    tq = pc_ref.shape[1]
    scale = HEAD_DIM ** -0.5 * LOG2E

    for br, (cq0, ck0, cv0, cg0) in enumerate(((A_Q, A_K, A_V, A_G), (C_Q, C_K, C_V, C_G))):
        gi = 2 * br
        head_row = lax.broadcasted_iota(jnp.int32, (2 * tq, 1), 0) < tq
        for kv in range(2):
            q2 = jnp.concatenate(
                [_norm_rope(pc_ref[cq0 + 2 * kv + j].astype(jnp.float32), gq_ref[gi:gi + 1, :],
                            scale=scale).astype(jnp.bfloat16) for j in range(2)], axis=0)
            k = _norm_rope(pc_ref[ck0 + kv].astype(jnp.float32), gk_ref[gi:gi + 1, :]).astype(jnp.bfloat16)
            vp = _with_ones(pc_ref[cv0 + kv])
            if br == 0:
                acc, _ = _softmax_pv(q2, [(k, vp, None)])
                o = _normalise(acc)
            else:
                sink = jnp.where(head_row, sink_ref[2 * kv], sink_ref[2 * kv + 1]) * LOG2E
                acc, m = _softmax_pv(q2, [(k, vp, None)], m=sink)
                o = _normalise(acc, jnp.exp2(sink - m))
            for j in range(2):
                h = 2 * kv + j
                _gated_store(o_ref, gi * BRANCH_W + h * LANE, o[j * tq:(j + 1) * tq], pc_ref[cg0 + h])

    for h in range(N_HEADS):
        q = _norm_rope(pc_ref[B_Q + h].astype(jnp.float32), gq_ref[1:2, :], scale=scale).astype(jnp.bfloat16)
        k = _norm_rope(pc_ref[B_K + h].astype(jnp.float32), gk_ref[1:2, :]).astype(jnp.bfloat16)
        acc, _ = _softmax_pv(q, [(k, _with_ones(pc_ref[B_V + h]), None)])
        _gated_store(o_ref, BRANCH_W + h * LANE, _normalise(acc), pc_ref[B_G + h])

    keys, vals = _mla_keys_values([pc_ref[D_CKV + j] for j in range(4)], pc_ref[D_KR], kvg_ref[...],
                                  dg_ref[2:3, :], dg_ref[3:4, :], wuk_ref[...], wuv_ref[...], None, None)
    for h in range(N_HEADS):
        q = _mla_queries(pc_ref[D_QN + h], pc_ref[D_QR + h // 2], h, dg_ref[0:1, :], dg_ref[1:2, :], None, None)
        acc, _ = _softmax_pv(q, [(keys[h], _with_ones(vals[h]), None)])
        _gated_store(o_ref, 3 * BRANCH_W + h * LANE, _normalise(acc), pc_ref[D_G + h])


def _attn_ctx(proj_c, sink, gq, gk, dg, kvg, wuk, wuv, layer):
    bsz, nc, ctx_len, _ = proj_c.shape
    full = lambda a: pl.BlockSpec(a.shape, lambda b: (0,) * a.ndim)
    wspec = pl.BlockSpec((None,) + wuk.shape[1:], lambda b: (layer, 0, 0))
    return pl.pallas_call(
        _attn_ctx_kernel,
        grid=(bsz,),
        in_specs=[pl.BlockSpec(memory_space=pltpu.SMEM),
                  pl.BlockSpec((None, nc, ctx_len, LANE), lambda b: (b, 0, 0, 0)),
                  full(gq), full(gk), full(dg), full(kvg), wspec, wspec],
        out_specs=pl.BlockSpec((None, ctx_len, 4 * BRANCH_W), lambda b: (b, 0, 0)),
        out_shape=jax.ShapeDtypeStruct((bsz, ctx_len, 4 * BRANCH_W), jnp.bfloat16),
        name="attn_ctx",
        compiler_params=_params("arbitrary"),
    )(sink, proj_c, gq, gk, dg, kvg, wuk, wuv)


def _outproj_kernel(*refs, n_in, mod_row):
    s_ref, mix_refs, (w_ref, mod_ref, o_ref) = refs[0], refs[1:1 + n_in], refs[1 + n_in:]
    d = s_ref.shape[1]
    acc = None
    for k in range(n_in):
        part = jnp.dot(mix_refs[k][...], w_ref[k], preferred_element_type=jnp.float32)
        acc = part if acc is None else acc + part
    mrow = pl.program_id(0) if mod_row is None else mod_row
    gate = mod_ref[pl.ds(mrow, 1), 2 * d:3 * d]
    o_ref[...] = s_ref[...] + gate * acc


def _outproj(stream, mixes, w, layer, mod, *, mod_row, tm):
    bsz, r, d = stream.shape
    n_in = len(mixes)
    wk = w.reshape(w.shape[0], n_in, w.shape[1] // n_in, d)
    mix_specs = [pl.BlockSpec((None, tm, m.shape[2]), lambda b, i: (b, i, 0)) for m in mixes]
    return pl.pallas_call(
        functools.partial(_outproj_kernel, n_in=n_in, mod_row=mod_row),
        grid=(bsz, r // tm),
        in_specs=[pl.BlockSpec((None, tm, d), lambda b, i: (b, i, 0))] + mix_specs + [
            pl.BlockSpec((None,) + wk.shape[1:], lambda b, i: (layer, 0, 0, 0)),
            pl.BlockSpec(mod.shape, lambda b, i: (0, 0))],
        out_specs=pl.BlockSpec((None, tm, d), lambda b, i: (b, i, 0)),
        out_shape=jax.ShapeDtypeStruct((bsz, r, d), jnp.float32),
        name="outproj",
        compiler_params=_params("arbitrary", "arbitrary"),
    )(stream, *mixes, wk, mod)


def _permute_w_d(w_t):
    sl = lambda a, n: w_t[:, a:a + n, :]
    qd, half = ABC_COLS, MLA_ROPE // 2
    pieces = [sl(qd + MLA_QK * h, MLA_NOPE) for h in range(N_HEADS)]
    pieces += [sl(6464, BRANCH_W), sl(5888, MLA_KV_RANK)]
    for pair in range(2):
        ha, hb = qd + MLA_QK * (2 * pair) + MLA_NOPE, qd + MLA_QK * (2 * pair + 1) + MLA_NOPE
        pieces += [sl(ha, half), sl(hb, half), sl(ha + half, half), sl(hb + half, half)]
    pieces += [sl(6400, half), sl(6400, half), sl(6400 + half, half), sl(6400 + half, half)]
    pieces += [jnp.zeros((w_t.shape[0], LANE, w_t.shape[2]), w_t.dtype)]
    return jnp.concatenate(pieces, axis=1)


def _w_in_kernel(w_ref, wd_ref, o_ref, *, n_abc):
    j = pl.program_id(1)

    @pl.when(j < n_abc)
    def _():
        o_ref[...] = w_ref[...].astype(jnp.bfloat16)

    @pl.when(j >= n_abc)
    def _():
        o_ref[...] = wd_ref[...].astype(jnp.bfloat16)


def _prep_w_in(w_in):
    depth, d, _ = w_in.shape
    w_t = jnp.swapaxes(w_in, 1, 2)
    wd = _permute_w_d(w_t)
    n_abc = ABC_COLS // TN_IN
    n_d = wd.shape[1] // TN_IN
    return pl.pallas_call(
        functools.partial(_w_in_kernel, n_abc=n_abc),
        grid=(depth, n_abc + n_d),
        in_specs=[pl.BlockSpec((None, TN_IN, d), lambda l, j: (l, jnp.minimum(j, n_abc - 1), 0)),
                  pl.BlockSpec((None, TN_IN, d), lambda l, j: (l, jnp.maximum(j - n_abc, 0), 0))],
        out_specs=pl.BlockSpec((None, TN_IN, d), lambda l, j: (l, j, 0)),
        out_shape=jax.ShapeDtypeStruct((depth, ABC_COLS + wd.shape[1], d), jnp.bfloat16),
        name="w_in_cast",
        compiler_params=_params("arbitrary", "arbitrary"),
    )(w_t, wd)


def _rope_tables(n_lat, rot_dim):
    tpos = jnp.arange(n_lat)
    row = (tpos // GRID_W).astype(jnp.float32)
    col = (tpos % GRID_W).astype(jnp.float32)
    n_freq = rot_dim // 4
    inv_freq = ROPE_THETA ** (-jnp.arange(n_freq, dtype=jnp.float32) / n_freq)
    ang = jnp.concatenate([row[:, None] * inv_freq, col[:, None] * inv_freq], axis=-1)
    cos, sin = jnp.cos(ang), jnp.sin(ang)
    rep = LANE // rot_dim
    return (jnp.concatenate([cos] * (2 * rep), axis=-1),
            jnp.concatenate([-sin] * rep + [sin] * rep, axis=-1))


def _dup_rope_gain(g):
    half = MLA_ROPE // 2
    r1, r2 = g[MLA_NOPE:MLA_NOPE + half], g[MLA_NOPE + half:]
    return g[None, :MLA_NOPE], jnp.concatenate([r1, r1, r2, r2])[None, :]


def kernel(x, c, ctx, c_ctx, norm_g, w_ada, b_ada, w_in, w_out, a_q_g, a_k_g, b_q_g, b_k_g, b_rpb,
           c_q_g, c_k_g, c_sink, d_q_g, d_k_g, d_kv_g, d_w_uk, d_w_uv):
    bsz, n_lat, d = x.shape
    ctx_len = ctx.shape[1]
    depth = w_in.shape[0]
    assert bsz <= 8 and n_lat % TM_IN == 0 and TM_IN % ctx_len == 0 and ctx_len % LANE == 0

    cc = jnp.zeros((16, d), jnp.float32).at[:bsz].set(c).at[8].set(c_ctx)
    mod = _ada(cc, w_ada, b_ada)
    w_in_p = _prep_w_in(w_in)
    w_out_p = w_out.astype(jnp.bfloat16)
    wuk, wuv = d_w_uk.astype(jnp.bfloat16), d_w_uv.astype(jnp.bfloat16)
    tabs_h = _rope_tables(n_lat, HEAD_DIM)
    tabs_r = _rope_tables(n_lat, MLA_ROPE)
    na_bias = _na_bias_tables(b_rpb, n_lat)

    hc = ctx.astype(x.dtype)
    for l in range(depth):
        with_ctx = l < depth - 1
        ng = norm_g[l][None, :]
        d_gains = _dup_rope_gain(d_q_g[l]) + _dup_rope_gain(d_k_g[l])
        kvg = d_kv_g[l].reshape(4, LANE)
        proj_x = _inproj(x, mod[l], ng, w_in_p, l, mod_row=None)
        if with_ctx:
            proj_c, cmap = _inproj(hc, mod[l], ng, w_in_p, l, mod_row=8), (lambda ch: ch)
        else:
            proj_c = _inproj(hc, mod[l], ng, w_in_p, l, mod_row=8, tile_stride=2)
            cmap = lambda ch: (ch // (2 * CPT)) * CPT + ch % CPT
        oa = _attn_gqa(proj_x, proj_c, cmap, tabs_h, a_q_g[l][None, :], a_k_g[l][None, :], None,
                       (A_Q, A_K, A_V, A_G))
        ob = _attn_b(proj_x, proj_c, cmap, na_bias, l, b_q_g[l][None, :], b_k_g[l][None, :])
        oc = _attn_gqa(proj_x, proj_c, cmap, tabs_h, c_q_g[l][None, :], c_k_g[l][None, :], c_sink[l],
                       (C_Q, C_K, C_V, C_G))
        od = _attn_d(proj_x, proj_c, cmap, tabs_r, d_gains + (kvg,), wuk, wuv, l)
        if with_ctx:
            o_ctx = _attn_ctx(proj_c, c_sink[l], jnp.stack([a_q_g[l], b_q_g[l], c_q_g[l]]),
                              jnp.stack([a_k_g[l], b_k_g[l], c_k_g[l]]), jnp.concatenate(d_gains, axis=0),
                              kvg, wuk, wuv, l)
            hc = _outproj(hc, (o_ctx,), w_out_p, l, mod[l], mod_row=8, tm=ctx_len)
        x = _outproj(x, (oa, ob, oc, od), w_out_p, l, mod[l], mod_row=None, tm=TM_OUT)
    return x
```

```python
import functools

import jax
import jax.numpy as jnp
from jax import lax
from jax.experimental import pallas as pl
from jax.experimental.pallas import tpu as pltpu

GRID_W = 64
HEAD_DIM = 128
BRANCH_W = 512
N_HEADS = 4
NA_KH = 8
NA_KW = 16
WINDOW = 128
MLA_KV_RANK = 512
MLA_NOPE = 128
MLA_ROPE = 64
MLA_QK = MLA_NOPE + MLA_ROPE
ROPE_THETA = 10000.0
EPS = 1e-6
NEG = -1e30
LOG2E = 1.4426950408889634

LANE = 128
TQ = 512
TM_IN = 1024
TN_IN = 1024
TM_OUT = 512
CPT = TN_IN // LANE
N_CHUNKS = 56
ABC_COLS = 5120
NA_STRIP = (NA_KH + TQ // GRID_W) * GRID_W
WIN_SPAN = TQ + 2 * WINDOW
VMEM_LIMIT = 48 * 1024 * 1024

A_Q, A_K, A_V, A_G = 0, 4, 6, 8
B_Q, B_K, B_V, B_G = 12, 16, 20, 24
C_Q, C_K, C_V, C_G = 28, 32, 34, 36
D_QN, D_G, D_CKV, D_QR, D_KR = 40, 44, 48, 52, 54

_NT = (((1,), (1,)), ((), ()))


def _params(*sem):
    return pltpu.CompilerParams(dimension_semantics=sem, vmem_limit_bytes=VMEM_LIMIT)


def _silu(x):
    return x * jax.nn.sigmoid(x)


def _ada_kernel(c_ref, w_ref, b_ref, o_ref):
    a = _silu(c_ref[...]).astype(jnp.bfloat16)
    o_ref[...] = jnp.dot(a, w_ref[...].astype(jnp.bfloat16),
                         preferred_element_type=jnp.float32) + b_ref[...]


def _ada(cc, w_ada, b_ada):
    depth, d, n = w_ada.shape
    tn = 512
    return pl.pallas_call(
        _ada_kernel,
        grid=(depth, n // tn),
        in_specs=[pl.BlockSpec((16, d), lambda l, j: (0, 0)),
                  pl.BlockSpec((None, d, tn), lambda l, j: (l, 0, j)),
                  pl.BlockSpec((None, 1, tn), lambda l, j: (l, 0, j))],
        out_specs=pl.BlockSpec((None, 16, tn), lambda l, j: (l, 0, j)),
        out_shape=jax.ShapeDtypeStruct((depth, 16, n), jnp.float32),
        name="ada",
        compiler_params=_params("arbitrary", "arbitrary"),
    )(cc, w_ada, b_ada.reshape(depth, 1, n))


def _inproj_kernel(x_ref, mod_ref, ng_ref, w_ref, o_ref, h_ref, *, mod_row):
    j = pl.program_id(2)
    nb, r, d = x_ref.shape
    mrow = pl.program_id(0) if mod_row is None else mod_row

    @pl.when(j == 0)
    def _():
        sh = mod_ref[pl.ds(mrow, 1), 0:d]
        gain = ng_ref[...] * (1.0 + mod_ref[pl.ds(mrow, 1), d:2 * d])
        for s in range(nb):
            x = x_ref[s]
            inv = lax.rsqrt(jnp.mean(x * x, axis=-1, keepdims=True) + EPS)
            h_ref[s * r:(s + 1) * r, :] = (x * inv * gain + sh).astype(jnp.bfloat16)

    acc = lax.dot_general(h_ref[...], w_ref[...], _NT, preferred_element_type=jnp.float32)
    for s in range(nb):
        for c in range(acc.shape[1] // LANE):
            o_ref[s, c] = acc[s * r:(s + 1) * r, c * LANE:(c + 1) * LANE].astype(jnp.bfloat16)


def _inproj(stream, mod, ng, w, layer, *, mod_row, tile_stride=1):
    bsz, r, d = stream.shape
    nb = min(bsz, max(1, TM_IN // r))
    rows = min(r, TM_IN)
    assert bsz % nb == 0 and r % rows == 0
    n_tiles = -(-(w.shape[1] // TN_IN) // tile_stride)
    return pl.pallas_call(
        functools.partial(_inproj_kernel, mod_row=mod_row),
        grid=(bsz // nb, r // rows, n_tiles),
        in_specs=[pl.BlockSpec((nb, rows, d), lambda b, i, j: (b, i, 0)),
                  pl.BlockSpec(mod.shape, lambda b, i, j: (0, 0)),
                  pl.BlockSpec((1, d), lambda b, i, j: (0, 0)),
                  pl.BlockSpec((None, TN_IN, d), lambda b, i, j: (layer, j * tile_stride, 0))],
        out_specs=pl.BlockSpec((nb, CPT, rows, LANE), lambda b, i, j: (b, j, i, 0)),
        out_shape=jax.ShapeDtypeStruct((bsz, n_tiles * CPT, r, LANE), jnp.bfloat16),
        scratch_shapes=[pltpu.VMEM((nb * rows, d), jnp.bfloat16)],
        name="inproj",
        compiler_params=_params("arbitrary", "arbitrary", "arbitrary"),
    )(stream, mod, ng, w)


def _norm_rope(x, gain, cos=None, sin=None, scale=None):
    y = x * lax.rsqrt(jnp.mean(x * x, axis=-1, keepdims=True) + EPS) * gain
    if cos is not None:
        y = y * cos + pltpu.roll(y, 64, 1) * sin
    if scale is not None:
        y = y * scale
    return y


def _chunk_spec(n, rows, chunk0, row_fn):
    return pl.BlockSpec((None, n, rows, LANE), lambda b, i: (b, chunk0 // n, row_fn(i), 0))


def _gated_store(o_ref, col, o, g):
    g = g.astype(jnp.float32)
    o_ref[:, col:col + LANE] = (o * _silu(g)).astype(o_ref.dtype)


def _with_ones(v):
    lane = lax.broadcasted_iota(jnp.int32, v.shape, 1)
    return jnp.concatenate([v, jnp.where(lane == 0, 1.0, 0.0).astype(v.dtype)], axis=-1)


def _softmax_pv(q, pieces, m=None):
    acc = None
    for k, vp, bias in pieces:
        s = lax.dot_general(q, k, _NT, preferred_element_type=jnp.float32)
        if bias is not None:
            rep, (rows, n) = s.shape[0] // bias.shape[0], bias.shape
            s = (s.reshape(rep, rows, n) + bias[None]).reshape(rep * rows, n)
        sb = s.astype(jnp.bfloat16)
        mc = jnp.max(sb, axis=-1, keepdims=True).astype(jnp.float32)
        mn = mc if m is None else jnp.maximum(m.astype(jnp.bfloat16).astype(jnp.float32), mc)
        p = jnp.exp2(sb - mn.astype(jnp.bfloat16))
        t = jnp.dot(p, vp, preferred_element_type=jnp.float32)
        acc = t if acc is None else acc * jnp.exp2(m - mn) + t
        m = mn
    return acc, m


def _normalise(acc, extra=None):
    l = acc[:, LANE:LANE + 1]
    if extra is not None:
        l = l + extra
    return acc[:, 0:LANE] / l


def _key_spans(t):
    cut = (t // 2 + 255) // 256 * 256
    return [(0, cut), (cut, t)]


def _prepare_queries(src_ref, qp_ref, slot, blk, qg_ref, cos_ref=None, sin_ref=None):
    tq = src_ref.shape[1]
    cos = sin = None
    if cos_ref is not None:
        r0 = pl.multiple_of(blk * tq, tq)
        cos, sin = cos_ref[pl.ds(r0, tq), :], sin_ref[pl.ds(r0, tq), :]
    for h in range(N_HEADS):
        qp_ref[slot, h] = _norm_rope(src_ref[h].astype(jnp.float32), qg_ref[...], cos, sin,
                                     HEAD_DIM ** -0.5 * LOG2E).astype(jnp.bfloat16)


def _next_block(i):
    return jnp.minimum(i + 1, pl.num_programs(1) - 1)


def _load_stacked_queries(qp_ref, slot):
    return [jnp.concatenate([qp_ref[slot, 2 * kv], qp_ref[slot, 2 * kv + 1]], axis=0) for kv in range(2)]


def _mla_queries(qn, qr, h, qgn, qgr, cos, sin):
    lane_grp = (lax.broadcasted_iota(jnp.int32, (1, LANE), 1) // (MLA_ROPE // 2)) % 2
    qn = qn.astype(jnp.float32)
    qt = jnp.where(lane_grp == h % 2, qr.astype(jnp.float32), 0.0)
    ms = jnp.sum(qn * qn + qt * qt, axis=-1, keepdims=True) / MLA_QK
    inv = lax.rsqrt(ms + EPS)
    qt = qt * inv * qgr
    if cos is not None:
        qt = qt * cos + pltpu.roll(qt, 64, 1) * sin
    return (jnp.concatenate([qn * inv * qgn, qt], axis=-1) * (MLA_QK ** -0.5 * LOG2E)).astype(jnp.bfloat16)


def _mla_keys_values(ckv, kr, kvg, kgn, kgr, wuk, wuv, cos, sin):
    c = [cj.astype(jnp.float32) for cj in ckv]
    ms = jnp.sum(sum(cj * cj for cj in c), axis=-1, keepdims=True) / MLA_KV_RANK
    inv = lax.rsqrt(ms + EPS)
    cn = jnp.concatenate([c[j] * inv * kvg[j:j + 1, :] for j in range(4)], axis=-1).astype(jnp.bfloat16)
    kn = jnp.dot(cn, wuk, preferred_element_type=jnp.float32)
    vv = jnp.dot(cn, wuv, preferred_element_type=jnp.float32)
    kr = kr.astype(jnp.float32)
    kr_sq = 0.5 * (kr * kr)
    kt = kr * kgr
    if cos is not None:
        kt = kt * cos + pltpu.roll(kt, 64, 1) * sin
    keys, vals = [], []
    for h in range(N_HEADS):
        kh = kn[:, h * LANE:(h + 1) * LANE]
        inv_h = lax.rsqrt(jnp.sum(kh * kh + kr_sq, axis=-1, keepdims=True) / MLA_QK + EPS)
        keys.append(jnp.concatenate([kh * inv_h * kgn, kt * inv_h], axis=-1).astype(jnp.bfloat16))
        vals.append(vv[:, h * LANE:(h + 1) * LANE].astype(jnp.bfloat16))
    return keys, vals


def _gqa_prep(kl_ref, kc_ref, vl_ref, vc_ref, ck_ref, sk_ref, kg_ref, kp_ref, vp_ref, ctx_len):
    for kv in range(2):
        kp_ref[kv, 0:ctx_len, :] = _norm_rope(kc_ref[kv].astype(jnp.float32), kg_ref[...]).astype(jnp.bfloat16)
        kp_ref[kv, ctx_len:, :] = _norm_rope(kl_ref[kv].astype(jnp.float32), kg_ref[...],
                                             ck_ref[...], sk_ref[...]).astype(jnp.bfloat16)
        vp_ref[kv, 0:ctx_len, :] = _with_ones(vc_ref[kv])
        vp_ref[kv, ctx_len:, :] = _with_ones(vl_ref[kv])


def _attn_a_kernel(q_ref, qn_ref, kl_ref, kc_ref, vl_ref, vc_ref, g_ref, ck_ref, sk_ref, qg_ref, kg_ref,
                   o_ref, kp_ref, vp_ref, qp_ref, *, ctx_len):
    i = pl.program_id(1)
    tq = q_ref.shape[1]

    @pl.when(i == 0)
    def _():
        _gqa_prep(kl_ref, kc_ref, vl_ref, vc_ref, ck_ref, sk_ref, kg_ref, kp_ref, vp_ref, ctx_len)
        _prepare_queries(q_ref, qp_ref, 0, 0, qg_ref, ck_ref, sk_ref)

    qs = _load_stacked_queries(qp_ref, i % 2)
    _prepare_queries(qn_ref, qp_ref, 1 - i % 2, _next_block(i), qg_ref, ck_ref, sk_ref)
    for kv in range(2):
        q2 = qs[kv]
        acc, _ = _softmax_pv(q2, [(kp_ref[kv, lo:hi, :], vp_ref[kv, lo:hi, :], None)
                                  for lo, hi in _key_spans(kp_ref.shape[1])])
        o = _normalise(acc)
        for j in range(2):
            _gated_store(o_ref, (2 * kv + j) * LANE, o[j * tq:(j + 1) * tq], g_ref[2 * kv + j])


def _attn_c_kernel(sink_ref, q_ref, qn_ref, kl_ref, kc_ref, vl_ref, vc_ref, g_ref, ck_ref, sk_ref,
                   qg_ref, kg_ref, o_ref, kp_ref, vp_ref, qp_ref, *, ctx_len):
    i = pl.program_id(1)
    tq = q_ref.shape[1]
    n_lat = kl_ref.shape[1]

    @pl.when(i == 0)
    def _():
        _gqa_prep(kl_ref, kc_ref, vl_ref, vc_ref, ck_ref, sk_ref, kg_ref, kp_ref, vp_ref, ctx_len)
        _prepare_queries(q_ref, qp_ref, 0, 0, qg_ref, ck_ref, sk_ref)

    qs = _load_stacked_queries(qp_ref, i % 2)
    _prepare_queries(qn_ref, qp_ref, 1 - i % 2, _next_block(i), qg_ref, ck_ref, sk_ref)

    q0 = i * tq
    ks = jnp.clip(q0 - WINDOW, 0, n_lat - WIN_SPAN)
    row0 = pl.multiple_of(ctx_len + ks, LANE)
    qi = lax.broadcasted_iota(jnp.int32, (tq, WIN_SPAN), 0)
    ki = lax.broadcasted_iota(jnp.int32, (tq, WIN_SPAN), 1)
    wmask = jnp.where(jnp.abs((qi - ki) + (q0 - ks)) <= WINDOW, 0.0, NEG)
    head_row = lax.broadcasted_iota(jnp.int32, (2 * tq, 1), 0) < tq
    for kv in range(2):
        q2 = qs[kv]
        sink = jnp.where(head_row, sink_ref[2 * kv], sink_ref[2 * kv + 1]) * LOG2E
        pieces = [(kp_ref[kv, 0:ctx_len, :], vp_ref[kv, 0:ctx_len, :], None),
                  (kp_ref[kv, pl.ds(row0, WIN_SPAN), :], vp_ref[kv, pl.ds(row0, WIN_SPAN), :], wmask)]
        acc, m = _softmax_pv(q2, pieces, m=sink)
        o = _normalise(acc, jnp.exp2(sink - m))
        for j in range(2):
            _gated_store(o_ref, (2 * kv + j) * LANE, o[j * tq:(j + 1) * tq], g_ref[2 * kv + j])


def _attn_gqa(proj_x, proj_c, cmap, tabs, qg, kg, sink, chunks):
    cq0, ck0, cv0, cg0 = chunks
    bsz, _, n_lat, _ = proj_x.shape
    ctx_len = proj_c.shape[2]
    t = ctx_len + n_lat
    cos, sin = tabs
    nq = n_lat // TQ
    row = lambda i: i
    nxt = lambda i: jnp.minimum(i + 1, nq - 1)
    zero = lambda i: 0
    in_specs = [_chunk_spec(4, TQ, cq0, row), _chunk_spec(4, TQ, cq0, nxt),
                _chunk_spec(2, n_lat, ck0, zero), _chunk_spec(2, ctx_len, cmap(ck0), zero),
                _chunk_spec(2, n_lat, cv0, zero), _chunk_spec(2, ctx_len, cmap(cv0), zero),
                _chunk_spec(4, TQ, cg0, row),
                pl.BlockSpec((n_lat, LANE), lambda b, i: (0, 0)),
                pl.BlockSpec((n_lat, LANE), lambda b, i: (0, 0)),
                pl.BlockSpec((1, LANE), lambda b, i: (0, 0)),
                pl.BlockSpec((1, LANE), lambda b, i: (0, 0))]
    args = [proj_x, proj_x, proj_x, proj_c, proj_x, proj_c, proj_x, cos, sin, qg, kg]
    if sink is None:
        body = _attn_a_kernel
    else:
        body = _attn_c_kernel
        in_specs = [pl.BlockSpec(memory_space=pltpu.SMEM)] + in_specs
        args = [sink] + args
    return pl.pallas_call(
        functools.partial(body, ctx_len=ctx_len),
        grid=(bsz, n_lat // TQ),
        in_specs=in_specs,
        out_specs=pl.BlockSpec((None, TQ, BRANCH_W), lambda b, i: (b, i, 0)),
        out_shape=jax.ShapeDtypeStruct((bsz, n_lat, BRANCH_W), jnp.bfloat16),
        scratch_shapes=[pltpu.VMEM((2, t, LANE), jnp.bfloat16), pltpu.VMEM((2, t, 2 * LANE), jnp.bfloat16),
                        pltpu.VMEM((2, N_HEADS, TQ, LANE), jnp.bfloat16)],
        name="attn_a" if sink is None else "attn_c",
        compiler_params=_params("arbitrary", "arbitrary"),
    )(*args)


def _attn_b_kernel(q_ref, qn_ref, kl_ref, kc_ref, vl_ref, vc_ref, g_ref, bias_ref, qg_ref, kg_ref,
                   o_ref, kp_ref, vp_ref, qp_ref, *, ctx_len):
    i = pl.program_id(1)
    tq = q_ref.shape[1]
    rows = kl_ref.shape[1] // GRID_W
    strip_rows = NA_STRIP // GRID_W

    @pl.when(i == 0)
    def _():
        for h in range(N_HEADS):
            kp_ref[h, 0:ctx_len, :] = _norm_rope(kc_ref[h].astype(jnp.float32), kg_ref[...]).astype(jnp.bfloat16)
            kp_ref[h, ctx_len:, :] = _norm_rope(kl_ref[h].astype(jnp.float32), kg_ref[...]).astype(jnp.bfloat16)
            vp_ref[h, 0:ctx_len, :] = _with_ones(vc_ref[h])
            vp_ref[h, ctx_len:, :] = _with_ones(vl_ref[h])
        _prepare_queries(q_ref, qp_ref, 0, 0, qg_ref)

    qs = [qp_ref[i % 2, h] for h in range(N_HEADS)]
    _prepare_queries(qn_ref, qp_ref, 1 - i % 2, _next_block(i), qg_ref)
    r0 = i * (tq // GRID_W)
    ss = jnp.clip(r0 - NA_KH // 2, 0, rows - strip_rows)
    row0 = pl.multiple_of(ctx_len + ss * GRID_W, LANE)
    for h in range(N_HEADS):
        pieces = [(kp_ref[h, 0:ctx_len, :], vp_ref[h, 0:ctx_len, :], None),
                  (kp_ref[h, pl.ds(row0, NA_STRIP), :], vp_ref[h, pl.ds(row0, NA_STRIP), :], bias_ref[h])]
        acc, _ = _softmax_pv(qs[h], pieces)
        _gated_store(o_ref, h * LANE, _normalise(acc), g_ref[h])


def _na_bias_kernel(rp_ref, o_ref, *, rows):
    q_rows, strip_rows = TQ // GRID_W, NA_STRIP // GRID_W
    qc = lax.broadcasted_iota(jnp.int32, (GRID_W, LANE), 0)
    lane = lax.broadcasted_iota(jnp.int32, (GRID_W, LANE), 1)
    kc = lane & (GRID_W - 1)
    cs = jnp.clip(qc - NA_KW // 2, 0, GRID_W - NA_KW)
    col_ok = (kc >= cs) & (kc < cs + NA_KW)
    second = lane >= GRID_W
    for var, r0 in enumerate((0, q_rows, rows - q_rows)):
        ss = min(max(r0 - NA_KH // 2, 0), rows - strip_rows)
        for j in range(q_rows):
            qr = r0 + j
            rs = min(max(qr - NA_KH // 2, 0), rows - NA_KH)
            for p in range(strip_rows // 2):
                kr0 = ss + 2 * p
                ok0, ok1 = rs <= kr0 < rs + NA_KH, rs <= kr0 + 1 < rs + NA_KH
                if ok0 or ok1:
                    e = kr0 - qr + NA_KH - 1
                    x = jnp.broadcast_to(rp_ref[e + 1:e + 2, :], (GRID_W, LANE))
                    band = pltpu.roll(x, LANE - (NA_KW - 1), 1, stride=1, stride_axis=0)
                    row_ok = second if (ok1 and not ok0) else (~second if (ok0 and not ok1) else None)
                    valid = col_ok if row_ok is None else (col_ok & row_ok)
                    tile = jnp.where(valid, band * LOG2E, NEG)
                else:
                    tile = jnp.full((GRID_W, LANE), NEG, jnp.float32)
                o_ref[var, j * GRID_W:(j + 1) * GRID_W, p * LANE:(p + 1) * LANE] = tile


def _na_bias_tables(rpb, n_lat):
    depth, nh, nr, nc = rpb.shape
    z = jnp.zeros((depth, nh, nr + 2, GRID_W), jnp.float32).at[:, :, 1:nr + 1, :nc].set(rpb)
    rp = jnp.concatenate([z[:, :, :-1], z[:, :, 1:]], axis=-1)
    return pl.pallas_call(
        functools.partial(_na_bias_kernel, rows=n_lat // GRID_W),
        grid=(depth, nh),
        in_specs=[pl.BlockSpec((None, None, nr + 1, LANE), lambda l, h: (l, h, 0, 0))],
        out_specs=pl.BlockSpec((None, 3, None, TQ, NA_STRIP), lambda l, h: (l, 0, h, 0, 0)),
        out_shape=jax.ShapeDtypeStruct((depth, 3, nh, TQ, NA_STRIP), jnp.float32),
        name="na_bias",
        compiler_params=_params("arbitrary", "arbitrary"),
    )(rp)


def _attn_b(proj_x, proj_c, cmap, bias, layer, qg, kg):
    bsz, _, n_lat, _ = proj_x.shape
    ctx_len = proj_c.shape[2]
    t = ctx_len + n_lat
    nq = n_lat // TQ
    row = lambda i: i
    nxt = lambda i: jnp.minimum(i + 1, nq - 1)
    zero = lambda i: 0

    def variant(i):
        return jnp.where(i == 0, 0, jnp.where(i == nq - 1, 2, 1))

    return pl.pallas_call(
        functools.partial(_attn_b_kernel, ctx_len=ctx_len),
        grid=(bsz, nq),
        in_specs=[_chunk_spec(4, TQ, B_Q, row), _chunk_spec(4, TQ, B_Q, nxt),
                  _chunk_spec(4, n_lat, B_K, zero), _chunk_spec(4, ctx_len, cmap(B_K), zero),
                  _chunk_spec(4, n_lat, B_V, zero), _chunk_spec(4, ctx_len, cmap(B_V), zero),
                  _chunk_spec(4, TQ, B_G, row),
                  pl.BlockSpec((None, None, N_HEADS, TQ, NA_STRIP), lambda b, i: (layer, variant(i), 0, 0, 0)),
                  pl.BlockSpec((1, LANE), lambda b, i: (0, 0)),
                  pl.BlockSpec((1, LANE), lambda b, i: (0, 0))],
        out_specs=pl.BlockSpec((None, TQ, BRANCH_W), lambda b, i: (b, i, 0)),
        out_shape=jax.ShapeDtypeStruct((bsz, n_lat, BRANCH_W), jnp.bfloat16),
        scratch_shapes=[pltpu.VMEM((N_HEADS, t, LANE), jnp.bfloat16),
                        pltpu.VMEM((N_HEADS, t, 2 * LANE), jnp.bfloat16),
                        pltpu.VMEM((2, N_HEADS, TQ, LANE), jnp.bfloat16)],
        name="attn_b",
        compiler_params=_params("arbitrary", "arbitrary"),
    )(proj_x, proj_x, proj_x, proj_c, proj_x, proj_c, proj_x, bias, qg, kg)


def _attn_d_kernel(qn_ref, qr_ref, qn2_ref, qr2_ref, ckvl_ref, ckvc_ref, krl_ref, krc_ref, g_ref, ck_ref, sk_ref,
                   qgn_ref, qgr_ref, kgn_ref, kgr_ref, kvg_ref, wuk_ref, wuv_ref,
                   o_ref, kp_ref, vp_ref, qp_ref, *, ctx_len):
    i = pl.program_id(1)
    n_lat = ckvl_ref.shape[1]
    tq = qn_ref.shape[1]

    def prepare_queries(nope_ref, rope_ref, slot, blk):
        r0 = pl.multiple_of(blk * tq, tq)
        cos, sin = ck_ref[pl.ds(r0, tq), :], sk_ref[pl.ds(r0, tq), :]
        for h in range(N_HEADS):
            qp_ref[slot, h] = _mla_queries(nope_ref[h], rope_ref[h // 2], h, qgn_ref[...], qgr_ref[...], cos, sin)

    @pl.when(i == 0)
    def _():
        prepare_queries(qn_ref, qr_ref, 0, 0)

        def fill(dst, ckv_ref, kr_ref, src, n, cos, sin):
            keys, vals = _mla_keys_values([ckv_ref[j, src:src + n, :] for j in range(4)], kr_ref[0, src:src + n, :],
                                          kvg_ref[...], kgn_ref[...], kgr_ref[...], wuk_ref[...], wuv_ref[...],
                                          cos, sin)
            for h in range(N_HEADS):
                kp_ref[h, dst:dst + n, :] = keys[h]
                vp_ref[h, dst:dst + n, :] = _with_ones(vals[h])

        fill(0, ckvc_ref, krc_ref, 0, ctx_len, None, None)
        rc = 512
        for r in range(0, n_lat, rc):
            fill(ctx_len + r, ckvl_ref, krl_ref, r, rc, ck_ref[r:r + rc, :], sk_ref[r:r + rc, :])

    qs = [qp_ref[i % 2, h] for h in range(N_HEADS)]
    prepare_queries(qn2_ref, qr2_ref, 1 - i % 2, _next_block(i))
    for h in range(N_HEADS):
        acc, _ = _softmax_pv(qs[h], [(kp_ref[h, lo:hi, :], vp_ref[h, lo:hi, :], None)
                                     for lo, hi in _key_spans(kp_ref.shape[1])])
        _gated_store(o_ref, h * LANE, _normalise(acc), g_ref[h])


def _attn_d(proj_x, proj_c, cmap, tabs, gains, wuk, wuv, layer):
    bsz, _, n_lat, _ = proj_x.shape
    ctx_len = proj_c.shape[2]
    t = ctx_len + n_lat
    nq = n_lat // TQ
    row = lambda i: i
    nxt = lambda i: jnp.minimum(i + 1, nq - 1)
    zero = lambda i: 0
    cos, sin = tabs
    qgn, qgr, kgn, kgr, kvg = gains
    vec = pl.BlockSpec((1, LANE), lambda b, i: (0, 0))
    wspec = pl.BlockSpec((None,) + wuk.shape[1:], lambda b, i: (layer, 0, 0))
    return pl.pallas_call(
        functools.partial(_attn_d_kernel, ctx_len=ctx_len),
        grid=(bsz, n_lat // TQ),
        in_specs=[_chunk_spec(4, TQ, D_QN, row), _chunk_spec(2, TQ, D_QR, row),
                  _chunk_spec(4, TQ, D_QN, nxt), _chunk_spec(2, TQ, D_QR, nxt),
                  _chunk_spec(4, n_lat, D_CKV, zero), _chunk_spec(4, ctx_len, cmap(D_CKV), zero),
                  _chunk_spec(1, n_lat, D_KR, zero), _chunk_spec(1, ctx_len, cmap(D_KR), zero),
                  _chunk_spec(4, TQ, D_G, row),
                  pl.BlockSpec((n_lat, LANE), lambda b, i: (0, 0)),
                  pl.BlockSpec((n_lat, LANE), lambda b, i: (0, 0)),
                  vec, vec, vec, vec,
                  pl.BlockSpec((4, LANE), lambda b, i: (0, 0)),
                  wspec, wspec],
        out_specs=pl.BlockSpec((None, TQ, BRANCH_W), lambda b, i: (b, i, 0)),
        out_shape=jax.ShapeDtypeStruct((bsz, n_lat, BRANCH_W), jnp.bfloat16),
        scratch_shapes=[pltpu.VMEM((N_HEADS, t, 2 * LANE), jnp.bfloat16),
                        pltpu.VMEM((N_HEADS, t, 2 * LANE), jnp.bfloat16),
                        pltpu.VMEM((2, N_HEADS, TQ, 2 * LANE), jnp.bfloat16)],
        name="attn_d",
        compiler_params=_params("arbitrary", "arbitrary"),
    )(proj_x, proj_x, proj_x, proj_x, proj_x, proj_c, proj_x, proj_c, proj_x, cos, sin,
      qgn, qgr, kgn, kgr, kvg, wuk, wuv)


def _attn_ctx_kernel(sink_ref, pc_ref, gq_ref, gk_ref, dg_ref, kvg_ref, wuk_ref, wuv_ref, o_ref):
    tq = pc_ref.shape[1]
    scale = HEAD_DIM ** -0.5 * LOG2E

    for br, (cq0, ck0, cv0, cg0) in enumerate(((A_Q, A_K, A_V, A_G), (C_Q, C_K, C_V, C_G))):
        gi = 2 * br
        head_row = lax.broadcasted_iota(jnp.int32, (2 * tq, 1), 0) < tq
        for kv in range(2):
            q2 = jnp.concatenate(
                [_norm_rope(pc_ref[cq0 + 2 * kv + j].astype(jnp.float32), gq_ref[gi:gi + 1, :],
                            scale=scale).astype(jnp.bfloat16) for j in range(2)], axis=0)
            k = _norm_rope(pc_ref[ck0 + kv].astype(jnp.float32), gk_ref[gi:gi + 1, :]).astype(jnp.bfloat16)
            vp = _with_ones(pc_ref[cv0 + kv])
            if br == 0:
                acc, _ = _softmax_pv(q2, [(k, vp, None)])
                o = _normalise(acc)
            else:
                sink = jnp.where(head_row, sink_ref[2 * kv], sink_ref[2 * kv + 1]) * LOG2E
                acc, m = _softmax_pv(q2, [(k, vp, None)], m=sink)
                o = _normalise(acc, jnp.exp2(sink - m))
            for j in range(2):
                h = 2 * kv + j
                _gated_store(o_ref, gi * BRANCH_W + h * LANE, o[j * tq:(j + 1) * tq], pc_ref[cg0 + h])

    for h in range(N_HEADS):
        q = _norm_rope(pc_ref[B_Q + h].astype(jnp.float32), gq_ref[1:2, :], scale=scale).astype(jnp.bfloat16)
        k = _norm_rope(pc_ref[B_K + h].astype(jnp.float32), gk_ref[1:2, :]).astype(jnp.bfloat16)
        acc, _ = _softmax_pv(q, [(k, _with_ones(pc_ref[B_V + h]), None)])
        _gated_store(o_ref, BRANCH_W + h * LANE, _normalise(acc), pc_ref[B_G + h])

    keys, vals = _mla_keys_values([pc_ref[D_CKV + j] for j in range(4)], pc_ref[D_KR], kvg_ref[...],
                                  dg_ref[2:3, :], dg_ref[3:4, :], wuk_ref[...], wuv_ref[...], None, None)
    for h in range(N_HEADS):
        q = _mla_queries(pc_ref[D_QN + h], pc_ref[D_QR + h // 2], h, dg_ref[0:1, :], dg_ref[1:2, :], None, None)
        acc, _ = _softmax_pv(q, [(keys[h], _with_ones(vals[h]), None)])
        _gated_store(o_ref, 3 * BRANCH_W + h * LANE, _normalise(acc), pc_ref[D_G + h])


def _attn_ctx(proj_c, sink, gq, gk, dg, kvg, wuk, wuv, layer):
    bsz, nc, ctx_len, _ = proj_c.shape
    full = lambda a: pl.BlockSpec(a.shape, lambda b: (0,) * a.ndim)
    wspec = pl.BlockSpec((None,) + wuk.shape[1:], lambda b: (layer, 0, 0))
    return pl.pallas_call(
        _attn_ctx_kernel,
        grid=(bsz,),
        in_specs=[pl.BlockSpec(memory_space=pltpu.SMEM),
                  pl.BlockSpec((None, nc, ctx_len, LANE), lambda b: (b, 0, 0, 0)),
                  full(gq), full(gk), full(dg), full(kvg), wspec, wspec],
        out_specs=pl.BlockSpec((None, ctx_len, 4 * BRANCH_W), lambda b: (b, 0, 0)),
        out_shape=jax.ShapeDtypeStruct((bsz, ctx_len, 4 * BRANCH_W), jnp.bfloat16),
        name="attn_ctx",
        compiler_params=_params("arbitrary"),
    )(sink, proj_c, gq, gk, dg, kvg, wuk, wuv)


def _outproj_kernel(*refs, n_in, mod_row):
    s_ref, mix_refs, (w_ref, mod_ref, o_ref) = refs[0], refs[1:1 + n_in], refs[1 + n_in:]
    d = s_ref.shape[1]
    acc = None
    for k in range(n_in):
        part = jnp.dot(mix_refs[k][...], w_ref[k], preferred_element_type=jnp.float32)
        acc = part if acc is None else acc + part
    mrow = pl.program_id(0) if mod_row is None else mod_row
    gate = mod_ref[pl.ds(mrow, 1), 2 * d:3 * d]
    o_ref[...] = s_ref[...] + gate * acc


def _outproj(stream, mixes, w, layer, mod, *, mod_row, tm):
    bsz, r, d = stream.shape
    n_in = len(mixes)
    width = w.shape[1] // n_in
    wk = w.reshape(w.shape[0], n_in, width, d)
    mix_specs = [pl.BlockSpec((None, tm, width), lambda b, i, col=(k if m.shape[2] > width else 0): (b, i, col))
                 for k, m in enumerate(mixes)]
    return pl.pallas_call(
        functools.partial(_outproj_kernel, n_in=n_in, mod_row=mod_row),
        grid=(bsz, r // tm),
        in_specs=[pl.BlockSpec((None, tm, d), lambda b, i: (b, i, 0))] + mix_specs + [
            pl.BlockSpec((None,) + wk.shape[1:], lambda b, i: (layer, 0, 0, 0)),
            pl.BlockSpec(mod.shape, lambda b, i: (0, 0))],
        out_specs=pl.BlockSpec((None, tm, d), lambda b, i: (b, i, 0)),
        out_shape=jax.ShapeDtypeStruct((bsz, r, d), jnp.float32),
        name="outproj",
        compiler_params=_params("arbitrary", "arbitrary"),
    )(stream, *mixes, wk, mod)


def _permute_w_d(w_t):
    dep, _, d = w_t.shape
    half = MLA_ROPE // 2
    q = w_t[:, ABC_COLS:ABC_COLS + N_HEADS * MLA_QK, :].reshape(dep, N_HEADS, MLA_QK, d)
    nope = q[:, :, :MLA_NOPE, :].reshape(dep, N_HEADS * MLA_NOPE, d)
    rope = q[:, :, MLA_NOPE:, :].reshape(dep, 2, 2, 2, half, d).transpose(0, 1, 3, 2, 4, 5)
    rope = rope.reshape(dep, N_HEADS * MLA_ROPE, d)
    kr = w_t[:, 6400:6400 + MLA_ROPE, :].reshape(dep, 2, 1, half, d)
    kr = jnp.broadcast_to(kr, (dep, 2, 2, half, d)).reshape(dep, 2 * MLA_ROPE, d)
    pad = jnp.zeros((dep, LANE, d), w_t.dtype)
    return jnp.concatenate([nope, w_t[:, 6464:6464 + BRANCH_W, :], w_t[:, 5888:5888 + MLA_KV_RANK, :],
                            rope, kr, pad], axis=1)


def _w_in_kernel(w_ref, wd_ref, o_ref, *, n_abc):
    j = pl.program_id(1)

    @pl.when(j < n_abc)
    def _():
        o_ref[...] = w_ref[...].astype(jnp.bfloat16)

    @pl.when(j >= n_abc)
    def _():
        o_ref[...] = wd_ref[...].astype(jnp.bfloat16)


def _prep_w_in(w_in):
    depth, d, _ = w_in.shape
    w_t = jnp.swapaxes(w_in, 1, 2)
    wd = _permute_w_d(w_t)
    n_abc = ABC_COLS // TN_IN
    n_d = wd.shape[1] // TN_IN
    return pl.pallas_call(
        functools.partial(_w_in_kernel, n_abc=n_abc),
        grid=(depth, n_abc + n_d),
        in_specs=[pl.BlockSpec((None, TN_IN, d), lambda l, j: (l, jnp.minimum(j, n_abc - 1), 0)),
                  pl.BlockSpec((None, TN_IN, d), lambda l, j: (l, jnp.maximum(j - n_abc, 0), 0))],
        out_specs=pl.BlockSpec((None, TN_IN, d), lambda l, j: (l, j, 0)),
        out_shape=jax.ShapeDtypeStruct((depth, ABC_COLS + wd.shape[1], d), jnp.bfloat16),
        name="w_in_cast",
        compiler_params=_params("arbitrary", "arbitrary"),
    )(w_t, wd)


def _rope_tables(n_lat, rot_dim):
    tpos = jnp.arange(n_lat)
    row = (tpos // GRID_W).astype(jnp.float32)
    col = (tpos % GRID_W).astype(jnp.float32)
    n_freq = rot_dim // 4
    inv_freq = ROPE_THETA ** (-jnp.arange(n_freq, dtype=jnp.float32) / n_freq)
    ang = jnp.concatenate([row[:, None] * inv_freq, col[:, None] * inv_freq], axis=-1)
    cos, sin = jnp.cos(ang), jnp.sin(ang)
    rep = LANE // rot_dim
    return (jnp.concatenate([cos] * (2 * rep), axis=-1),
            jnp.concatenate([-sin] * rep + [sin] * rep, axis=-1))


def _dup_rope_gain(g):
    half = MLA_ROPE // 2
    r1, r2 = g[MLA_NOPE:MLA_NOPE + half], g[MLA_NOPE + half:]
    return g[None, :MLA_NOPE], jnp.concatenate([r1, r1, r2, r2])[None, :]


def kernel(x, c, ctx, c_ctx, norm_g, w_ada, b_ada, w_in, w_out, a_q_g, a_k_g, b_q_g, b_k_g, b_rpb,
           c_q_g, c_k_g, c_sink, d_q_g, d_k_g, d_kv_g, d_w_uk, d_w_uv):
    bsz, n_lat, d = x.shape
    ctx_len = ctx.shape[1]
    depth = w_in.shape[0]
    assert bsz <= 8 and n_lat % TM_IN == 0 and TM_IN % ctx_len == 0 and ctx_len % LANE == 0

    cc = jnp.zeros((16, d), jnp.float32).at[:bsz].set(c).at[8].set(c_ctx)
    mod = _ada(cc, w_ada, b_ada)
    w_in_p = _prep_w_in(w_in)
    w_out_p = w_out.astype(jnp.bfloat16)
    wuk, wuv = d_w_uk.astype(jnp.bfloat16), d_w_uv.astype(jnp.bfloat16)
    tabs_h = _rope_tables(n_lat, HEAD_DIM)
    tabs_r = _rope_tables(n_lat, MLA_ROPE)
    na_bias = _na_bias_tables(b_rpb, n_lat)

    hc = ctx.astype(x.dtype)
    for l in range(depth):
        with_ctx = l < depth - 1
        ng = norm_g[l][None, :]
        d_gains = _dup_rope_gain(d_q_g[l]) + _dup_rope_gain(d_k_g[l])
        kvg = d_kv_g[l].reshape(4, LANE)
        proj_x = _inproj(x, mod[l], ng, w_in_p, l, mod_row=None)
        if with_ctx:
            proj_c, cmap = _inproj(hc, mod[l], ng, w_in_p, l, mod_row=8), (lambda ch: ch)
        else:
            proj_c = _inproj(hc, mod[l], ng, w_in_p, l, mod_row=8, tile_stride=2)
            cmap = lambda ch: (ch // (2 * CPT)) * CPT + ch % CPT
        oa = _attn_gqa(proj_x, proj_c, cmap, tabs_h, a_q_g[l][None, :], a_k_g[l][None, :], None,
                       (A_Q, A_K, A_V, A_G))
        ob = _attn_b(proj_x, proj_c, cmap, na_bias, l, b_q_g[l][None, :], b_k_g[l][None, :])
        oc = _attn_gqa(proj_x, proj_c, cmap, tabs_h, c_q_g[l][None, :], c_k_g[l][None, :], c_sink[l],
                       (C_Q, C_K, C_V, C_G))
        od = _attn_d(proj_x, proj_c, cmap, tabs_r, d_gains + (kvg,), wuk, wuv, l)
        if with_ctx:
            o_ctx = _attn_ctx(proj_c, c_sink[l], jnp.stack([a_q_g[l], b_q_g[l], c_q_g[l]]),
                              jnp.stack([a_k_g[l], b_k_g[l], c_k_g[l]]), jnp.concatenate(d_gains, axis=0),
                              kvg, wuk, wuv, l)
            hc = _outproj(hc, (o_ctx,) * 4, w_out_p, l, mod[l], mod_row=8, tm=ctx_len)
        x = _outproj(x, (oa, ob, oc, od), w_out_p, l, mod[l], mod_row=None, tm=TM_OUT)
    return x
```

```python
import functools

import jax
import jax.numpy as jnp
from jax import lax
from jax.experimental import pallas as pl
from jax.experimental.pallas import tpu as pltpu

GRID_W = 64
HEAD_DIM = 128
BRANCH_W = 512
N_HEADS = 4
NA_KH = 8
NA_KW = 16
WINDOW = 128
MLA_KV_RANK = 512
MLA_NOPE = 128
MLA_ROPE = 64
MLA_QK = MLA_NOPE + MLA_ROPE
ROPE_THETA = 10000.0
EPS = 1e-6
NEG = -1e30
LOG2E = 1.4426950408889634

LANE = 128
TQ = 512
TM_IN = 1024
TN_IN = 1024
TM_OUT = 512
CPT = TN_IN // LANE
N_CHUNKS = 56
ABC_COLS = 5120
NA_STRIP = (NA_KH + TQ // GRID_W) * GRID_W
WIN_SPAN = TQ + 2 * WINDOW
VMEM_LIMIT = 48 * 1024 * 1024

A_Q, A_K, A_V, A_G = 0, 4, 6, 8
B_Q, B_K, B_V, B_G = 12, 16, 20, 24
C_Q, C_K, C_V, C_G = 28, 32, 34, 36
D_QN, D_G, D_CKV, D_QR, D_KR = 40, 44, 48, 52, 54

_NT = (((1,), (1,)), ((), ()))


def _params(*sem):
    return pltpu.CompilerParams(dimension_semantics=sem, vmem_limit_bytes=VMEM_LIMIT)


def _silu(x):
    return x * jax.nn.sigmoid(x)


def _ada_kernel(c_ref, w_ref, b_ref, o_ref):
    a = _silu(c_ref[...]).astype(jnp.bfloat16)
    o_ref[...] = jnp.dot(a, w_ref[...].astype(jnp.bfloat16),
                         preferred_element_type=jnp.float32) + b_ref[...]


def _ada(cc, w_ada, b_ada):
    depth, d, n = w_ada.shape
    tn = 512
    return pl.pallas_call(
        _ada_kernel,
        grid=(depth, n // tn),
        in_specs=[pl.BlockSpec((16, d), lambda l, j: (0, 0)),
                  pl.BlockSpec((None, d, tn), lambda l, j: (l, 0, j)),
                  pl.BlockSpec((None, 1, tn), lambda l, j: (l, 0, j))],
        out_specs=pl.BlockSpec((None, 16, tn), lambda l, j: (l, 0, j)),
        out_shape=jax.ShapeDtypeStruct((depth, 16, n), jnp.float32),
        name="ada",
        compiler_params=_params("arbitrary", "arbitrary"),
    )(cc, w_ada, b_ada.reshape(depth, 1, n))


def _inproj_kernel(x_ref, mod_ref, ng_ref, w_ref, o_ref, h_ref, *, mod_row):
    j = pl.program_id(2)
    nb, r, d = x_ref.shape
    mrow = pl.program_id(0) if mod_row is None else mod_row

    @pl.when(j == 0)
    def _():
        sh = mod_ref[pl.ds(mrow, 1), 0:d]
        gain = ng_ref[...] * (1.0 + mod_ref[pl.ds(mrow, 1), d:2 * d])
        for s in range(nb):
            x = x_ref[s]
            inv = lax.rsqrt(jnp.mean(x * x, axis=-1, keepdims=True) + EPS)
            h_ref[s * r:(s + 1) * r, :] = (x * inv * gain + sh).astype(jnp.bfloat16)

    acc = lax.dot_general(h_ref[...], w_ref[...], _NT, preferred_element_type=jnp.float32)
    for s in range(nb):
        for c in range(acc.shape[1] // LANE):
            o_ref[s, c] = acc[s * r:(s + 1) * r, c * LANE:(c + 1) * LANE].astype(jnp.bfloat16)


def _inproj(stream, mod, ng, w, layer, *, mod_row, tile_stride=1):
    bsz, r, d = stream.shape
    nb = min(bsz, max(1, TM_IN // r))
    rows = min(r, TM_IN)
    assert bsz % nb == 0 and r % rows == 0
    n_tiles = -(-(w.shape[1] // TN_IN) // tile_stride)
    return pl.pallas_call(
        functools.partial(_inproj_kernel, mod_row=mod_row),
        grid=(bsz // nb, r // rows, n_tiles),
        in_specs=[pl.BlockSpec((nb, rows, d), lambda b, i, j: (b, i, 0)),
                  pl.BlockSpec(mod.shape, lambda b, i, j: (0, 0)),
                  pl.BlockSpec((1, d), lambda b, i, j: (0, 0)),
                  pl.BlockSpec((None, TN_IN, d), lambda b, i, j: (layer, j * tile_stride, 0))],
        out_specs=pl.BlockSpec((nb, CPT, rows, LANE), lambda b, i, j: (b, j, i, 0)),
        out_shape=jax.ShapeDtypeStruct((bsz, n_tiles * CPT, r, LANE), jnp.bfloat16),
        scratch_shapes=[pltpu.VMEM((nb * rows, d), jnp.bfloat16)],
        name="inproj",
        compiler_params=_params("arbitrary", "arbitrary", "arbitrary"),
    )(stream, mod, ng, w)


def _norm_rope(x, gain, cos=None, sin=None, scale=None):
    y = x * lax.rsqrt(jnp.mean(x * x, axis=-1, keepdims=True) + EPS) * gain
    if cos is not None:
        y = y * cos + pltpu.roll(y, 64, 1) * sin
    if scale is not None:
        y = y * scale
    return y


def _chunk_spec(n, rows, chunk0, row_fn):
    return pl.BlockSpec((None, n, rows, LANE), lambda b, i: (b, chunk0 // n, row_fn(i), 0))


def _gated_store(o_ref, col, o, g):
    g = g.astype(jnp.float32)
    o_ref[:, col:col + LANE] = (o * _silu(g)).astype(o_ref.dtype)


def _with_ones(v):
    lane = lax.broadcasted_iota(jnp.int32, v.shape, 1)
    return jnp.concatenate([v, jnp.where(lane == 0, 1.0, 0.0).astype(v.dtype)], axis=-1)


def _softmax_pv(qs, pieces, lower=None):
    scores = []
    for q, head_pieces in zip(qs, pieces):
        row = []
        for k, _, bias in head_pieces:
            s = lax.dot_general(q, k, _NT, preferred_element_type=jnp.float32)
            if bias is not None:
                rep, (rows, n) = s.shape[0] // bias.shape[0], bias.shape
                s = (s.reshape(rep, rows, n) + bias[None]).reshape(rep * rows, n)
            row.append(s.astype(jnp.bfloat16))
        scores.append(row)
    maxes = []
    for h, row in enumerate(scores):
        m = functools.reduce(jnp.maximum, [jnp.max(s, axis=-1, keepdims=True) for s in row])
        if lower is not None:
            m = jnp.maximum(m, lower[h].astype(jnp.bfloat16))
        maxes.append(m)
    probs = [[jnp.exp2(s - m) for s in row] for row, m in zip(scores, maxes)]
    accs = []
    for row, head_pieces in zip(probs, pieces):
        acc = None
        for p, (_, vp, _) in zip(row, head_pieces):
            part = jnp.dot(p, vp, preferred_element_type=jnp.float32)
            acc = part if acc is None else acc + part
        accs.append(acc)
    return accs, [m.astype(jnp.float32) for m in maxes]


def _normalise(acc, extra=None):
    l = acc[:, LANE:LANE + 1]
    if extra is not None:
        l = l + extra
    return acc[:, 0:LANE] / l


def _key_spans(t):
    cut = (t // 2 + 255) // 256 * 256
    return [(0, cut), (cut, t)]


def _prepare_queries(src_ref, qp_ref, slot, blk, qg_ref, cos_ref=None, sin_ref=None):
    tq = src_ref.shape[1]
    cos = sin = None
    if cos_ref is not None:
        r0 = pl.multiple_of(blk * tq, tq)
        cos, sin = cos_ref[pl.ds(r0, tq), :], sin_ref[pl.ds(r0, tq), :]
    for h in range(N_HEADS):
        qp_ref[slot, h] = _norm_rope(src_ref[h].astype(jnp.float32), qg_ref[...], cos, sin,
                                     HEAD_DIM ** -0.5 * LOG2E).astype(jnp.bfloat16)


def _next_block(i):
    return jnp.minimum(i + 1, pl.num_programs(1) - 1)


def _load_stacked_queries(qp_ref, slot):
    return [jnp.concatenate([qp_ref[slot, 2 * kv], qp_ref[slot, 2 * kv + 1]], axis=0) for kv in range(2)]


def _mla_queries(qn, qr, h, qgn, qgr, cos, sin):
    lane_grp = (lax.broadcasted_iota(jnp.int32, (1, LANE), 1) // (MLA_ROPE // 2)) % 2
    qn = qn.astype(jnp.float32)
    qt = jnp.where(lane_grp == h % 2, qr.astype(jnp.float32), 0.0)
    ms = jnp.sum(qn * qn + qt * qt, axis=-1, keepdims=True) / MLA_QK
    inv = lax.rsqrt(ms + EPS)
    qt = qt * inv * qgr
    if cos is not None:
        qt = qt * cos + pltpu.roll(qt, 64, 1) * sin
    return (jnp.concatenate([qn * inv * qgn, qt], axis=-1) * (MLA_QK ** -0.5 * LOG2E)).astype(jnp.bfloat16)


def _mla_keys_values(ckv, kr, kvg, kgn, kgr, wuk, wuv, cos, sin):
    c = [cj.astype(jnp.float32) for cj in ckv]
    ms = jnp.sum(sum(cj * cj for cj in c), axis=-1, keepdims=True) / MLA_KV_RANK
    inv = lax.rsqrt(ms + EPS)
    cn = jnp.concatenate([c[j] * inv * kvg[j:j + 1, :] for j in range(4)], axis=-1).astype(jnp.bfloat16)
    kn = jnp.dot(cn, wuk, preferred_element_type=jnp.float32)
    vv = jnp.dot(cn, wuv, preferred_element_type=jnp.float32)
    kr = kr.astype(jnp.float32)
    kr_sq = 0.5 * (kr * kr)
    kt = kr * kgr
    if cos is not None:
        kt = kt * cos + pltpu.roll(kt, 64, 1) * sin
    keys, vals = [], []
    for h in range(N_HEADS):
        kh = kn[:, h * LANE:(h + 1) * LANE]
        inv_h = lax.rsqrt(jnp.sum(kh * kh + kr_sq, axis=-1, keepdims=True) / MLA_QK + EPS)
        keys.append(jnp.concatenate([kh * inv_h * kgn, kt * inv_h], axis=-1).astype(jnp.bfloat16))
        vals.append(vv[:, h * LANE:(h + 1) * LANE].astype(jnp.bfloat16))
    return keys, vals


def _gqa_prep(kl_ref, kc_ref, vl_ref, vc_ref, ck_ref, sk_ref, kg_ref, kp_ref, vp_ref, ctx_len):
    for kv in range(2):
        kp_ref[kv, 0:ctx_len, :] = _norm_rope(kc_ref[kv].astype(jnp.float32), kg_ref[...]).astype(jnp.bfloat16)
        kp_ref[kv, ctx_len:, :] = _norm_rope(kl_ref[kv].astype(jnp.float32), kg_ref[...],
                                             ck_ref[...], sk_ref[...]).astype(jnp.bfloat16)
        vp_ref[kv, 0:ctx_len, :] = _with_ones(vc_ref[kv])
        vp_ref[kv, ctx_len:, :] = _with_ones(vl_ref[kv])


def _attn_a_kernel(q_ref, qn_ref, kl_ref, kc_ref, vl_ref, vc_ref, g_ref, ck_ref, sk_ref, qg_ref, kg_ref,
                   o_ref, kp_ref, vp_ref, qp_ref, *, ctx_len):
    i = pl.program_id(1)
    tq = q_ref.shape[1]

    @pl.when(i == 0)
    def _():
        _gqa_prep(kl_ref, kc_ref, vl_ref, vc_ref, ck_ref, sk_ref, kg_ref, kp_ref, vp_ref, ctx_len)
        _prepare_queries(q_ref, qp_ref, 0, 0, qg_ref, ck_ref, sk_ref)

    qs = _load_stacked_queries(qp_ref, i % 2)
    _prepare_queries(qn_ref, qp_ref, 1 - i % 2, _next_block(i), qg_ref, ck_ref, sk_ref)
    accs, _ = _softmax_pv(qs, [[(kp_ref[kv, lo:hi, :], vp_ref[kv, lo:hi, :], None)
                                for lo, hi in _key_spans(kp_ref.shape[1])] for kv in range(2)])
    for kv in range(2):
        o = _normalise(accs[kv])
        for j in range(2):
            _gated_store(o_ref, (2 * kv + j) * LANE, o[j * tq:(j + 1) * tq], g_ref[2 * kv + j])


def _attn_c_kernel(sink_ref, q_ref, qn_ref, kl_ref, kc_ref, vl_ref, vc_ref, g_ref, ck_ref, sk_ref,
                   qg_ref, kg_ref, o_ref, kp_ref, vp_ref, qp_ref, *, ctx_len):
    i = pl.program_id(1)
    tq = q_ref.shape[1]
    n_lat = kl_ref.shape[1]

    @pl.when(i == 0)
    def _():
        _gqa_prep(kl_ref, kc_ref, vl_ref, vc_ref, ck_ref, sk_ref, kg_ref, kp_ref, vp_ref, ctx_len)
        _prepare_queries(q_ref, qp_ref, 0, 0, qg_ref, ck_ref, sk_ref)

    qs = _load_stacked_queries(qp_ref, i % 2)
    _prepare_queries(qn_ref, qp_ref, 1 - i % 2, _next_block(i), qg_ref, ck_ref, sk_ref)

    q0 = i * tq
    ks = jnp.clip(q0 - WINDOW, 0, n_lat - WIN_SPAN)
    row0 = pl.multiple_of(ctx_len + ks, LANE)
    qi = lax.broadcasted_iota(jnp.int32, (tq, WIN_SPAN), 0)
    ki = lax.broadcasted_iota(jnp.int32, (tq, WIN_SPAN), 1)
    wmask = jnp.where(jnp.abs((qi - ki) + (q0 - ks)) <= WINDOW, 0.0, NEG)
    head_row = lax.broadcasted_iota(jnp.int32, (2 * tq, 1), 0) < tq
    sinks = [jnp.where(head_row, sink_ref[2 * kv], sink_ref[2 * kv + 1]) * LOG2E for kv in range(2)]
    pieces = [[(kp_ref[kv, 0:ctx_len, :], vp_ref[kv, 0:ctx_len, :], None),
               (kp_ref[kv, pl.ds(row0, WIN_SPAN), :], vp_ref[kv, pl.ds(row0, WIN_SPAN), :], wmask)]
              for kv in range(2)]
    accs, ms = _softmax_pv(qs, pieces, lower=sinks)
    for kv in range(2):
        o = _normalise(accs[kv], jnp.exp2(sinks[kv] - ms[kv]))
        for j in range(2):
            _gated_store(o_ref, (2 * kv + j) * LANE, o[j * tq:(j + 1) * tq], g_ref[2 * kv + j])


def _attn_gqa(proj_x, proj_c, cmap, tabs, qg, kg, sink, chunks):
    cq0, ck0, cv0, cg0 = chunks
    bsz, _, n_lat, _ = proj_x.shape
    ctx_len = proj_c.shape[2]
    t = ctx_len + n_lat
    cos, sin = tabs
    nq = n_lat // TQ
    row = lambda i: i
    nxt = lambda i: jnp.minimum(i + 1, nq - 1)
    zero = lambda i: 0
    in_specs = [_chunk_spec(4, TQ, cq0, row), _chunk_spec(4, TQ, cq0, nxt),
                _chunk_spec(2, n_lat, ck0, zero), _chunk_spec(2, ctx_len, cmap(ck0), zero),
                _chunk_spec(2, n_lat, cv0, zero), _chunk_spec(2, ctx_len, cmap(cv0), zero),
                _chunk_spec(4, TQ, cg0, row),
                pl.BlockSpec((n_lat, LANE), lambda b, i: (0, 0)),
                pl.BlockSpec((n_lat, LANE), lambda b, i: (0, 0)),
                pl.BlockSpec((1, LANE), lambda b, i: (0, 0)),
                pl.BlockSpec((1, LANE), lambda b, i: (0, 0))]
    args = [proj_x, proj_x, proj_x, proj_c, proj_x, proj_c, proj_x, cos, sin, qg, kg]
    if sink is None:
        body = _attn_a_kernel
    else:
        body = _attn_c_kernel
        in_specs = [pl.BlockSpec(memory_space=pltpu.SMEM)] + in_specs
        args = [sink] + args
    return pl.pallas_call(
        functools.partial(body, ctx_len=ctx_len),
        grid=(bsz, n_lat // TQ),
        in_specs=in_specs,
        out_specs=pl.BlockSpec((None, TQ, BRANCH_W), lambda b, i: (b, i, 0)),
        out_shape=jax.ShapeDtypeStruct((bsz, n_lat, BRANCH_W), jnp.bfloat16),
        scratch_shapes=[pltpu.VMEM((2, t, LANE), jnp.bfloat16), pltpu.VMEM((2, t, 2 * LANE), jnp.bfloat16),
                        pltpu.VMEM((2, N_HEADS, TQ, LANE), jnp.bfloat16)],
        name="attn_a" if sink is None else "attn_c",
        compiler_params=_params("arbitrary", "arbitrary"),
    )(*args)


def _attn_b_kernel(q_ref, qn_ref, kl_ref, kc_ref, vl_ref, vc_ref, g_ref, bias_ref, qg_ref, kg_ref,
                   o_ref, kp_ref, vp_ref, qp_ref, *, ctx_len):
    i = pl.program_id(1)
    tq = q_ref.shape[1]
    rows = kl_ref.shape[1] // GRID_W
    strip_rows = NA_STRIP // GRID_W

    @pl.when(i == 0)
    def _():
        for h in range(N_HEADS):
            kp_ref[h, 0:ctx_len, :] = _norm_rope(kc_ref[h].astype(jnp.float32), kg_ref[...]).astype(jnp.bfloat16)
            kp_ref[h, ctx_len:, :] = _norm_rope(kl_ref[h].astype(jnp.float32), kg_ref[...]).astype(jnp.bfloat16)
            vp_ref[h, 0:ctx_len, :] = _with_ones(vc_ref[h])
            vp_ref[h, ctx_len:, :] = _with_ones(vl_ref[h])
        _prepare_queries(q_ref, qp_ref, 0, 0, qg_ref)

    qs = [qp_ref[i % 2, h] for h in range(N_HEADS)]
    _prepare_queries(qn_ref, qp_ref, 1 - i % 2, _next_block(i), qg_ref)
    r0 = i * (tq // GRID_W)
    ss = jnp.clip(r0 - NA_KH // 2, 0, rows - strip_rows)
    row0 = pl.multiple_of(ctx_len + ss * GRID_W, LANE)
    pieces = [[(kp_ref[h, 0:ctx_len, :], vp_ref[h, 0:ctx_len, :], None),
               (kp_ref[h, pl.ds(row0, NA_STRIP), :], vp_ref[h, pl.ds(row0, NA_STRIP), :], bias_ref[h])]
              for h in range(N_HEADS)]
    accs, _ = _softmax_pv(qs, pieces)
    for h in range(N_HEADS):
        _gated_store(o_ref, h * LANE, _normalise(accs[h]), g_ref[h])


def _na_bias_kernel(rp_ref, o_ref, *, rows):
    q_rows, strip_rows = TQ // GRID_W, NA_STRIP // GRID_W
    qc = lax.broadcasted_iota(jnp.int32, (GRID_W, LANE), 0)
    lane = lax.broadcasted_iota(jnp.int32, (GRID_W, LANE), 1)
    kc = lane & (GRID_W - 1)
    cs = jnp.clip(qc - NA_KW // 2, 0, GRID_W - NA_KW)
    col_ok = (kc >= cs) & (kc < cs + NA_KW)
    second = lane >= GRID_W
    for var, r0 in enumerate((0, q_rows, rows - q_rows)):
        ss = min(max(r0 - NA_KH // 2, 0), rows - strip_rows)
        for j in range(q_rows):
            qr = r0 + j
            rs = min(max(qr - NA_KH // 2, 0), rows - NA_KH)
            for p in range(strip_rows // 2):
                kr0 = ss + 2 * p
                ok0, ok1 = rs <= kr0 < rs + NA_KH, rs <= kr0 + 1 < rs + NA_KH
                if ok0 or ok1:
                    e = kr0 - qr + NA_KH - 1
                    x = jnp.broadcast_to(rp_ref[e + 1:e + 2, :], (GRID_W, LANE))
                    band = pltpu.roll(x, LANE - (NA_KW - 1), 1, stride=1, stride_axis=0)
                    row_ok = second if (ok1 and not ok0) else (~second if (ok0 and not ok1) else None)
                    valid = col_ok if row_ok is None else (col_ok & row_ok)
                    tile = jnp.where(valid, band * LOG2E, NEG)
                else:
                    tile = jnp.full((GRID_W, LANE), NEG, jnp.float32)
                o_ref[var, j * GRID_W:(j + 1) * GRID_W, p * LANE:(p + 1) * LANE] = tile


def _na_bias_tables(rpb, n_lat):
    depth, nh, nr, nc = rpb.shape
    z = jnp.zeros((depth, nh, nr + 2, GRID_W), jnp.float32).at[:, :, 1:nr + 1, :nc].set(rpb)
    rp = jnp.concatenate([z[:, :, :-1], z[:, :, 1:]], axis=-1)
    return pl.pallas_call(
        functools.partial(_na_bias_kernel, rows=n_lat // GRID_W),
        grid=(depth, nh),
        in_specs=[pl.BlockSpec((None, None, nr + 1, LANE), lambda l, h: (l, h, 0, 0))],
        out_specs=pl.BlockSpec((None, 3, None, TQ, NA_STRIP), lambda l, h: (l, 0, h, 0, 0)),
        out_shape=jax.ShapeDtypeStruct((depth, 3, nh, TQ, NA_STRIP), jnp.float32),
        name="na_bias",
        compiler_params=_params("arbitrary", "arbitrary"),
    )(rp)


def _attn_b(proj_x, proj_c, cmap, bias, layer, qg, kg):
    bsz, _, n_lat, _ = proj_x.shape
    ctx_len = proj_c.shape[2]
    t = ctx_len + n_lat
    nq = n_lat // TQ
    row = lambda i: i
    nxt = lambda i: jnp.minimum(i + 1, nq - 1)
    zero = lambda i: 0

    def variant(i):
        return jnp.where(i == 0, 0, jnp.where(i == nq - 1, 2, 1))

    return pl.pallas_call(
        functools.partial(_attn_b_kernel, ctx_len=ctx_len),
        grid=(bsz, nq),
        in_specs=[_chunk_spec(4, TQ, B_Q, row), _chunk_spec(4, TQ, B_Q, nxt),
                  _chunk_spec(4, n_lat, B_K, zero), _chunk_spec(4, ctx_len, cmap(B_K), zero),
                  _chunk_spec(4, n_lat, B_V, zero), _chunk_spec(4, ctx_len, cmap(B_V), zero),
                  _chunk_spec(4, TQ, B_G, row),
                  pl.BlockSpec((None, None, N_HEADS, TQ, NA_STRIP), lambda b, i: (layer, variant(i), 0, 0, 0)),
                  pl.BlockSpec((1, LANE), lambda b, i: (0, 0)),
                  pl.BlockSpec((1, LANE), lambda b, i: (0, 0))],
        out_specs=pl.BlockSpec((None, TQ, BRANCH_W), lambda b, i: (b, i, 0)),
        out_shape=jax.ShapeDtypeStruct((bsz, n_lat, BRANCH_W), jnp.bfloat16),
        scratch_shapes=[pltpu.VMEM((N_HEADS, t, LANE), jnp.bfloat16),
                        pltpu.VMEM((N_HEADS, t, 2 * LANE), jnp.bfloat16),
                        pltpu.VMEM((2, N_HEADS, TQ, LANE), jnp.bfloat16)],
        name="attn_b",
        compiler_params=_params("arbitrary", "arbitrary"),
    )(proj_x, proj_x, proj_x, proj_c, proj_x, proj_c, proj_x, bias, qg, kg)


def _attn_d_kernel(qn_ref, qr_ref, qn2_ref, qr2_ref, ckvl_ref, ckvc_ref, krl_ref, krc_ref, g_ref, ck_ref, sk_ref,
                   qgn_ref, qgr_ref, kgn_ref, kgr_ref, kvg_ref, wuk_ref, wuv_ref,
                   o_ref, kp_ref, vp_ref, qp_ref, *, ctx_len):
    i = pl.program_id(1)
    n_lat = ckvl_ref.shape[1]
    tq = qn_ref.shape[1]

    def prepare_queries(nope_ref, rope_ref, slot, blk):
        r0 = pl.multiple_of(blk * tq, tq)
        cos, sin = ck_ref[pl.ds(r0, tq), :], sk_ref[pl.ds(r0, tq), :]
        for h in range(N_HEADS):
            qp_ref[slot, h] = _mla_queries(nope_ref[h], rope_ref[h // 2], h, qgn_ref[...], qgr_ref[...], cos, sin)

    @pl.when(i == 0)
    def _():
        prepare_queries(qn_ref, qr_ref, 0, 0)

        def fill(dst, ckv_ref, kr_ref, src, n, cos, sin):
            keys, vals = _mla_keys_values([ckv_ref[j, src:src + n, :] for j in range(4)], kr_ref[0, src:src + n, :],
                                          kvg_ref[...], kgn_ref[...], kgr_ref[...], wuk_ref[...], wuv_ref[...],
                                          cos, sin)
            for h in range(N_HEADS):
                kp_ref[h, dst:dst + n, :] = keys[h]
                vp_ref[h, dst:dst + n, :] = _with_ones(vals[h])

        fill(0, ckvc_ref, krc_ref, 0, ctx_len, None, None)
        rc = 512
        for r in range(0, n_lat, rc):
            fill(ctx_len + r, ckvl_ref, krl_ref, r, rc, ck_ref[r:r + rc, :], sk_ref[r:r + rc, :])

    qs = [qp_ref[i % 2, h] for h in range(N_HEADS)]
    prepare_queries(qn2_ref, qr2_ref, 1 - i % 2, _next_block(i))
    accs, _ = _softmax_pv(qs, [[(kp_ref[h, lo:hi, :], vp_ref[h, lo:hi, :], None)
                                for lo, hi in _key_spans(kp_ref.shape[1])] for h in range(N_HEADS)])
    for h in range(N_HEADS):
        _gated_store(o_ref, h * LANE, _normalise(accs[h]), g_ref[h])


def _attn_d(proj_x, proj_c, cmap, tabs, gains, wuk, wuv, layer):
    bsz, _, n_lat, _ = proj_x.shape
    ctx_len = proj_c.shape[2]
    t = ctx_len + n_lat
    nq = n_lat // TQ
    row = lambda i: i
    nxt = lambda i: jnp.minimum(i + 1, nq - 1)
    zero = lambda i: 0
    cos, sin = tabs
    qgn, qgr, kgn, kgr, kvg = gains
    vec = pl.BlockSpec((1, LANE), lambda b, i: (0, 0))
    wspec = pl.BlockSpec((None,) + wuk.shape[1:], lambda b, i: (layer, 0, 0))
    return pl.pallas_call(
        functools.partial(_attn_d_kernel, ctx_len=ctx_len),
        grid=(bsz, n_lat // TQ),
        in_specs=[_chunk_spec(4, TQ, D_QN, row), _chunk_spec(2, TQ, D_QR, row),
                  _chunk_spec(4, TQ, D_QN, nxt), _chunk_spec(2, TQ, D_QR, nxt),
                  _chunk_spec(4, n_lat, D_CKV, zero), _chunk_spec(4, ctx_len, cmap(D_CKV), zero),
                  _chunk_spec(1, n_lat, D_KR, zero), _chunk_spec(1, ctx_len, cmap(D_KR), zero),
                  _chunk_spec(4, TQ, D_G, row),
                  pl.BlockSpec((n_lat, LANE), lambda b, i: (0, 0)),
                  pl.BlockSpec((n_lat, LANE), lambda b, i: (0, 0)),
                  vec, vec, vec, vec,
                  pl.BlockSpec((4, LANE), lambda b, i: (0, 0)),
                  wspec, wspec],
        out_specs=pl.BlockSpec((None, TQ, BRANCH_W), lambda b, i: (b, i, 0)),
        out_shape=jax.ShapeDtypeStruct((bsz, n_lat, BRANCH_W), jnp.bfloat16),
        scratch_shapes=[pltpu.VMEM((N_HEADS, t, 2 * LANE), jnp.bfloat16),
                        pltpu.VMEM((N_HEADS, t, 2 * LANE), jnp.bfloat16),
                        pltpu.VMEM((2, N_HEADS, TQ, 2 * LANE), jnp.bfloat16)],
        name="attn_d",
        compiler_params=_params("arbitrary", "arbitrary"),
    )(proj_x, proj_x, proj_x, proj_x, proj_x, proj_c, proj_x, proj_c, proj_x, cos, sin,
      qgn, qgr, kgn, kgr, kvg, wuk, wuv)


def _attn_ctx_kernel(sink_ref, pc_ref, gq_ref, gk_ref, dg_ref, kvg_ref, wuk_ref, wuv_ref, o_ref):
    tq = pc_ref.shape[1]
    scale = HEAD_DIM ** -0.5 * LOG2E

    def prepared(chunk, gains, row, q_scale=None):
        return _norm_rope(pc_ref[chunk].astype(jnp.float32), gains[row:row + 1, :], scale=q_scale).astype(jnp.bfloat16)

    head_row = lax.broadcasted_iota(jnp.int32, (2 * tq, 1), 0) < tq
    for br, (cq0, ck0, cv0, cg0) in ((0, (A_Q, A_K, A_V, A_G)), (2, (C_Q, C_K, C_V, C_G))):
        qs = [jnp.concatenate([prepared(cq0 + 2 * kv + j, gq_ref, br, scale) for j in range(2)], axis=0)
              for kv in range(2)]
        pieces = [[(prepared(ck0 + kv, gk_ref, br), _with_ones(pc_ref[cv0 + kv]), None)] for kv in range(2)]
        if br == 0:
            accs, _ = _softmax_pv(qs, pieces)
            outs = [_normalise(acc) for acc in accs]
        else:
            sinks = [jnp.where(head_row, sink_ref[2 * kv], sink_ref[2 * kv + 1]) * LOG2E for kv in range(2)]
            accs, ms = _softmax_pv(qs, pieces, lower=sinks)
            outs = [_normalise(acc, jnp.exp2(sk - m)) for acc, sk, m in zip(accs, sinks, ms)]
        for kv in range(2):
            for j in range(2):
                h = 2 * kv + j
                _gated_store(o_ref, br * BRANCH_W + h * LANE, outs[kv][j * tq:(j + 1) * tq], pc_ref[cg0 + h])

    accs, _ = _softmax_pv([prepared(B_Q + h, gq_ref, 1, scale) for h in range(N_HEADS)],
                          [[(prepared(B_K + h, gk_ref, 1), _with_ones(pc_ref[B_V + h]), None)]
                           for h in range(N_HEADS)])
    for h in range(N_HEADS):
        _gated_store(o_ref, BRANCH_W + h * LANE, _normalise(accs[h]), pc_ref[B_G + h])

    keys, vals = _mla_keys_values([pc_ref[D_CKV + j] for j in range(4)], pc_ref[D_KR], kvg_ref[...],
                                  dg_ref[2:3, :], dg_ref[3:4, :], wuk_ref[...], wuv_ref[...], None, None)
    qs = [_mla_queries(pc_ref[D_QN + h], pc_ref[D_QR + h // 2], h, dg_ref[0:1, :], dg_ref[1:2, :], None, None)
          for h in range(N_HEADS)]
    accs, _ = _softmax_pv(qs, [[(keys[h], _with_ones(vals[h]), None)] for h in range(N_HEADS)])
    for h in range(N_HEADS):
        _gated_store(o_ref, 3 * BRANCH_W + h * LANE, _normalise(accs[h]), pc_ref[D_G + h])


def _attn_ctx(proj_c, sink, gq, gk, dg, kvg, wuk, wuv, layer):
    bsz, nc, ctx_len, _ = proj_c.shape
    full = lambda a: pl.BlockSpec(a.shape, lambda b: (0,) * a.ndim)
    wspec = pl.BlockSpec((None,) + wuk.shape[1:], lambda b: (layer, 0, 0))
    return pl.pallas_call(
        _attn_ctx_kernel,
        grid=(bsz,),
        in_specs=[pl.BlockSpec(memory_space=pltpu.SMEM),
                  pl.BlockSpec((None, nc, ctx_len, LANE), lambda b: (b, 0, 0, 0)),
                  full(gq), full(gk), full(dg), full(kvg), wspec, wspec],
        out_specs=pl.BlockSpec((None, ctx_len, 4 * BRANCH_W), lambda b: (b, 0, 0)),
        out_shape=jax.ShapeDtypeStruct((bsz, ctx_len, 4 * BRANCH_W), jnp.bfloat16),
        name="attn_ctx",
        compiler_params=_params("arbitrary"),
    )(sink, proj_c, gq, gk, dg, kvg, wuk, wuv)


def _outproj_kernel(*refs, n_in, mod_row):
    s_ref, mix_refs, (w_ref, mod_ref, o_ref) = refs[0], refs[1:1 + n_in], refs[1 + n_in:]
    d = s_ref.shape[1]
    acc = None
    for k in range(n_in):
        part = jnp.dot(mix_refs[k][...], w_ref[k], preferred_element_type=jnp.float32)
        acc = part if acc is None else acc + part
    mrow = pl.program_id(0) if mod_row is None else mod_row
    gate = mod_ref[pl.ds(mrow, 1), 2 * d:3 * d]
    o_ref[...] = s_ref[...] + gate * acc


def _outproj(stream, mixes, w, layer, mod, *, mod_row, tm):
    bsz, r, d = stream.shape
    n_in = len(mixes)
    width = w.shape[1] // n_in
    wk = w.reshape(w.shape[0], n_in, width, d)
    mix_specs = [pl.BlockSpec((None, tm, width), lambda b, i, col=(k if m.shape[2] > width else 0): (b, i, col))
                 for k, m in enumerate(mixes)]
    return pl.pallas_call(
        functools.partial(_outproj_kernel, n_in=n_in, mod_row=mod_row),
        grid=(bsz, r // tm),
        in_specs=[pl.BlockSpec((None, tm, d), lambda b, i: (b, i, 0))] + mix_specs + [
            pl.BlockSpec((None,) + wk.shape[1:], lambda b, i: (layer, 0, 0, 0)),
            pl.BlockSpec(mod.shape, lambda b, i: (0, 0))],
        out_specs=pl.BlockSpec((None, tm, d), lambda b, i: (b, i, 0)),
        out_shape=jax.ShapeDtypeStruct((bsz, r, d), jnp.float32),
        name="outproj",
        compiler_params=_params("arbitrary", "arbitrary"),
    )(stream, *mixes, wk, mod)


def _permute_w_d(w_t):
    dep, _, d = w_t.shape
    half = MLA_ROPE // 2
    q = w_t[:, ABC_COLS:ABC_COLS + N_HEADS * MLA_QK, :].reshape(dep, N_HEADS, MLA_QK, d)
    nope = q[:, :, :MLA_NOPE, :].reshape(dep, N_HEADS * MLA_NOPE, d)
    rope = q[:, :, MLA_NOPE:, :].reshape(dep, 2, 2, 2, half, d).transpose(0, 1, 3, 2, 4, 5)
    rope = rope.reshape(dep, N_HEADS * MLA_ROPE, d)
    kr = w_t[:, 6400:6400 + MLA_ROPE, :].reshape(dep, 2, 1, half, d)
    kr = jnp.broadcast_to(kr, (dep, 2, 2, half, d)).reshape(dep, 2 * MLA_ROPE, d)
    pad = jnp.zeros((dep, LANE, d), w_t.dtype)
    return jnp.concatenate([nope, w_t[:, 6464:6464 + BRANCH_W, :], w_t[:, 5888:5888 + MLA_KV_RANK, :],
                            rope, kr, pad], axis=1)


def _w_in_kernel(w_ref, wd_ref, o_ref, *, n_abc):
    j = pl.program_id(1)

    @pl.when(j < n_abc)
    def _():
        o_ref[...] = w_ref[...].astype(jnp.bfloat16)

    @pl.when(j >= n_abc)
    def _():
        o_ref[...] = wd_ref[...].astype(jnp.bfloat16)


def _prep_w_in(w_in):
    depth, d, _ = w_in.shape
    w_t = jnp.swapaxes(w_in, 1, 2)
    wd = _permute_w_d(w_t)
    n_abc = ABC_COLS // TN_IN
    n_d = wd.shape[1] // TN_IN
    return pl.pallas_call(
        functools.partial(_w_in_kernel, n_abc=n_abc),
        grid=(depth, n_abc + n_d),
        in_specs=[pl.BlockSpec((None, TN_IN, d), lambda l, j: (l, jnp.minimum(j, n_abc - 1), 0)),
                  pl.BlockSpec((None, TN_IN, d), lambda l, j: (l, jnp.maximum(j - n_abc, 0), 0))],
        out_specs=pl.BlockSpec((None, TN_IN, d), lambda l, j: (l, j, 0)),
        out_shape=jax.ShapeDtypeStruct((depth, ABC_COLS + wd.shape[1], d), jnp.bfloat16),
        name="w_in_cast",
        compiler_params=_params("arbitrary", "arbitrary"),
    )(w_t, wd)


def _rope_tables(n_lat, rot_dim):
    tpos = jnp.arange(n_lat)
    row = (tpos // GRID_W).astype(jnp.float32)
    col = (tpos % GRID_W).astype(jnp.float32)
    n_freq = rot_dim // 4
    inv_freq = ROPE_THETA ** (-jnp.arange(n_freq, dtype=jnp.float32) / n_freq)
    ang = jnp.concatenate([row[:, None] * inv_freq, col[:, None] * inv_freq], axis=-1)
    cos, sin = jnp.cos(ang), jnp.sin(ang)
    rep = LANE // rot_dim
    return (jnp.concatenate([cos] * (2 * rep), axis=-1),
            jnp.concatenate([-sin] * rep + [sin] * rep, axis=-1))


def _dup_rope_gain(g):
    half = MLA_ROPE // 2
    r1, r2 = g[MLA_NOPE:MLA_NOPE + half], g[MLA_NOPE + half:]
    return g[None, :MLA_NOPE], jnp.concatenate([r1, r1, r2, r2])[None, :]


def kernel(x, c, ctx, c_ctx, norm_g, w_ada, b_ada, w_in, w_out, a_q_g, a_k_g, b_q_g, b_k_g, b_rpb,
           c_q_g, c_k_g, c_sink, d_q_g, d_k_g, d_kv_g, d_w_uk, d_w_uv):
    bsz, n_lat, d = x.shape
    ctx_len = ctx.shape[1]
    depth = w_in.shape[0]
    assert bsz <= 8 and n_lat % TM_IN == 0 and TM_IN % ctx_len == 0 and ctx_len % LANE == 0

    cc = jnp.zeros((16, d), jnp.float32).at[:bsz].set(c).at[8].set(c_ctx)
    mod = _ada(cc, w_ada, b_ada)
    w_in_p = _prep_w_in(w_in)
    w_out_p = w_out.astype(jnp.bfloat16)
    wuk, wuv = d_w_uk.astype(jnp.bfloat16), d_w_uv.astype(jnp.bfloat16)
    tabs_h = _rope_tables(n_lat, HEAD_DIM)
    tabs_r = _rope_tables(n_lat, MLA_ROPE)
    na_bias = _na_bias_tables(b_rpb, n_lat)

    hc = ctx.astype(x.dtype)
    for l in range(depth):
        with_ctx = l < depth - 1
        ng = norm_g[l][None, :]
        d_gains = _dup_rope_gain(d_q_g[l]) + _dup_rope_gain(d_k_g[l])
        kvg = d_kv_g[l].reshape(4, LANE)
        proj_x = _inproj(x, mod[l], ng, w_in_p, l, mod_row=None)
        if with_ctx:
            proj_c, cmap = _inproj(hc, mod[l], ng, w_in_p, l, mod_row=8), (lambda ch: ch)
        else:
            proj_c = _inproj(hc, mod[l], ng, w_in_p, l, mod_row=8, tile_stride=2)
            cmap = lambda ch: (ch // (2 * CPT)) * CPT + ch % CPT
        oa = _attn_gqa(proj_x, proj_c, cmap, tabs_h, a_q_g[l][None, :], a_k_g[l][None, :], None,
                       (A_Q, A_K, A_V, A_G))
        ob = _attn_b(proj_x, proj_c, cmap, na_bias, l, b_q_g[l][None, :], b_k_g[l][None, :])
        oc = _attn_gqa(proj_x, proj_c, cmap, tabs_h, c_q_g[l][None, :], c_k_g[l][None, :], c_sink[l],
                       (C_Q, C_K, C_V, C_G))
        od = _attn_d(proj_x, proj_c, cmap, tabs_r, d_gains + (kvg,), wuk, wuv, l)
        if with_ctx:
            o_ctx = _attn_ctx(proj_c, c_sink[l], jnp.stack([a_q_g[l], b_q_g[l], c_q_g[l]]),
                              jnp.stack([a_k_g[l], b_k_g[l], c_k_g[l]]), jnp.concatenate(d_gains, axis=0),
                              kvg, wuk, wuv, l)
            hc = _outproj(hc, (o_ctx,) * 4, w_out_p, l, mod[l], mod_row=8, tm=ctx_len)
        x = _outproj(x, (oa, ob, oc, od), w_out_p, l, mod[l], mod_row=None, tm=TM_OUT)
    return x
```

```python
import functools

import jax
import jax.numpy as jnp
from jax import lax
from jax.experimental import pallas as pl
from jax.experimental.pallas import tpu as pltpu

GRID_W = 64
HEAD_DIM = 128
BRANCH_W = 512
N_HEADS = 4
NA_KH = 8
NA_KW = 16
WINDOW = 128
MLA_KV_RANK = 512
MLA_NOPE = 128
MLA_ROPE = 64
MLA_QK = MLA_NOPE + MLA_ROPE
ROPE_THETA = 10000.0
EPS = 1e-6
NEG = -1e30
LOG2E = 1.4426950408889634

LANE = 128
TQ = 512
TM_IN = 1024
TN_IN = 1024
TM_OUT = 512
CPT = TN_IN // LANE
N_CHUNKS = 56
ABC_COLS = 5120
D_CKV_COL = ABC_COLS + N_HEADS * MLA_QK
D_KR_COL = D_CKV_COL + MLA_KV_RANK
D_G_COL = D_KR_COL + MLA_ROPE
NA_STRIP = (NA_KH + TQ // GRID_W) * GRID_W
WIN_SPAN = TQ + 2 * WINDOW
VMEM_LIMIT = 48 * 1024 * 1024

A_Q, A_K, A_V, A_G = 0, 4, 6, 8
B_Q, B_K, B_V, B_G = 12, 16, 20, 24
C_Q, C_K, C_V, C_G = 28, 32, 34, 36
D_QN, D_G, D_CKV, D_QR, D_KR = 40, 44, 48, 52, 54

_NT = (((1,), (1,)), ((), ()))


def _params(*sem):
    return pltpu.CompilerParams(dimension_semantics=sem, vmem_limit_bytes=VMEM_LIMIT)


def _silu(x):
    return x * jax.nn.sigmoid(x)


def _ada_kernel(c_ref, w_ref, b_ref, o_ref):
    a = _silu(c_ref[...]).astype(jnp.bfloat16)
    o_ref[...] = jnp.dot(a, w_ref[...].astype(jnp.bfloat16),
                         preferred_element_type=jnp.float32) + b_ref[...]


def _ada(cc, w_ada, b_ada):
    depth, d, n = w_ada.shape
    tn = 512
    return pl.pallas_call(
        _ada_kernel,
        grid=(depth, n // tn),
        in_specs=[pl.BlockSpec((16, d), lambda l, j: (0, 0)),
                  pl.BlockSpec((None, d, tn), lambda l, j: (l, 0, j)),
                  pl.BlockSpec((None, 1, tn), lambda l, j: (l, 0, j))],
        out_specs=pl.BlockSpec((None, 16, tn), lambda l, j: (l, 0, j)),
        out_shape=jax.ShapeDtypeStruct((depth, 16, n), jnp.float32),
        name="ada",
        compiler_params=_params("arbitrary", "arbitrary"),
    )(cc, w_ada, b_ada.reshape(depth, 1, n))


def _inproj_kernel(x_ref, mod_ref, ng_ref, w_ref, o_ref, h_ref, *, mod_row):
    j = pl.program_id(2)
    nb, r, d = x_ref.shape
    mrow = pl.program_id(0) if mod_row is None else mod_row

    @pl.when(j == 0)
    def _():
        sh = mod_ref[pl.ds(mrow, 1), 0:d]
        gain = ng_ref[...] * (1.0 + mod_ref[pl.ds(mrow, 1), d:2 * d])
        for s in range(nb):
            x = x_ref[s]
            inv = lax.rsqrt(jnp.mean(x * x, axis=-1, keepdims=True) + EPS)
            h_ref[s * r:(s + 1) * r, :] = (x * inv * gain + sh).astype(jnp.bfloat16)

    acc = lax.dot_general(h_ref[...], w_ref[...], _NT, preferred_element_type=jnp.float32)
    for s in range(nb):
        for c in range(acc.shape[1] // LANE):
            o_ref[s, c] = acc[s * r:(s + 1) * r, c * LANE:(c + 1) * LANE].astype(jnp.bfloat16)


def _inproj(stream, mod, ng, w, layer, *, mod_row, tile_stride=1):
    bsz, r, d = stream.shape
    nb = min(bsz, max(1, TM_IN // r))
    rows = min(r, TM_IN)
    assert bsz % nb == 0 and r % rows == 0
    n_tiles = -(-(w.shape[1] // TN_IN) // tile_stride)
    return pl.pallas_call(
        functools.partial(_inproj_kernel, mod_row=mod_row),
        grid=(bsz // nb, r // rows, n_tiles),
        in_specs=[pl.BlockSpec((nb, rows, d), lambda b, i, j: (b, i, 0)),
                  pl.BlockSpec(mod.shape, lambda b, i, j: (0, 0)),
                  pl.BlockSpec((1, d), lambda b, i, j: (0, 0)),
                  pl.BlockSpec((None, TN_IN, d), lambda b, i, j: (layer, j * tile_stride, 0))],
        out_specs=pl.BlockSpec((nb, CPT, rows, LANE), lambda b, i, j: (b, j, i, 0)),
        out_shape=jax.ShapeDtypeStruct((bsz, n_tiles * CPT, r, LANE), jnp.bfloat16),
        scratch_shapes=[pltpu.VMEM((nb * rows, d), jnp.bfloat16)],
        name="inproj",
        compiler_params=_params("arbitrary", "arbitrary", "arbitrary"),
    )(stream, mod, ng, w)


def _norm_rope(x, gain, cos=None, sin=None, scale=None):
    y = x * lax.rsqrt(jnp.mean(x * x, axis=-1, keepdims=True) + EPS) * gain
    if cos is not None:
        y = y * cos + pltpu.roll(y, 64, 1) * sin
    if scale is not None:
        y = y * scale
    return y


def _chunk_spec(n, rows, chunk0, row_fn):
    return pl.BlockSpec((None, n, rows, LANE), lambda b, i: (b, chunk0 // n, row_fn(i), 0))


def _gated_store(o_ref, col, o, g):
    g = g.astype(jnp.float32)
    o_ref[:, col:col + LANE] = (o * _silu(g)).astype(o_ref.dtype)


def _with_ones(v):
    lane = lax.broadcasted_iota(jnp.int32, v.shape, 1)
    return jnp.concatenate([v, jnp.where(lane == 0, 1.0, 0.0).astype(v.dtype)], axis=-1)


def _softmax_pv(qs, pieces, lower=None):
    scores = []
    for q, head_pieces in zip(qs, pieces):
        row = []
        for k, _, bias in head_pieces:
            s = lax.dot_general(q, k, _NT, preferred_element_type=jnp.float32)
            if bias is not None:
                rep, (rows, n) = s.shape[0] // bias.shape[0], bias.shape
                s = (s.reshape(rep, rows, n) + bias[None]).reshape(rep * rows, n)
            row.append(s.astype(jnp.bfloat16))
        scores.append(row)
    maxes = []
    for h, row in enumerate(scores):
        m = functools.reduce(jnp.maximum, [jnp.max(s, axis=-1, keepdims=True) for s in row])
        if lower is not None:
            m = jnp.maximum(m, lower[h].astype(jnp.bfloat16))
        maxes.append(m)
    probs = [[jnp.exp2(s - m) for s in row] for row, m in zip(scores, maxes)]
    accs = []
    for row, head_pieces in zip(probs, pieces):
        acc = None
        for p, (_, vp, _) in zip(row, head_pieces):
            part = jnp.dot(p, vp, preferred_element_type=jnp.float32)
            acc = part if acc is None else acc + part
        accs.append(acc)
    return accs, [m.astype(jnp.float32) for m in maxes]


def _normalise(acc, extra=None):
    l = acc[:, LANE:LANE + 1]
    if extra is not None:
        l = l + extra
    return acc[:, 0:LANE] / l


def _key_spans(t):
    cut = (t // 2 + 255) // 256 * 256
    return [(0, cut), (cut, t)]


def _prepare_queries(src_ref, qp_ref, slot, blk, qg_ref, cos_ref=None, sin_ref=None):
    tq = src_ref.shape[1]
    cos = sin = None
    if cos_ref is not None:
        r0 = pl.multiple_of(blk * tq, tq)
        cos, sin = cos_ref[pl.ds(r0, tq), :], sin_ref[pl.ds(r0, tq), :]
    for h in range(N_HEADS):
        qp_ref[slot, h] = _norm_rope(src_ref[h].astype(jnp.float32), qg_ref[...], cos, sin,
                                     HEAD_DIM ** -0.5 * LOG2E).astype(jnp.bfloat16)


def _next_block(i):
    return jnp.minimum(i + 1, pl.num_programs(1) - 1)


def _load_stacked_queries(qp_ref, slot):
    return [jnp.concatenate([qp_ref[slot, 2 * kv], qp_ref[slot, 2 * kv + 1]], axis=0) for kv in range(2)]


def _mla_queries(qn, qr, h, qgn, qgr, cos, sin):
    lane_grp = (lax.broadcasted_iota(jnp.int32, (1, LANE), 1) // (MLA_ROPE // 2)) % 2
    qn = qn.astype(jnp.float32)
    qt = jnp.where(lane_grp == h % 2, qr.astype(jnp.float32), 0.0)
    ms = jnp.sum(qn * qn + qt * qt, axis=-1, keepdims=True) / MLA_QK
    inv = lax.rsqrt(ms + EPS)
    qt = qt * inv * qgr
    if cos is not None:
        qt = qt * cos + pltpu.roll(qt, 64, 1) * sin
    return (jnp.concatenate([qn * inv * qgn, qt], axis=-1) * (MLA_QK ** -0.5 * LOG2E)).astype(jnp.bfloat16)


def _mla_keys_values(ckv, kr, kvg, kgn, kgr, wuk, wuv, cos, sin):
    c = [cj.astype(jnp.float32) for cj in ckv]
    ms = jnp.sum(sum(cj * cj for cj in c), axis=-1, keepdims=True) / MLA_KV_RANK
    inv = lax.rsqrt(ms + EPS)
    cn = jnp.concatenate([c[j] * inv * kvg[j:j + 1, :] for j in range(4)], axis=-1).astype(jnp.bfloat16)
    kn = jnp.dot(cn, wuk, preferred_element_type=jnp.float32)
    vv = jnp.dot(cn, wuv, preferred_element_type=jnp.float32)
    kr = kr.astype(jnp.float32)
    kr_sq = 0.5 * (kr * kr)
    kt = kr * kgr
    if cos is not None:
        kt = kt * cos + pltpu.roll(kt, 64, 1) * sin
    keys, vals = [], []
    for h in range(N_HEADS):
        kh = kn[:, h * LANE:(h + 1) * LANE]
        inv_h = lax.rsqrt(jnp.sum(kh * kh + kr_sq, axis=-1, keepdims=True) / MLA_QK + EPS)
        keys.append(jnp.concatenate([kh * inv_h * kgn, kt * inv_h], axis=-1).astype(jnp.bfloat16))
        vals.append(vv[:, h * LANE:(h + 1) * LANE].astype(jnp.bfloat16))
    return keys, vals


def _gqa_prep(kl_ref, kc_ref, vl_ref, vc_ref, ck_ref, sk_ref, kg_ref, kp_ref, vp_ref, ctx_len):
    for kv in range(2):
        kp_ref[kv, 0:ctx_len, :] = _norm_rope(kc_ref[kv].astype(jnp.float32), kg_ref[...]).astype(jnp.bfloat16)
        kp_ref[kv, ctx_len:, :] = _norm_rope(kl_ref[kv].astype(jnp.float32), kg_ref[...],
                                             ck_ref[...], sk_ref[...]).astype(jnp.bfloat16)
        vp_ref[kv, 0:ctx_len, :] = _with_ones(vc_ref[kv])
        vp_ref[kv, ctx_len:, :] = _with_ones(vl_ref[kv])


def _attn_a_kernel(q_ref, qn_ref, kl_ref, kc_ref, vl_ref, vc_ref, g_ref, ck_ref, sk_ref, qg_ref, kg_ref,
                   o_ref, kp_ref, vp_ref, qp_ref, *, ctx_len):
    i = pl.program_id(1)
    tq = q_ref.shape[1]

    @pl.when(i == 0)
    def _():
        _gqa_prep(kl_ref, kc_ref, vl_ref, vc_ref, ck_ref, sk_ref, kg_ref, kp_ref, vp_ref, ctx_len)
        _prepare_queries(q_ref, qp_ref, 0, 0, qg_ref, ck_ref, sk_ref)

    qs = _load_stacked_queries(qp_ref, i % 2)
    _prepare_queries(qn_ref, qp_ref, 1 - i % 2, _next_block(i), qg_ref, ck_ref, sk_ref)
    accs, _ = _softmax_pv(qs, [[(kp_ref[kv, lo:hi, :], vp_ref[kv, lo:hi, :], None)
                                for lo, hi in _key_spans(kp_ref.shape[1])] for kv in range(2)])
    for kv in range(2):
        o = _normalise(accs[kv])
        for j in range(2):
            _gated_store(o_ref, (2 * kv + j) * LANE, o[j * tq:(j + 1) * tq], g_ref[2 * kv + j])


def _attn_c_kernel(sink_ref, q_ref, qn_ref, kl_ref, kc_ref, vl_ref, vc_ref, g_ref, ck_ref, sk_ref,
                   qg_ref, kg_ref, o_ref, kp_ref, vp_ref, qp_ref, *, ctx_len):
    i = pl.program_id(1)
    tq = q_ref.shape[1]
    n_lat = kl_ref.shape[1]

    @pl.when(i == 0)
    def _():
        _gqa_prep(kl_ref, kc_ref, vl_ref, vc_ref, ck_ref, sk_ref, kg_ref, kp_ref, vp_ref, ctx_len)
        _prepare_queries(q_ref, qp_ref, 0, 0, qg_ref, ck_ref, sk_ref)

    qs = _load_stacked_queries(qp_ref, i % 2)
    _prepare_queries(qn_ref, qp_ref, 1 - i % 2, _next_block(i), qg_ref, ck_ref, sk_ref)

    q0 = i * tq
    ks = jnp.clip(q0 - WINDOW, 0, n_lat - WIN_SPAN)
    row0 = pl.multiple_of(ctx_len + ks, LANE)
    qi = lax.broadcasted_iota(jnp.int32, (tq, WIN_SPAN), 0)
    ki = lax.broadcasted_iota(jnp.int32, (tq, WIN_SPAN), 1)
    wmask = jnp.where(jnp.abs((qi - ki) + (q0 - ks)) <= WINDOW, 0.0, NEG)
    head_row = lax.broadcasted_iota(jnp.int32, (2 * tq, 1), 0) < tq
    sinks = [jnp.where(head_row, sink_ref[2 * kv], sink_ref[2 * kv + 1]) * LOG2E for kv in range(2)]
    pieces = [[(kp_ref[kv, 0:ctx_len, :], vp_ref[kv, 0:ctx_len, :], None),
               (kp_ref[kv, pl.ds(row0, WIN_SPAN), :], vp_ref[kv, pl.ds(row0, WIN_SPAN), :], wmask)]
              for kv in range(2)]
    accs, ms = _softmax_pv(qs, pieces, lower=sinks)
    for kv in range(2):
        o = _normalise(accs[kv], jnp.exp2(sinks[kv] - ms[kv]))
        for j in range(2):
            _gated_store(o_ref, (2 * kv + j) * LANE, o[j * tq:(j + 1) * tq], g_ref[2 * kv + j])


def _attn_gqa(proj_x, proj_c, cmap, tabs, qg, kg, sink, chunks):
    cq0, ck0, cv0, cg0 = chunks
    bsz, _, n_lat, _ = proj_x.shape
    ctx_len = proj_c.shape[2]
    t = ctx_len + n_lat
    cos, sin = tabs
    nq = n_lat // TQ
    row = lambda i: i
    nxt = lambda i: jnp.minimum(i + 1, nq - 1)
    zero = lambda i: 0
    in_specs = [_chunk_spec(4, TQ, cq0, row), _chunk_spec(4, TQ, cq0, nxt),
                _chunk_spec(2, n_lat, ck0, zero), _chunk_spec(2, ctx_len, cmap(ck0), zero),
                _chunk_spec(2, n_lat, cv0, zero), _chunk_spec(2, ctx_len, cmap(cv0), zero),
                _chunk_spec(4, TQ, cg0, row),
                pl.BlockSpec((n_lat, LANE), lambda b, i: (0, 0)),
                pl.BlockSpec((n_lat, LANE), lambda b, i: (0, 0)),
                pl.BlockSpec((1, LANE), lambda b, i: (0, 0)),
                pl.BlockSpec((1, LANE), lambda b, i: (0, 0))]
    args = [proj_x, proj_x, proj_x, proj_c, proj_x, proj_c, proj_x, cos, sin, qg, kg]
    if sink is None:
        body = _attn_a_kernel
    else:
        body = _attn_c_kernel
        in_specs = [pl.BlockSpec(memory_space=pltpu.SMEM)] + in_specs
        args = [sink] + args
    return pl.pallas_call(
        functools.partial(body, ctx_len=ctx_len),
        grid=(bsz, n_lat // TQ),
        in_specs=in_specs,
        out_specs=pl.BlockSpec((None, TQ, BRANCH_W), lambda b, i: (b, i, 0)),
        out_shape=jax.ShapeDtypeStruct((bsz, n_lat, BRANCH_W), jnp.bfloat16),
        scratch_shapes=[pltpu.VMEM((2, t, LANE), jnp.bfloat16), pltpu.VMEM((2, t, 2 * LANE), jnp.bfloat16),
                        pltpu.VMEM((2, N_HEADS, TQ, LANE), jnp.bfloat16)],
        name="attn_a" if sink is None else "attn_c",
        compiler_params=_params("arbitrary", "arbitrary"),
    )(*args)


def _attn_b_kernel(q_ref, qn_ref, kl_ref, kc_ref, vl_ref, vc_ref, g_ref, bias_ref, qg_ref, kg_ref,
                   o_ref, kp_ref, vp_ref, qp_ref, *, ctx_len):
    i = pl.program_id(1)
    tq = q_ref.shape[1]
    rows = kl_ref.shape[1] // GRID_W
    strip_rows = NA_STRIP // GRID_W

    @pl.when(i == 0)
    def _():
        for h in range(N_HEADS):
            kp_ref[h, 0:ctx_len, :] = _norm_rope(kc_ref[h].astype(jnp.float32), kg_ref[...]).astype(jnp.bfloat16)
            kp_ref[h, ctx_len:, :] = _norm_rope(kl_ref[h].astype(jnp.float32), kg_ref[...]).astype(jnp.bfloat16)
            vp_ref[h, 0:ctx_len, :] = _with_ones(vc_ref[h])
            vp_ref[h, ctx_len:, :] = _with_ones(vl_ref[h])
        _prepare_queries(q_ref, qp_ref, 0, 0, qg_ref)

    qs = [qp_ref[i % 2, h] for h in range(N_HEADS)]
    _prepare_queries(qn_ref, qp_ref, 1 - i % 2, _next_block(i), qg_ref)
    r0 = i * (tq // GRID_W)
    ss = jnp.clip(r0 - NA_KH // 2, 0, rows - strip_rows)
    row0 = pl.multiple_of(ctx_len + ss * GRID_W, LANE)
    pieces = [[(kp_ref[h, 0:ctx_len, :], vp_ref[h, 0:ctx_len, :], None),
               (kp_ref[h, pl.ds(row0, NA_STRIP), :], vp_ref[h, pl.ds(row0, NA_STRIP), :], bias_ref[h])]
              for h in range(N_HEADS)]
    accs, _ = _softmax_pv(qs, pieces)
    for h in range(N_HEADS):
        _gated_store(o_ref, h * LANE, _normalise(accs[h]), g_ref[h])


def _na_bias_kernel(rp_ref, o_ref, *, rows):
    q_rows, strip_rows = TQ // GRID_W, NA_STRIP // GRID_W
    qc = lax.broadcasted_iota(jnp.int32, (GRID_W, LANE), 0)
    lane = lax.broadcasted_iota(jnp.int32, (GRID_W, LANE), 1)
    kc = lane & (GRID_W - 1)
    cs = jnp.clip(qc - NA_KW // 2, 0, GRID_W - NA_KW)
    col_ok = (kc >= cs) & (kc < cs + NA_KW)
    second = lane >= GRID_W
    for var, r0 in enumerate((0, q_rows, rows - q_rows)):
        ss = min(max(r0 - NA_KH // 2, 0), rows - strip_rows)
        for j in range(q_rows):
            qr = r0 + j
            rs = min(max(qr - NA_KH // 2, 0), rows - NA_KH)
            for p in range(strip_rows // 2):
                kr0 = ss + 2 * p
                ok0, ok1 = rs <= kr0 < rs + NA_KH, rs <= kr0 + 1 < rs + NA_KH
                if ok0 or ok1:
                    e = kr0 - qr + NA_KH - 1
                    x = jnp.broadcast_to(rp_ref[e + 1:e + 2, :], (GRID_W, LANE))
                    band = pltpu.roll(x, LANE - (NA_KW - 1), 1, stride=1, stride_axis=0)
                    row_ok = second if (ok1 and not ok0) else (~second if (ok0 and not ok1) else None)
                    valid = col_ok if row_ok is None else (col_ok & row_ok)
                    tile = jnp.where(valid, band * LOG2E, NEG)
                else:
                    tile = jnp.full((GRID_W, LANE), NEG, jnp.float32)
                o_ref[var, j * GRID_W:(j + 1) * GRID_W, p * LANE:(p + 1) * LANE] = tile


def _na_bias_tables(rpb, n_lat):
    depth, nh, nr, nc = rpb.shape
    z = jnp.zeros((depth, nh, nr + 2, GRID_W), jnp.float32).at[:, :, 1:nr + 1, :nc].set(rpb)
    rp = jnp.concatenate([z[:, :, :-1], z[:, :, 1:]], axis=-1)
    return pl.pallas_call(
        functools.partial(_na_bias_kernel, rows=n_lat // GRID_W),
        grid=(depth, nh),
        in_specs=[pl.BlockSpec((None, None, nr + 1, LANE), lambda l, h: (l, h, 0, 0))],
        out_specs=pl.BlockSpec((None, 3, None, TQ, NA_STRIP), lambda l, h: (l, 0, h, 0, 0)),
        out_shape=jax.ShapeDtypeStruct((depth, 3, nh, TQ, NA_STRIP), jnp.float32),
        name="na_bias",
        compiler_params=_params("arbitrary", "arbitrary"),
    )(rp)


def _attn_b(proj_x, proj_c, cmap, bias, layer, qg, kg):
    bsz, _, n_lat, _ = proj_x.shape
    ctx_len = proj_c.shape[2]
    t = ctx_len + n_lat
    nq = n_lat // TQ
    row = lambda i: i
    nxt = lambda i: jnp.minimum(i + 1, nq - 1)
    zero = lambda i: 0

    def variant(i):
        return jnp.where(i == 0, 0, jnp.where(i == nq - 1, 2, 1))

    return pl.pallas_call(
        functools.partial(_attn_b_kernel, ctx_len=ctx_len),
        grid=(bsz, nq),
        in_specs=[_chunk_spec(4, TQ, B_Q, row), _chunk_spec(4, TQ, B_Q, nxt),
                  _chunk_spec(4, n_lat, B_K, zero), _chunk_spec(4, ctx_len, cmap(B_K), zero),
                  _chunk_spec(4, n_lat, B_V, zero), _chunk_spec(4, ctx_len, cmap(B_V), zero),
                  _chunk_spec(4, TQ, B_G, row),
                  pl.BlockSpec((None, None, N_HEADS, TQ, NA_STRIP), lambda b, i: (layer, variant(i), 0, 0, 0)),
                  pl.BlockSpec((1, LANE), lambda b, i: (0, 0)),
                  pl.BlockSpec((1, LANE), lambda b, i: (0, 0))],
        out_specs=pl.BlockSpec((None, TQ, BRANCH_W), lambda b, i: (b, i, 0)),
        out_shape=jax.ShapeDtypeStruct((bsz, n_lat, BRANCH_W), jnp.bfloat16),
        scratch_shapes=[pltpu.VMEM((N_HEADS, t, LANE), jnp.bfloat16),
                        pltpu.VMEM((N_HEADS, t, 2 * LANE), jnp.bfloat16),
                        pltpu.VMEM((2, N_HEADS, TQ, LANE), jnp.bfloat16)],
        name="attn_b",
        compiler_params=_params("arbitrary", "arbitrary"),
    )(proj_x, proj_x, proj_x, proj_c, proj_x, proj_c, proj_x, bias, qg, kg)


def _attn_d_kernel(qn_ref, qr_ref, qn2_ref, qr2_ref, ckvl_ref, ckvc_ref, krl_ref, krc_ref, g_ref, ck_ref, sk_ref,
                   qgn_ref, qgr_ref, kgn_ref, kgr_ref, kvg_ref, wuk_ref, wuv_ref,
                   o_ref, kp_ref, vp_ref, qp_ref, *, ctx_len):
    i = pl.program_id(1)
    n_lat = ckvl_ref.shape[1]
    tq = qn_ref.shape[1]

    def prepare_queries(nope_ref, rope_ref, slot, blk):
        r0 = pl.multiple_of(blk * tq, tq)
        cos, sin = ck_ref[pl.ds(r0, tq), :], sk_ref[pl.ds(r0, tq), :]
        for h in range(N_HEADS):
            qp_ref[slot, h] = _mla_queries(nope_ref[h], rope_ref[h // 2], h, qgn_ref[...], qgr_ref[...], cos, sin)

    @pl.when(i == 0)
    def _():
        prepare_queries(qn_ref, qr_ref, 0, 0)

        def fill(dst, ckv_ref, kr_ref, src, n, cos, sin):
            keys, vals = _mla_keys_values([ckv_ref[j, src:src + n, :] for j in range(4)], kr_ref[0, src:src + n, :],
                                          kvg_ref[...], kgn_ref[...], kgr_ref[...], wuk_ref[...], wuv_ref[...],
                                          cos, sin)
            for h in range(N_HEADS):
                kp_ref[h, dst:dst + n, :] = keys[h]
                vp_ref[h, dst:dst + n, :] = _with_ones(vals[h])

        fill(0, ckvc_ref, krc_ref, 0, ctx_len, None, None)
        rc = 512
        for r in range(0, n_lat, rc):
            fill(ctx_len + r, ckvl_ref, krl_ref, r, rc, ck_ref[r:r + rc, :], sk_ref[r:r + rc, :])

    qs = [qp_ref[i % 2, h] for h in range(N_HEADS)]
    prepare_queries(qn2_ref, qr2_ref, 1 - i % 2, _next_block(i))
    accs, _ = _softmax_pv(qs, [[(kp_ref[h, lo:hi, :], vp_ref[h, lo:hi, :], None)
                                for lo, hi in _key_spans(kp_ref.shape[1])] for h in range(N_HEADS)])
    for h in range(N_HEADS):
        _gated_store(o_ref, h * LANE, _normalise(accs[h]), g_ref[h])


def _attn_d(proj_x, proj_c, cmap, tabs, gains, wuk, wuv, layer):
    bsz, _, n_lat, _ = proj_x.shape
    ctx_len = proj_c.shape[2]
    t = ctx_len + n_lat
    nq = n_lat // TQ
    row = lambda i: i
    nxt = lambda i: jnp.minimum(i + 1, nq - 1)
    zero = lambda i: 0
    cos, sin = tabs
    qgn, qgr, kgn, kgr, kvg = gains
    vec = pl.BlockSpec((1, LANE), lambda b, i: (0, 0))
    wspec = pl.BlockSpec((None,) + wuk.shape[1:], lambda b, i: (layer, 0, 0))
    return pl.pallas_call(
        functools.partial(_attn_d_kernel, ctx_len=ctx_len),
        grid=(bsz, n_lat // TQ),
        in_specs=[_chunk_spec(4, TQ, D_QN, row), _chunk_spec(2, TQ, D_QR, row),
                  _chunk_spec(4, TQ, D_QN, nxt), _chunk_spec(2, TQ, D_QR, nxt),
                  _chunk_spec(4, n_lat, D_CKV, zero), _chunk_spec(4, ctx_len, cmap(D_CKV), zero),
                  _chunk_spec(1, n_lat, D_KR, zero), _chunk_spec(1, ctx_len, cmap(D_KR), zero),
                  _chunk_spec(4, TQ, D_G, row),
                  pl.BlockSpec((n_lat, LANE), lambda b, i: (0, 0)),
                  pl.BlockSpec((n_lat, LANE), lambda b, i: (0, 0)),
                  vec, vec, vec, vec,
                  pl.BlockSpec((4, LANE), lambda b, i: (0, 0)),
                  wspec, wspec],
        out_specs=pl.BlockSpec((None, TQ, BRANCH_W), lambda b, i: (b, i, 0)),
        out_shape=jax.ShapeDtypeStruct((bsz, n_lat, BRANCH_W), jnp.bfloat16),
        scratch_shapes=[pltpu.VMEM((N_HEADS, t, 2 * LANE), jnp.bfloat16),
                        pltpu.VMEM((N_HEADS, t, 2 * LANE), jnp.bfloat16),
                        pltpu.VMEM((2, N_HEADS, TQ, 2 * LANE), jnp.bfloat16)],
        name="attn_d",
        compiler_params=_params("arbitrary", "arbitrary"),
    )(proj_x, proj_x, proj_x, proj_x, proj_x, proj_c, proj_x, proj_c, proj_x, cos, sin,
      qgn, qgr, kgn, kgr, kvg, wuk, wuv)


def _attn_ctx_kernel(sink_ref, pc_ref, gq_ref, gk_ref, dg_ref, kvg_ref, wuk_ref, wuv_ref, o_ref):
    tq = pc_ref.shape[1]
    scale = HEAD_DIM ** -0.5 * LOG2E

    def prepared(chunk, gains, row, q_scale=None):
        return _norm_rope(pc_ref[chunk].astype(jnp.float32), gains[row:row + 1, :], scale=q_scale).astype(jnp.bfloat16)

    head_row = lax.broadcasted_iota(jnp.int32, (2 * tq, 1), 0) < tq
    for br, (cq0, ck0, cv0, cg0) in ((0, (A_Q, A_K, A_V, A_G)), (2, (C_Q, C_K, C_V, C_G))):
        qs = [jnp.concatenate([prepared(cq0 + 2 * kv + j, gq_ref, br, scale) for j in range(2)], axis=0)
              for kv in range(2)]
        pieces = [[(prepared(ck0 + kv, gk_ref, br), _with_ones(pc_ref[cv0 + kv]), None)] for kv in range(2)]
        if br == 0:
            accs, _ = _softmax_pv(qs, pieces)
            outs = [_normalise(acc) for acc in accs]
        else:
            sinks = [jnp.where(head_row, sink_ref[2 * kv], sink_ref[2 * kv + 1]) * LOG2E for kv in range(2)]
            accs, ms = _softmax_pv(qs, pieces, lower=sinks)
            outs = [_normalise(acc, jnp.exp2(sk - m)) for acc, sk, m in zip(accs, sinks, ms)]
        for kv in range(2):
            for j in range(2):
                h = 2 * kv + j
                _gated_store(o_ref, br * BRANCH_W + h * LANE, outs[kv][j * tq:(j + 1) * tq], pc_ref[cg0 + h])

    accs, _ = _softmax_pv([prepared(B_Q + h, gq_ref, 1, scale) for h in range(N_HEADS)],
                          [[(prepared(B_K + h, gk_ref, 1), _with_ones(pc_ref[B_V + h]), None)]
                           for h in range(N_HEADS)])
    for h in range(N_HEADS):
        _gated_store(o_ref, BRANCH_W + h * LANE, _normalise(accs[h]), pc_ref[B_G + h])

    keys, vals = _mla_keys_values([pc_ref[D_CKV + j] for j in range(4)], pc_ref[D_KR], kvg_ref[...],
                                  dg_ref[2:3, :], dg_ref[3:4, :], wuk_ref[...], wuv_ref[...], None, None)
    qs = [_mla_queries(pc_ref[D_QN + h], pc_ref[D_QR + h // 2], h, dg_ref[0:1, :], dg_ref[1:2, :], None, None)
          for h in range(N_HEADS)]
    accs, _ = _softmax_pv(qs, [[(keys[h], _with_ones(vals[h]), None)] for h in range(N_HEADS)])
    for h in range(N_HEADS):
        _gated_store(o_ref, 3 * BRANCH_W + h * LANE, _normalise(accs[h]), pc_ref[D_G + h])


def _attn_ctx(proj_c, sink, gq, gk, dg, kvg, wuk, wuv, layer):
    bsz, nc, ctx_len, _ = proj_c.shape
    full = lambda a: pl.BlockSpec(a.shape, lambda b: (0,) * a.ndim)
    wspec = pl.BlockSpec((None,) + wuk.shape[1:], lambda b: (layer, 0, 0))
    return pl.pallas_call(
        _attn_ctx_kernel,
        grid=(bsz,),
        in_specs=[pl.BlockSpec(memory_space=pltpu.SMEM),
                  pl.BlockSpec((None, nc, ctx_len, LANE), lambda b: (b, 0, 0, 0)),
                  full(gq), full(gk), full(dg), full(kvg), wspec, wspec],
        out_specs=pl.BlockSpec((None, ctx_len, 4 * BRANCH_W), lambda b: (b, 0, 0)),
        out_shape=jax.ShapeDtypeStruct((bsz, ctx_len, 4 * BRANCH_W), jnp.bfloat16),
        name="attn_ctx",
        compiler_params=_params("arbitrary"),
    )(sink, proj_c, gq, gk, dg, kvg, wuk, wuv)


def _outproj_kernel(*refs, n_in, mod_row):
    s_ref, mix_refs, (w_ref, mod_ref, o_ref) = refs[0], refs[1:1 + n_in], refs[1 + n_in:]
    d = s_ref.shape[1]
    acc = None
    for k in range(n_in):
        part = jnp.dot(mix_refs[k][...], w_ref[k], preferred_element_type=jnp.float32)
        acc = part if acc is None else acc + part
    mrow = pl.program_id(0) if mod_row is None else mod_row
    gate = mod_ref[pl.ds(mrow, 1), 2 * d:3 * d]
    o_ref[...] = s_ref[...] + gate * acc


def _outproj(stream, mixes, w, layer, mod, *, mod_row, tm):
    bsz, r, d = stream.shape
    n_in = len(mixes)
    width = w.shape[1] // n_in
    wk = w.reshape(w.shape[0], n_in, width, d)
    mix_specs = [pl.BlockSpec((None, tm, width), lambda b, i, col=(k if m.shape[2] > width else 0): (b, i, col))
                 for k, m in enumerate(mixes)]
    return pl.pallas_call(
        functools.partial(_outproj_kernel, n_in=n_in, mod_row=mod_row),
        grid=(bsz, r // tm),
        in_specs=[pl.BlockSpec((None, tm, d), lambda b, i: (b, i, 0))] + mix_specs + [
            pl.BlockSpec((None,) + wk.shape[1:], lambda b, i: (layer, 0, 0, 0)),
            pl.BlockSpec(mod.shape, lambda b, i: (0, 0))],
        out_specs=pl.BlockSpec((None, tm, d), lambda b, i: (b, i, 0)),
        out_shape=jax.ShapeDtypeStruct((bsz, r, d), jnp.float32),
        name="outproj",
        compiler_params=_params("arbitrary", "arbitrary"),
    )(stream, *mixes, wk, mod)


def _permute_w_d(w_t):
    dep, _, d = w_t.shape
    half = MLA_ROPE // 2
    q = w_t[:, ABC_COLS:ABC_COLS + N_HEADS * MLA_QK, :].reshape(dep, N_HEADS, MLA_QK, d)
    nope = q[:, :, :MLA_NOPE, :].reshape(dep, N_HEADS * MLA_NOPE, d)
    rope = q[:, :, MLA_NOPE:, :].reshape(dep, 2, 2, 2, half, d).transpose(0, 1, 3, 2, 4, 5)
    rope = rope.reshape(dep, N_HEADS * MLA_ROPE, d)
    kr = w_t[:, D_KR_COL:D_KR_COL + MLA_ROPE, :].reshape(dep, 2, 1, half, d)
    kr = jnp.broadcast_to(kr, (dep, 2, 2, half, d)).reshape(dep, 2 * MLA_ROPE, d)
    pad = jnp.zeros((dep, LANE, d), w_t.dtype)
    return jnp.concatenate([nope, w_t[:, D_G_COL:D_G_COL + BRANCH_W, :], w_t[:, D_CKV_COL:D_CKV_COL + MLA_KV_RANK, :],
                            rope, kr, pad], axis=1)


def _w_in_kernel(w_ref, wd_ref, o_ref, *, n_abc):
    j = pl.program_id(1)

    @pl.when(j < n_abc)
    def _():
        o_ref[...] = w_ref[...].astype(jnp.bfloat16)

    @pl.when(j >= n_abc)
    def _():
        o_ref[...] = wd_ref[...].astype(jnp.bfloat16)


def _prep_w_in(w_in):
    depth, d, _ = w_in.shape
    w_t = jnp.swapaxes(w_in, 1, 2)
    wd = _permute_w_d(w_t)
    n_abc = ABC_COLS // TN_IN
    n_d = wd.shape[1] // TN_IN
    return pl.pallas_call(
        functools.partial(_w_in_kernel, n_abc=n_abc),
        grid=(depth, n_abc + n_d),
        in_specs=[pl.BlockSpec((None, TN_IN, d), lambda l, j: (l, jnp.minimum(j, n_abc - 1), 0)),
                  pl.BlockSpec((None, TN_IN, d), lambda l, j: (l, jnp.maximum(j - n_abc, 0), 0))],
        out_specs=pl.BlockSpec((None, TN_IN, d), lambda l, j: (l, j, 0)),
        out_shape=jax.ShapeDtypeStruct((depth, ABC_COLS + wd.shape[1], d), jnp.bfloat16),
        name="w_in_cast",
        compiler_params=_params("arbitrary", "arbitrary"),
    )(w_t, wd)


def _rope_tables(n_lat, rot_dim):
    tpos = jnp.arange(n_lat)
    row = (tpos // GRID_W).astype(jnp.float32)
    col = (tpos % GRID_W).astype(jnp.float32)
    n_freq = rot_dim // 4
    inv_freq = ROPE_THETA ** (-jnp.arange(n_freq, dtype=jnp.float32) / n_freq)
    ang = jnp.concatenate([row[:, None] * inv_freq, col[:, None] * inv_freq], axis=-1)
    cos, sin = jnp.cos(ang), jnp.sin(ang)
    rep = LANE // rot_dim
    return (jnp.concatenate([cos] * (2 * rep), axis=-1),
            jnp.concatenate([-sin] * rep + [sin] * rep, axis=-1))


def _dup_rope_gain(g):
    half = MLA_ROPE // 2
    r1, r2 = g[MLA_NOPE:MLA_NOPE + half], g[MLA_NOPE + half:]
    return g[None, :MLA_NOPE], jnp.concatenate([r1, r1, r2, r2])[None, :]


def kernel(x, c, ctx, c_ctx, norm_g, w_ada, b_ada, w_in, w_out, a_q_g, a_k_g, b_q_g, b_k_g, b_rpb,
           c_q_g, c_k_g, c_sink, d_q_g, d_k_g, d_kv_g, d_w_uk, d_w_uv):
    bsz, n_lat, d = x.shape
    ctx_len = ctx.shape[1]
    depth = w_in.shape[0]
    assert bsz <= 8 and n_lat % TM_IN == 0 and TM_IN % ctx_len == 0 and ctx_len % LANE == 0

    cc = jnp.zeros((16, d), jnp.float32).at[:bsz].set(c).at[8].set(c_ctx)
    mod = _ada(cc, w_ada, b_ada)
    w_in_p = _prep_w_in(w_in)
    w_out_p = w_out.astype(jnp.bfloat16)
    wuk, wuv = d_w_uk.astype(jnp.bfloat16), d_w_uv.astype(jnp.bfloat16)
    tabs_h = _rope_tables(n_lat, HEAD_DIM)
    tabs_r = _rope_tables(n_lat, MLA_ROPE)
    na_bias = _na_bias_tables(b_rpb, n_lat)

    hc = ctx.astype(x.dtype)
    for l in range(depth):
        with_ctx = l < depth - 1
        ng = norm_g[l][None, :]
        d_gains = _dup_rope_gain(d_q_g[l]) + _dup_rope_gain(d_k_g[l])
        kvg = d_kv_g[l].reshape(4, LANE)
        proj_x = _inproj(x, mod[l], ng, w_in_p, l, mod_row=None)
        if with_ctx:
            proj_c, cmap = _inproj(hc, mod[l], ng, w_in_p, l, mod_row=8), (lambda ch: ch)
        else:
            proj_c = _inproj(hc, mod[l], ng, w_in_p, l, mod_row=8, tile_stride=2)
            cmap = lambda ch: (ch // (2 * CPT)) * CPT + ch % CPT
        oa = _attn_gqa(proj_x, proj_c, cmap, tabs_h, a_q_g[l][None, :], a_k_g[l][None, :], None,
                       (A_Q, A_K, A_V, A_G))
        ob = _attn_b(proj_x, proj_c, cmap, na_bias, l, b_q_g[l][None, :], b_k_g[l][None, :])
        oc = _attn_gqa(proj_x, proj_c, cmap, tabs_h, c_q_g[l][None, :], c_k_g[l][None, :], c_sink[l],
                       (C_Q, C_K, C_V, C_G))
        od = _attn_d(proj_x, proj_c, cmap, tabs_r, d_gains + (kvg,), wuk, wuv, l)
        if with_ctx:
            o_ctx = _attn_ctx(proj_c, c_sink[l], jnp.stack([a_q_g[l], b_q_g[l], c_q_g[l]]),
                              jnp.stack([a_k_g[l], b_k_g[l], c_k_g[l]]), jnp.concatenate(d_gains, axis=0),
                              kvg, wuk, wuv, l)
            hc = _outproj(hc, (o_ctx,) * 4, w_out_p, l, mod[l], mod_row=8, tm=ctx_len)
        x = _outproj(x, (oa, ob, oc, od), w_out_p, l, mod[l], mod_row=None, tm=TM_OUT)
    return x
```

```python
import functools

import jax
import jax.numpy as jnp
from jax import lax
from jax.experimental import pallas as pl
from jax.experimental.pallas import tpu as pltpu

GRID_W = 64
HEAD_DIM = 128
BRANCH_W = 512
N_HEADS = 4
NA_KH = 8
NA_KW = 16
WINDOW = 128
MLA_KV_RANK = 512
MLA_NOPE = 128
MLA_ROPE = 64
MLA_QK = MLA_NOPE + MLA_ROPE
ROPE_THETA = 10000.0
EPS = 1e-6
NEG = -1e30
LOG2E = 1.4426950408889634

LANE = 128
TQ = 512
TM_IN = 1024
TN_IN = 1024
TN_IN_LATENT = 1792
TM_OUT = 512
CPT = TN_IN // LANE
N_CHUNKS = 56
ABC_COLS = 5120
D_CKV_COL = ABC_COLS + N_HEADS * MLA_QK
D_KR_COL = D_CKV_COL + MLA_KV_RANK
D_G_COL = D_KR_COL + MLA_ROPE
NA_STRIP = (NA_KH + TQ // GRID_W) * GRID_W
WIN_SPAN = TQ + 2 * WINDOW
VMEM_LIMIT = 48 * 1024 * 1024

A_Q, A_K, A_V, A_G = 0, 4, 6, 8
B_Q, B_K, B_V, B_G = 12, 16, 20, 24
C_Q, C_K, C_V, C_G = 28, 32, 34, 36
D_QN, D_G, D_CKV, D_QR, D_KR = 40, 44, 48, 52, 54

_NT = (((1,), (1,)), ((), ()))


def _params(*sem):
    return pltpu.CompilerParams(dimension_semantics=sem, vmem_limit_bytes=VMEM_LIMIT)


def _silu(x):
    return x * jax.nn.sigmoid(x)


def _ada_kernel(c_ref, w_ref, b_ref, o_ref):
    a = _silu(c_ref[...]).astype(jnp.bfloat16)
    o_ref[...] = jnp.dot(a, w_ref[...].astype(jnp.bfloat16),
                         preferred_element_type=jnp.float32) + b_ref[...]


def _ada(cc, w_ada, b_ada):
    depth, d, n = w_ada.shape
    tn = 1024
    return pl.pallas_call(
        _ada_kernel,
        grid=(depth, n // tn),
        in_specs=[pl.BlockSpec((16, d), lambda l, j: (0, 0)),
                  pl.BlockSpec((None, d, tn), lambda l, j: (l, 0, j)),
                  pl.BlockSpec((None, 1, tn), lambda l, j: (l, 0, j))],
        out_specs=pl.BlockSpec((None, 16, tn), lambda l, j: (l, 0, j)),
        out_shape=jax.ShapeDtypeStruct((depth, 16, n), jnp.float32),
        name="ada",
        compiler_params=_params("arbitrary", "arbitrary"),
    )(cc, w_ada, b_ada.reshape(depth, 1, n))


def _inproj_kernel(x_ref, mod_ref, ng_ref, w_ref, o_ref, h_ref, *, mod_row):
    j = pl.program_id(2)
    nb, r, d = x_ref.shape
    mrow = pl.program_id(0) if mod_row is None else mod_row

    @pl.when(j == 0)
    def _():
        sh = mod_ref[pl.ds(mrow, 1), 0:d]
        gain = ng_ref[...] * (1.0 + mod_ref[pl.ds(mrow, 1), d:2 * d])
        for s in range(nb):
            x = x_ref[s]
            inv = lax.rsqrt(jnp.mean(x * x, axis=-1, keepdims=True) + EPS)
            h_ref[s * r:(s + 1) * r, :] = (x * inv * gain + sh).astype(jnp.bfloat16)

    acc = lax.dot_general(h_ref[...], w_ref[...], _NT, preferred_element_type=jnp.float32)
    for s in range(nb):
        for c in range(acc.shape[1] // LANE):
            o_ref[s, c] = acc[s * r:(s + 1) * r, c * LANE:(c + 1) * LANE].astype(jnp.bfloat16)


def _inproj(stream, mod, ng, w, layer, *, mod_row, tn=TN_IN, tile_stride=1):
    bsz, r, d = stream.shape
    nb = min(bsz, max(1, TM_IN // r))
    rows = min(r, TM_IN)
    assert bsz % nb == 0 and r % rows == 0
    n_tiles = -(-(w.shape[1] // tn) // tile_stride)
    cpt = tn // LANE
    return pl.pallas_call(
        functools.partial(_inproj_kernel, mod_row=mod_row),
        grid=(bsz // nb, r // rows, n_tiles),
        in_specs=[pl.BlockSpec((nb, rows, d), lambda b, i, j: (b, i, 0)),
                  pl.BlockSpec(mod.shape, lambda b, i, j: (0, 0)),
                  pl.BlockSpec((1, d), lambda b, i, j: (0, 0)),
                  pl.BlockSpec((None, tn, d), lambda b, i, j: (layer, j * tile_stride, 0))],
        out_specs=pl.BlockSpec((nb, cpt, rows, LANE), lambda b, i, j: (b, j, i, 0)),
        out_shape=jax.ShapeDtypeStruct((bsz, n_tiles * cpt, r, LANE), jnp.bfloat16),
        scratch_shapes=[pltpu.VMEM((nb * rows, d), jnp.bfloat16)],
        name="inproj",
        compiler_params=_params("arbitrary", "arbitrary", "arbitrary"),
    )(stream, mod, ng, w)


def _norm_rope(x, gain, cos=None, sin=None, scale=None):
    y = x * lax.rsqrt(jnp.mean(x * x, axis=-1, keepdims=True) + EPS) * gain
    if cos is not None:
        y = y * cos + pltpu.roll(y, 64, 1) * sin
    if scale is not None:
        y = y * scale
    return y


def _chunk_spec(n, rows, chunk0, row_fn):
    return pl.BlockSpec((None, n, rows, LANE), lambda b, i: (b, chunk0 // n, row_fn(i), 0))


def _gated_store(o_ref, col, o, g):
    g = g.astype(jnp.float32)
    o_ref[:, col:col + LANE] = (o * _silu(g)).astype(o_ref.dtype)


def _with_ones(v):
    lane = lax.broadcasted_iota(jnp.int32, v.shape, 1)
    return jnp.concatenate([v, jnp.where(lane == 0, 1.0, 0.0).astype(v.dtype)], axis=-1)


def _softmax_pv(qs, pieces, lower=None):
    scores = []
    for q, head_pieces in zip(qs, pieces):
        row = []
        for k, _, bias in head_pieces:
            s = lax.dot_general(q, k, _NT, preferred_element_type=jnp.float32)
            if bias is not None:
                rep, (rows, n) = s.shape[0] // bias.shape[0], bias.shape
                s = (s.reshape(rep, rows, n) + bias[None]).reshape(rep * rows, n)
            row.append(s.astype(jnp.bfloat16))
        scores.append(row)
    maxes = []
    for h, row in enumerate(scores):
        m = functools.reduce(jnp.maximum, [jnp.max(s, axis=-1, keepdims=True) for s in row])
        if lower is not None:
            m = jnp.maximum(m, lower[h].astype(jnp.bfloat16))
        maxes.append(m)
    probs = [[jnp.exp2(s - m) for s in row] for row, m in zip(scores, maxes)]
    accs = []
    for row, head_pieces in zip(probs, pieces):
        acc = None
        for p, (_, vp, _) in zip(row, head_pieces):
            part = jnp.dot(p, vp, preferred_element_type=jnp.float32)
            acc = part if acc is None else acc + part
        accs.append(acc)
    return accs, [m.astype(jnp.float32) for m in maxes]


def _normalise(acc, extra=None):
    l = acc[:, LANE:LANE + 1]
    if extra is not None:
        l = l + extra
    return acc[:, 0:LANE] / l


def _key_spans(t):
    cut = (t // 2 + 255) // 256 * 256
    return [(0, cut), (cut, t)]


def _prepare_queries(src_ref, qp_ref, slot, blk, qg_ref, cos_ref=None, sin_ref=None):
    tq = src_ref.shape[1]
    cos = sin = None
    if cos_ref is not None:
        r0 = pl.multiple_of(blk * tq, tq)
        cos, sin = cos_ref[pl.ds(r0, tq), :], sin_ref[pl.ds(r0, tq), :]
    for h in range(N_HEADS):
        qp_ref[slot, h] = _norm_rope(src_ref[h].astype(jnp.float32), qg_ref[...], cos, sin,
                                     HEAD_DIM ** -0.5 * LOG2E).astype(jnp.bfloat16)


def _next_block(i):
    return jnp.minimum(i + 1, pl.num_programs(1) - 1)


def _load_stacked_queries(qp_ref, slot):
    return [jnp.concatenate([qp_ref[slot, 2 * kv], qp_ref[slot, 2 * kv + 1]], axis=0) for kv in range(2)]


def _mla_queries(qn, qr, h, qgn, qgr, cos, sin):
    lane_grp = (lax.broadcasted_iota(jnp.int32, (1, LANE), 1) // (MLA_ROPE // 2)) % 2
    qn = qn.astype(jnp.float32)
    qt = jnp.where(lane_grp == h % 2, qr.astype(jnp.float32), 0.0)
    ms = jnp.sum(qn * qn + qt * qt, axis=-1, keepdims=True) / MLA_QK
    inv = lax.rsqrt(ms + EPS)
    qt = qt * inv * qgr
    if cos is not None:
        qt = qt * cos + pltpu.roll(qt, 64, 1) * sin
    return (jnp.concatenate([qn * inv * qgn, qt], axis=-1) * (MLA_QK ** -0.5 * LOG2E)).astype(jnp.bfloat16)


def _mla_keys_values(ckv, kr, kvg, kgn, kgr, wuk, wuv, cos, sin):
    c = [cj.astype(jnp.float32) for cj in ckv]
    ms = jnp.sum(sum(cj * cj for cj in c), axis=-1, keepdims=True) / MLA_KV_RANK
    inv = lax.rsqrt(ms + EPS)
    cn = jnp.concatenate([c[j] * inv * kvg[j:j + 1, :] for j in range(4)], axis=-1).astype(jnp.bfloat16)
    kn = jnp.dot(cn, wuk, preferred_element_type=jnp.float32)
    vv = jnp.dot(cn, wuv, preferred_element_type=jnp.float32)
    kr = kr.astype(jnp.float32)
    kr_sq = 0.5 * (kr * kr)
    kt = kr * kgr
    if cos is not None:
        kt = kt * cos + pltpu.roll(kt, 64, 1) * sin
    keys, vals = [], []
    for h in range(N_HEADS):
        kh = kn[:, h * LANE:(h + 1) * LANE]
        inv_h = lax.rsqrt(jnp.sum(kh * kh + kr_sq, axis=-1, keepdims=True) / MLA_QK + EPS)
        keys.append(jnp.concatenate([kh * inv_h * kgn, kt * inv_h], axis=-1).astype(jnp.bfloat16))
        vals.append(vv[:, h * LANE:(h + 1) * LANE].astype(jnp.bfloat16))
    return keys, vals


def _gqa_prep(kl_ref, kc_ref, vl_ref, vc_ref, ck_ref, sk_ref, kg_ref, kp_ref, vp_ref, ctx_len):
    for kv in range(2):
        kp_ref[kv, 0:ctx_len, :] = _norm_rope(kc_ref[kv].astype(jnp.float32), kg_ref[...]).astype(jnp.bfloat16)
        kp_ref[kv, ctx_len:, :] = _norm_rope(kl_ref[kv].astype(jnp.float32), kg_ref[...],
                                             ck_ref[...], sk_ref[...]).astype(jnp.bfloat16)
        vp_ref[kv, 0:ctx_len, :] = _with_ones(vc_ref[kv])
        vp_ref[kv, ctx_len:, :] = _with_ones(vl_ref[kv])


def _attn_a_kernel(q_ref, qn_ref, kl_ref, kc_ref, vl_ref, vc_ref, g_ref, ck_ref, sk_ref, qg_ref, kg_ref,
                   o_ref, kp_ref, vp_ref, qp_ref, *, ctx_len):
    i = pl.program_id(1)
    tq = q_ref.shape[1]

    @pl.when(i == 0)
    def _():
        _gqa_prep(kl_ref, kc_ref, vl_ref, vc_ref, ck_ref, sk_ref, kg_ref, kp_ref, vp_ref, ctx_len)
        _prepare_queries(q_ref, qp_ref, 0, 0, qg_ref, ck_ref, sk_ref)

    qs = _load_stacked_queries(qp_ref, i % 2)
    _prepare_queries(qn_ref, qp_ref, 1 - i % 2, _next_block(i), qg_ref, ck_ref, sk_ref)
    accs, _ = _softmax_pv(qs, [[(kp_ref[kv, lo:hi, :], vp_ref[kv, lo:hi, :], None)
                                for lo, hi in _key_spans(kp_ref.shape[1])] for kv in range(2)])
    for kv in range(2):
        o = _normalise(accs[kv])
        for j in range(2):
            _gated_store(o_ref, (2 * kv + j) * LANE, o[j * tq:(j + 1) * tq], g_ref[2 * kv + j])


def _attn_c_kernel(sink_ref, q_ref, qn_ref, kl_ref, kc_ref, vl_ref, vc_ref, g_ref, ck_ref, sk_ref,
                   qg_ref, kg_ref, o_ref, kp_ref, vp_ref, qp_ref, *, ctx_len):
    i = pl.program_id(1)
    tq = q_ref.shape[1]
    n_lat = kl_ref.shape[1]

    @pl.when(i == 0)
    def _():
        _gqa_prep(kl_ref, kc_ref, vl_ref, vc_ref, ck_ref, sk_ref, kg_ref, kp_ref, vp_ref, ctx_len)
        _prepare_queries(q_ref, qp_ref, 0, 0, qg_ref, ck_ref, sk_ref)

    qs = _load_stacked_queries(qp_ref, i % 2)
    _prepare_queries(qn_ref, qp_ref, 1 - i % 2, _next_block(i), qg_ref, ck_ref, sk_ref)

    q0 = i * tq
    ks = jnp.clip(q0 - WINDOW, 0, n_lat - WIN_SPAN)
    row0 = pl.multiple_of(ctx_len + ks, LANE)
    qi = lax.broadcasted_iota(jnp.int32, (tq, WIN_SPAN), 0)
    ki = lax.broadcasted_iota(jnp.int32, (tq, WIN_SPAN), 1)
    wmask = jnp.where(jnp.abs((qi - ki) + (q0 - ks)) <= WINDOW, 0.0, NEG)
    head_row = lax.broadcasted_iota(jnp.int32, (2 * tq, 1), 0) < tq
    sinks = [jnp.where(head_row, sink_ref[2 * kv], sink_ref[2 * kv + 1]) * LOG2E for kv in range(2)]
    pieces = [[(kp_ref[kv, 0:ctx_len, :], vp_ref[kv, 0:ctx_len, :], None),
               (kp_ref[kv, pl.ds(row0, WIN_SPAN), :], vp_ref[kv, pl.ds(row0, WIN_SPAN), :], wmask)]
              for kv in range(2)]
    accs, ms = _softmax_pv(qs, pieces, lower=sinks)
    for kv in range(2):
        o = _normalise(accs[kv], jnp.exp2(sinks[kv] - ms[kv]))
        for j in range(2):
            _gated_store(o_ref, (2 * kv + j) * LANE, o[j * tq:(j + 1) * tq], g_ref[2 * kv + j])


def _attn_gqa(proj_x, proj_c, cmap, tabs, qg, kg, sink, chunks):
    cq0, ck0, cv0, cg0 = chunks
    bsz, _, n_lat, _ = proj_x.shape
    ctx_len = proj_c.shape[2]
    t = ctx_len + n_lat
    cos, sin = tabs
    nq = n_lat // TQ
    row = lambda i: i
    nxt = lambda i: jnp.minimum(i + 1, nq - 1)
    zero = lambda i: 0
    in_specs = [_chunk_spec(4, TQ, cq0, row), _chunk_spec(4, TQ, cq0, nxt),
                _chunk_spec(2, n_lat, ck0, zero), _chunk_spec(2, ctx_len, cmap(ck0), zero),
                _chunk_spec(2, n_lat, cv0, zero), _chunk_spec(2, ctx_len, cmap(cv0), zero),
                _chunk_spec(4, TQ, cg0, row),
                pl.BlockSpec((n_lat, LANE), lambda b, i: (0, 0)),
                pl.BlockSpec((n_lat, LANE), lambda b, i: (0, 0)),
                pl.BlockSpec((1, LANE), lambda b, i: (0, 0)),
                pl.BlockSpec((1, LANE), lambda b, i: (0, 0))]
    args = [proj_x, proj_x, proj_x, proj_c, proj_x, proj_c, proj_x, cos, sin, qg, kg]
    if sink is None:
        body = _attn_a_kernel
    else:
        body = _attn_c_kernel
        in_specs = [pl.BlockSpec(memory_space=pltpu.SMEM)] + in_specs
        args = [sink] + args
    return pl.pallas_call(
        functools.partial(body, ctx_len=ctx_len),
        grid=(bsz, n_lat // TQ),
        in_specs=in_specs,
        out_specs=pl.BlockSpec((None, TQ, BRANCH_W), lambda b, i: (b, i, 0)),
        out_shape=jax.ShapeDtypeStruct((bsz, n_lat, BRANCH_W), jnp.bfloat16),
        scratch_shapes=[pltpu.VMEM((2, t, LANE), jnp.bfloat16), pltpu.VMEM((2, t, 2 * LANE), jnp.bfloat16),
                        pltpu.VMEM((2, N_HEADS, TQ, LANE), jnp.bfloat16)],
        name="attn_a" if sink is None else "attn_c",
        compiler_params=_params("arbitrary", "arbitrary"),
    )(*args)


def _attn_b_kernel(q_ref, qn_ref, kl_ref, kc_ref, vl_ref, vc_ref, g_ref, bias_ref, qg_ref, kg_ref,
                   o_ref, kp_ref, vp_ref, qp_ref, *, ctx_len):
    i = pl.program_id(1)
    tq = q_ref.shape[1]
    rows = kl_ref.shape[1] // GRID_W
    strip_rows = NA_STRIP // GRID_W

    @pl.when(i == 0)
    def _():
        for h in range(N_HEADS):
            kp_ref[h, 0:ctx_len, :] = _norm_rope(kc_ref[h].astype(jnp.float32), kg_ref[...]).astype(jnp.bfloat16)
            kp_ref[h, ctx_len:, :] = _norm_rope(kl_ref[h].astype(jnp.float32), kg_ref[...]).astype(jnp.bfloat16)
            vp_ref[h, 0:ctx_len, :] = _with_ones(vc_ref[h])
            vp_ref[h, ctx_len:, :] = _with_ones(vl_ref[h])
        _prepare_queries(q_ref, qp_ref, 0, 0, qg_ref)

    qs = [qp_ref[i % 2, h] for h in range(N_HEADS)]
    _prepare_queries(qn_ref, qp_ref, 1 - i % 2, _next_block(i), qg_ref)
    r0 = i * (tq // GRID_W)
    ss = jnp.clip(r0 - NA_KH // 2, 0, rows - strip_rows)
    row0 = pl.multiple_of(ctx_len + ss * GRID_W, LANE)
    pieces = [[(kp_ref[h, 0:ctx_len, :], vp_ref[h, 0:ctx_len, :], None),
               (kp_ref[h, pl.ds(row0, NA_STRIP), :], vp_ref[h, pl.ds(row0, NA_STRIP), :], bias_ref[h])]
              for h in range(N_HEADS)]
    accs, _ = _softmax_pv(qs, pieces)
    for h in range(N_HEADS):
        _gated_store(o_ref, h * LANE, _normalise(accs[h]), g_ref[h])


def _na_bias_kernel(rp_ref, o_ref, *, rows):
    q_rows, strip_rows = TQ // GRID_W, NA_STRIP // GRID_W
    qc = lax.broadcasted_iota(jnp.int32, (GRID_W, LANE), 0)
    lane = lax.broadcasted_iota(jnp.int32, (GRID_W, LANE), 1)
    kc = lane & (GRID_W - 1)
    cs = jnp.clip(qc - NA_KW // 2, 0, GRID_W - NA_KW)
    col_ok = (kc >= cs) & (kc < cs + NA_KW)
    second = lane >= GRID_W
    for var, r0 in enumerate((0, q_rows, rows - q_rows)):
        ss = min(max(r0 - NA_KH // 2, 0), rows - strip_rows)
        for j in range(q_rows):
            qr = r0 + j
            rs = min(max(qr - NA_KH // 2, 0), rows - NA_KH)
            for p in range(strip_rows // 2):
                kr0 = ss + 2 * p
                ok0, ok1 = rs <= kr0 < rs + NA_KH, rs <= kr0 + 1 < rs + NA_KH
                if ok0 or ok1:
                    e = kr0 - qr + NA_KH - 1
                    x = jnp.broadcast_to(rp_ref[e + 1:e + 2, :], (GRID_W, LANE))
                    band = pltpu.roll(x, LANE - (NA_KW - 1), 1, stride=1, stride_axis=0)
                    row_ok = second if (ok1 and not ok0) else (~second if (ok0 and not ok1) else None)
                    valid = col_ok if row_ok is None else (col_ok & row_ok)
                    tile = jnp.where(valid, band * LOG2E, NEG)
                else:
                    tile = jnp.full((GRID_W, LANE), NEG, jnp.float32)
                o_ref[var, j * GRID_W:(j + 1) * GRID_W, p * LANE:(p + 1) * LANE] = tile


def _na_bias_tables(rpb, n_lat):
    depth, nh, nr, nc = rpb.shape
    z = jnp.zeros((depth, nh, nr + 2, GRID_W), jnp.float32).at[:, :, 1:nr + 1, :nc].set(rpb)
    rp = jnp.concatenate([z[:, :, :-1], z[:, :, 1:]], axis=-1)
    return pl.pallas_call(
        functools.partial(_na_bias_kernel, rows=n_lat // GRID_W),
        grid=(depth, nh),
        in_specs=[pl.BlockSpec((None, None, nr + 1, LANE), lambda l, h: (l, h, 0, 0))],
        out_specs=pl.BlockSpec((None, 3, None, TQ, NA_STRIP), lambda l, h: (l, 0, h, 0, 0)),
        out_shape=jax.ShapeDtypeStruct((depth, 3, nh, TQ, NA_STRIP), jnp.float32),
        name="na_bias",
        compiler_params=_params("arbitrary", "arbitrary"),
    )(rp)


def _attn_b(proj_x, proj_c, cmap, bias, layer, qg, kg):
    bsz, _, n_lat, _ = proj_x.shape
    ctx_len = proj_c.shape[2]
    t = ctx_len + n_lat
    nq = n_lat // TQ
    row = lambda i: i
    nxt = lambda i: jnp.minimum(i + 1, nq - 1)
    zero = lambda i: 0

    def variant(i):
        return jnp.where(i == 0, 0, jnp.where(i == nq - 1, 2, 1))

    return pl.pallas_call(
        functools.partial(_attn_b_kernel, ctx_len=ctx_len),
        grid=(bsz, nq),
        in_specs=[_chunk_spec(4, TQ, B_Q, row), _chunk_spec(4, TQ, B_Q, nxt),
                  _chunk_spec(4, n_lat, B_K, zero), _chunk_spec(4, ctx_len, cmap(B_K), zero),
                  _chunk_spec(4, n_lat, B_V, zero), _chunk_spec(4, ctx_len, cmap(B_V), zero),
                  _chunk_spec(4, TQ, B_G, row),
                  pl.BlockSpec((None, None, N_HEADS, TQ, NA_STRIP), lambda b, i: (layer, variant(i), 0, 0, 0)),
                  pl.BlockSpec((1, LANE), lambda b, i: (0, 0)),
                  pl.BlockSpec((1, LANE), lambda b, i: (0, 0))],
        out_specs=pl.BlockSpec((None, TQ, BRANCH_W), lambda b, i: (b, i, 0)),
        out_shape=jax.ShapeDtypeStruct((bsz, n_lat, BRANCH_W), jnp.bfloat16),
        scratch_shapes=[pltpu.VMEM((N_HEADS, t, LANE), jnp.bfloat16),
                        pltpu.VMEM((N_HEADS, t, 2 * LANE), jnp.bfloat16),
                        pltpu.VMEM((2, N_HEADS, TQ, LANE), jnp.bfloat16)],
        name="attn_b",
        compiler_params=_params("arbitrary", "arbitrary"),
    )(proj_x, proj_x, proj_x, proj_c, proj_x, proj_c, proj_x, bias, qg, kg)


def _attn_d_kernel(qn_ref, qr_ref, qn2_ref, qr2_ref, ckvl_ref, ckvc_ref, krl_ref, krc_ref, g_ref, ck_ref, sk_ref,
                   qgn_ref, qgr_ref, kgn_ref, kgr_ref, kvg_ref, wuk_ref, wuv_ref,
                   o_ref, kp_ref, vp_ref, qp_ref, *, ctx_len):
    i = pl.program_id(1)
    n_lat = ckvl_ref.shape[1]
    tq = qn_ref.shape[1]

    def prepare_queries(nope_ref, rope_ref, slot, blk):
        r0 = pl.multiple_of(blk * tq, tq)
        cos, sin = ck_ref[pl.ds(r0, tq), :], sk_ref[pl.ds(r0, tq), :]
        for h in range(N_HEADS):
            qp_ref[slot, h] = _mla_queries(nope_ref[h], rope_ref[h // 2], h, qgn_ref[...], qgr_ref[...], cos, sin)

    @pl.when(i == 0)
    def _():
        prepare_queries(qn_ref, qr_ref, 0, 0)

        def fill(dst, ckv_ref, kr_ref, src, n, cos, sin):
            keys, vals = _mla_keys_values([ckv_ref[j, src:src + n, :] for j in range(4)], kr_ref[0, src:src + n, :],
                                          kvg_ref[...], kgn_ref[...], kgr_ref[...], wuk_ref[...], wuv_ref[...],
                                          cos, sin)
            for h in range(N_HEADS):
                kp_ref[h, dst:dst + n, :] = keys[h]
                vp_ref[h, dst:dst + n, :] = _with_ones(vals[h])

        fill(0, ckvc_ref, krc_ref, 0, ctx_len, None, None)
        rc = 1024
        for r in range(0, n_lat, rc):
            fill(ctx_len + r, ckvl_ref, krl_ref, r, rc, ck_ref[r:r + rc, :], sk_ref[r:r + rc, :])

    qs = [qp_ref[i % 2, h] for h in range(N_HEADS)]
    prepare_queries(qn2_ref, qr2_ref, 1 - i % 2, _next_block(i))
    accs, _ = _softmax_pv(qs, [[(kp_ref[h, lo:hi, :], vp_ref[h, lo:hi, :], None)
                                for lo, hi in _key_spans(kp_ref.shape[1])] for h in range(N_HEADS)])
    for h in range(N_HEADS):
        _gated_store(o_ref, h * LANE, _normalise(accs[h]), g_ref[h])


def _attn_d(proj_x, proj_c, cmap, tabs, gains, wuk, wuv, layer):
    bsz, _, n_lat, _ = proj_x.shape
    ctx_len = proj_c.shape[2]
    t = ctx_len + n_lat
    nq = n_lat // TQ
    row = lambda i: i
    nxt = lambda i: jnp.minimum(i + 1, nq - 1)
    zero = lambda i: 0
    cos, sin = tabs
    qgn, qgr, kgn, kgr, kvg = gains
    vec = pl.BlockSpec((1, LANE), lambda b, i: (0, 0))
    wspec = pl.BlockSpec((None,) + wuk.shape[1:], lambda b, i: (layer, 0, 0))
    return pl.pallas_call(
        functools.partial(_attn_d_kernel, ctx_len=ctx_len),
        grid=(bsz, n_lat // TQ),
        in_specs=[_chunk_spec(4, TQ, D_QN, row), _chunk_spec(2, TQ, D_QR, row),
                  _chunk_spec(4, TQ, D_QN, nxt), _chunk_spec(2, TQ, D_QR, nxt),
                  _chunk_spec(4, n_lat, D_CKV, zero), _chunk_spec(4, ctx_len, cmap(D_CKV), zero),
                  _chunk_spec(1, n_lat, D_KR, zero), _chunk_spec(1, ctx_len, cmap(D_KR), zero),
                  _chunk_spec(4, TQ, D_G, row),
                  pl.BlockSpec((n_lat, LANE), lambda b, i: (0, 0)),
                  pl.BlockSpec((n_lat, LANE), lambda b, i: (0, 0)),
                  vec, vec, vec, vec,
                  pl.BlockSpec((4, LANE), lambda b, i: (0, 0)),
                  wspec, wspec],
        out_specs=pl.BlockSpec((None, TQ, BRANCH_W), lambda b, i: (b, i, 0)),
        out_shape=jax.ShapeDtypeStruct((bsz, n_lat, BRANCH_W), jnp.bfloat16),
        scratch_shapes=[pltpu.VMEM((N_HEADS, t, 2 * LANE), jnp.bfloat16),
                        pltpu.VMEM((N_HEADS, t, 2 * LANE), jnp.bfloat16),
                        pltpu.VMEM((2, N_HEADS, TQ, 2 * LANE), jnp.bfloat16)],
        name="attn_d",
        compiler_params=_params("arbitrary", "arbitrary"),
    )(proj_x, proj_x, proj_x, proj_x, proj_x, proj_c, proj_x, proj_c, proj_x, cos, sin,
      qgn, qgr, kgn, kgr, kvg, wuk, wuv)


def _attn_ctx_kernel(sink_ref, pc_ref, gq_ref, gk_ref, dg_ref, kvg_ref, wuk_ref, wuv_ref, o_ref):
    tq = pc_ref.shape[1]
    scale = HEAD_DIM ** -0.5 * LOG2E

    def prepared(chunk, gains, row, q_scale=None):
        return _norm_rope(pc_ref[chunk].astype(jnp.float32), gains[row:row + 1, :], scale=q_scale).astype(jnp.bfloat16)

    head_row = lax.broadcasted_iota(jnp.int32, (2 * tq, 1), 0) < tq
    for br, (cq0, ck0, cv0, cg0) in ((0, (A_Q, A_K, A_V, A_G)), (2, (C_Q, C_K, C_V, C_G))):
        qs = [jnp.concatenate([prepared(cq0 + 2 * kv + j, gq_ref, br, scale) for j in range(2)], axis=0)
              for kv in range(2)]
        pieces = [[(prepared(ck0 + kv, gk_ref, br), _with_ones(pc_ref[cv0 + kv]), None)] for kv in range(2)]
        if br == 0:
            accs, _ = _softmax_pv(qs, pieces)
            outs = [_normalise(acc) for acc in accs]
        else:
            sinks = [jnp.where(head_row, sink_ref[2 * kv], sink_ref[2 * kv + 1]) * LOG2E for kv in range(2)]
            accs, ms = _softmax_pv(qs, pieces, lower=sinks)
            outs = [_normalise(acc, jnp.exp2(sk - m)) for acc, sk, m in zip(accs, sinks, ms)]
        for kv in range(2):
            for j in range(2):
                h = 2 * kv + j
                _gated_store(o_ref, br * BRANCH_W + h * LANE, outs[kv][j * tq:(j + 1) * tq], pc_ref[cg0 + h])

    accs, _ = _softmax_pv([prepared(B_Q + h, gq_ref, 1, scale) for h in range(N_HEADS)],
                          [[(prepared(B_K + h, gk_ref, 1), _with_ones(pc_ref[B_V + h]), None)]
                           for h in range(N_HEADS)])
    for h in range(N_HEADS):
        _gated_store(o_ref, BRANCH_W + h * LANE, _normalise(accs[h]), pc_ref[B_G + h])

    keys, vals = _mla_keys_values([pc_ref[D_CKV + j] for j in range(4)], pc_ref[D_KR], kvg_ref[...],
                                  dg_ref[2:3, :], dg_ref[3:4, :], wuk_ref[...], wuv_ref[...], None, None)
    qs = [_mla_queries(pc_ref[D_QN + h], pc_ref[D_QR + h // 2], h, dg_ref[0:1, :], dg_ref[1:2, :], None, None)
          for h in range(N_HEADS)]
    accs, _ = _softmax_pv(qs, [[(keys[h], _with_ones(vals[h]), None)] for h in range(N_HEADS)])
    for h in range(N_HEADS):
        _gated_store(o_ref, 3 * BRANCH_W + h * LANE, _normalise(accs[h]), pc_ref[D_G + h])


def _attn_ctx(proj_c, sink, gq, gk, dg, kvg, wuk, wuv, layer):
    bsz, nc, ctx_len, _ = proj_c.shape
    full = lambda a: pl.BlockSpec(a.shape, lambda b: (0,) * a.ndim)
    wspec = pl.BlockSpec((None,) + wuk.shape[1:], lambda b: (layer, 0, 0))
    return pl.pallas_call(
        _attn_ctx_kernel,
        grid=(bsz,),
        in_specs=[pl.BlockSpec(memory_space=pltpu.SMEM),
                  pl.BlockSpec((None, nc, ctx_len, LANE), lambda b: (b, 0, 0, 0)),
                  full(gq), full(gk), full(dg), full(kvg), wspec, wspec],
        out_specs=pl.BlockSpec((None, ctx_len, 4 * BRANCH_W), lambda b: (b, 0, 0)),
        out_shape=jax.ShapeDtypeStruct((bsz, ctx_len, 4 * BRANCH_W), jnp.bfloat16),
        name="attn_ctx",
        compiler_params=_params("arbitrary"),
    )(sink, proj_c, gq, gk, dg, kvg, wuk, wuv)


def _outproj_kernel(*refs, n_in, mod_row):
    s_ref, mix_refs, (w_ref, mod_ref, o_ref) = refs[0], refs[1:1 + n_in], refs[1 + n_in:]
    d = s_ref.shape[1]
    acc = None
    for k in range(n_in):
        part = jnp.dot(mix_refs[k][...], w_ref[k], preferred_element_type=jnp.float32)
        acc = part if acc is None else acc + part
    mrow = pl.program_id(0) if mod_row is None else mod_row
    gate = mod_ref[pl.ds(mrow, 1), 2 * d:3 * d]
    o_ref[...] = s_ref[...] + gate * acc


def _outproj(stream, mixes, w, layer, mod, *, mod_row, tm):
    bsz, r, d = stream.shape
    n_in = len(mixes)
    width = w.shape[1] // n_in
    wk = w.reshape(w.shape[0], n_in, width, d)
    mix_specs = [pl.BlockSpec((None, tm, width), lambda b, i, col=(k if m.shape[2] > width else 0): (b, i, col))
                 for k, m in enumerate(mixes)]
    return pl.pallas_call(
        functools.partial(_outproj_kernel, n_in=n_in, mod_row=mod_row),
        grid=(bsz, r // tm),
        in_specs=[pl.BlockSpec((None, tm, d), lambda b, i: (b, i, 0))] + mix_specs + [
            pl.BlockSpec((None,) + wk.shape[1:], lambda b, i: (layer, 0, 0, 0)),
            pl.BlockSpec(mod.shape, lambda b, i: (0, 0))],
        out_specs=pl.BlockSpec((None, tm, d), lambda b, i: (b, i, 0)),
        out_shape=jax.ShapeDtypeStruct((bsz, r, d), jnp.float32),
        name="outproj",
        compiler_params=_params("arbitrary", "arbitrary"),
    )(stream, *mixes, wk, mod)


def _permute_w_d(w_t):
    dep, _, d = w_t.shape
    half = MLA_ROPE // 2
    q = w_t[:, ABC_COLS:ABC_COLS + N_HEADS * MLA_QK, :].reshape(dep, N_HEADS, MLA_QK, d)
    nope = q[:, :, :MLA_NOPE, :].reshape(dep, N_HEADS * MLA_NOPE, d)
    rope = q[:, :, MLA_NOPE:, :].reshape(dep, 2, 2, 2, half, d).transpose(0, 1, 3, 2, 4, 5)
    rope = rope.reshape(dep, N_HEADS * MLA_ROPE, d)
    kr = w_t[:, D_KR_COL:D_KR_COL + MLA_ROPE, :].reshape(dep, 2, 1, half, d)
    kr = jnp.broadcast_to(kr, (dep, 2, 2, half, d)).reshape(dep, 2 * MLA_ROPE, d)
    pad = jnp.zeros((dep, LANE, d), w_t.dtype)
    return jnp.concatenate([nope, w_t[:, D_G_COL:D_G_COL + BRANCH_W, :], w_t[:, D_CKV_COL:D_CKV_COL + MLA_KV_RANK, :],
                            rope, kr, pad], axis=1)


def _w_in_kernel(w_ref, wd_ref, o_ref, *, n_abc):
    j = pl.program_id(1)

    @pl.when(j < n_abc)
    def _():
        o_ref[...] = w_ref[...].astype(jnp.bfloat16)

    @pl.when(j >= n_abc)
    def _():
        o_ref[...] = wd_ref[...].astype(jnp.bfloat16)


def _prep_w_in(w_in):
    depth, d, _ = w_in.shape
    w_t = jnp.swapaxes(w_in, 1, 2)
    wd = _permute_w_d(w_t)
    n_abc = ABC_COLS // TN_IN
    n_d = wd.shape[1] // TN_IN
    return pl.pallas_call(
        functools.partial(_w_in_kernel, n_abc=n_abc),
        grid=(depth, n_abc + n_d),
        in_specs=[pl.BlockSpec((None, TN_IN, d), lambda l, j: (l, jnp.minimum(j, n_abc - 1), 0)),
                  pl.BlockSpec((None, TN_IN, d), lambda l, j: (l, jnp.maximum(j - n_abc, 0), 0))],
        out_specs=pl.BlockSpec((None, TN_IN, d), lambda l, j: (l, j, 0)),
        out_shape=jax.ShapeDtypeStruct((depth, ABC_COLS + wd.shape[1], d), jnp.bfloat16),
        name="w_in_cast",
        compiler_params=_params("arbitrary", "arbitrary"),
    )(w_t, wd)


def _rope_tables(n_lat, rot_dim):
    tpos = jnp.arange(n_lat)
    row = (tpos // GRID_W).astype(jnp.float32)
    col = (tpos % GRID_W).astype(jnp.float32)
    n_freq = rot_dim // 4
    inv_freq = ROPE_THETA ** (-jnp.arange(n_freq, dtype=jnp.float32) / n_freq)
    ang = jnp.concatenate([row[:, None] * inv_freq, col[:, None] * inv_freq], axis=-1)
    cos, sin = jnp.cos(ang), jnp.sin(ang)
    rep = LANE // rot_dim
    return (jnp.concatenate([cos] * (2 * rep), axis=-1),
            jnp.concatenate([-sin] * rep + [sin] * rep, axis=-1))


def _dup_rope_gain(g):
    half = MLA_ROPE // 2
    r1, r2 = g[MLA_NOPE:MLA_NOPE + half], g[MLA_NOPE + half:]
    return g[None, :MLA_NOPE], jnp.concatenate([r1, r1, r2, r2])[None, :]


def kernel(x, c, ctx, c_ctx, norm_g, w_ada, b_ada, w_in, w_out, a_q_g, a_k_g, b_q_g, b_k_g, b_rpb,
           c_q_g, c_k_g, c_sink, d_q_g, d_k_g, d_kv_g, d_w_uk, d_w_uv):
    bsz, n_lat, d = x.shape
    ctx_len = ctx.shape[1]
    depth = w_in.shape[0]
    assert bsz <= 8 and n_lat % TM_IN == 0 and TM_IN % ctx_len == 0 and ctx_len % LANE == 0

    cc = jnp.zeros((16, d), jnp.float32).at[:bsz].set(c).at[8].set(c_ctx)
    mod = _ada(cc, w_ada, b_ada)
    w_in_p = _prep_w_in(w_in)
    w_out_p = w_out.astype(jnp.bfloat16)
    wuk, wuv = d_w_uk.astype(jnp.bfloat16), d_w_uv.astype(jnp.bfloat16)
    tabs_h = _rope_tables(n_lat, HEAD_DIM)
    tabs_r = _rope_tables(n_lat, MLA_ROPE)
    na_bias = _na_bias_tables(b_rpb, n_lat)

    hc = ctx.astype(x.dtype)
    for l in range(depth):
        with_ctx = l < depth - 1
        ng = norm_g[l][None, :]
        d_gains = _dup_rope_gain(d_q_g[l]) + _dup_rope_gain(d_k_g[l])
        kvg = d_kv_g[l].reshape(4, LANE)
        proj_x = _inproj(x, mod[l], ng, w_in_p, l, mod_row=None, tn=TN_IN_LATENT)
        if with_ctx:
            proj_c, cmap = _inproj(hc, mod[l], ng, w_in_p, l, mod_row=8, tn=TN_IN_LATENT), (lambda ch: ch)
        else:
            proj_c = _inproj(hc, mod[l], ng, w_in_p, l, mod_row=8, tile_stride=2)
            cmap = lambda ch: (ch // (2 * CPT)) * CPT + ch % CPT
        oa = _attn_gqa(proj_x, proj_c, cmap, tabs_h, a_q_g[l][None, :], a_k_g[l][None, :], None,
                       (A_Q, A_K, A_V, A_G))
        ob = _attn_b(proj_x, proj_c, cmap, na_bias, l, b_q_g[l][None, :], b_k_g[l][None, :])
        oc = _attn_gqa(proj_x, proj_c, cmap, tabs_h, c_q_g[l][None, :], c_k_g[l][None, :], c_sink[l],
                       (C_Q, C_K, C_V, C_G))
        od = _attn_d(proj_x, proj_c, cmap, tabs_r, d_gains + (kvg,), wuk, wuv, l)
        if with_ctx:
            o_ctx = _attn_ctx(proj_c, c_sink[l], jnp.stack([a_q_g[l], b_q_g[l], c_q_g[l]]),
                              jnp.stack([a_k_g[l], b_k_g[l], c_k_g[l]]), jnp.concatenate(d_gains, axis=0),
                              kvg, wuk, wuv, l)
            hc = _outproj(hc, (o_ctx,) * 4, w_out_p, l, mod[l], mod_row=8, tm=ctx_len)
        x = _outproj(x, (oa, ob, oc, od), w_out_p, l, mod[l], mod_row=None, tm=TM_OUT)
    return x
```

```python
import functools

import jax
import jax.numpy as jnp
from jax import lax
from jax.experimental import pallas as pl
from jax.experimental.pallas import tpu as pltpu

GRID_W = 64
HEAD_DIM = 128
BRANCH_W = 512
N_HEADS = 4
NA_KH = 8
NA_KW = 16
WINDOW = 128
MLA_KV_RANK = 512
MLA_NOPE = 128
MLA_ROPE = 64
MLA_QK = MLA_NOPE + MLA_ROPE
ROPE_THETA = 10000.0
EPS = 1e-6
NEG = -1e30
LOG2E = 1.4426950408889634

LANE = 128
TQ = 512
TM_IN = 1024
TN_IN = 1024
TN_IN_LATENT = 1792
TM_OUT = 512
CPT = TN_IN // LANE
N_CHUNKS = 56
ABC_COLS = 5120
D_CKV_COL = ABC_COLS + N_HEADS * MLA_QK
D_KR_COL = D_CKV_COL + MLA_KV_RANK
D_G_COL = D_KR_COL + MLA_ROPE
NA_STRIP = (NA_KH + TQ // GRID_W) * GRID_W
WIN_SPAN = TQ + 2 * WINDOW
VMEM_LIMIT = 48 * 1024 * 1024

A_Q, A_K, A_V, A_G = 0, 4, 6, 8
B_Q, B_K, B_V, B_G = 12, 16, 20, 24
C_Q, C_K, C_V, C_G = 28, 32, 34, 36
D_QN, D_G, D_CKV, D_QR, D_KR = 40, 44, 48, 52, 54

_NT = (((1,), (1,)), ((), ()))


def _params(*sem):
    return pltpu.CompilerParams(dimension_semantics=sem, vmem_limit_bytes=VMEM_LIMIT)


def _silu(x):
    return x * jax.nn.sigmoid(x)


def _ada_kernel(c_ref, w_ref, b_ref, o_ref):
    a = _silu(c_ref[...]).astype(jnp.bfloat16)
    o_ref[...] = jnp.dot(a, w_ref[...].astype(jnp.bfloat16),
                         preferred_element_type=jnp.float32) + b_ref[...]


def _ada(cc, w_ada, b_ada):
    depth, d, n = w_ada.shape
    tn = 1024
    return pl.pallas_call(
        _ada_kernel,
        grid=(depth, n // tn),
        in_specs=[pl.BlockSpec((16, d), lambda l, j: (0, 0)),
                  pl.BlockSpec((None, d, tn), lambda l, j: (l, 0, j)),
                  pl.BlockSpec((None, 1, tn), lambda l, j: (l, 0, j))],
        out_specs=pl.BlockSpec((None, 16, tn), lambda l, j: (l, 0, j)),
        out_shape=jax.ShapeDtypeStruct((depth, 16, n), jnp.float32),
        name="ada",
        compiler_params=_params("arbitrary", "arbitrary"),
    )(cc, w_ada, b_ada.reshape(depth, 1, n))


def _inproj_kernel(x_ref, mod_ref, ng_ref, w_ref, o_ref, h_ref, *, mod_row):
    j = pl.program_id(2)
    nb, r, d = x_ref.shape
    mrow = pl.program_id(0) if mod_row is None else mod_row

    @pl.when(j == 0)
    def _():
        sh = mod_ref[pl.ds(mrow, 1), 0:d]
        gain = ng_ref[...] * (1.0 + mod_ref[pl.ds(mrow, 1), d:2 * d])
        for s in range(nb):
            x = x_ref[s]
            inv = lax.rsqrt(jnp.mean(x * x, axis=-1, keepdims=True) + EPS)
            h_ref[s * r:(s + 1) * r, :] = (x * inv * gain + sh).astype(jnp.bfloat16)

    acc = lax.dot_general(h_ref[...], w_ref[...], _NT, preferred_element_type=jnp.float32)
    for s in range(nb):
        for c in range(acc.shape[1] // LANE):
            o_ref[s, c] = acc[s * r:(s + 1) * r, c * LANE:(c + 1) * LANE].astype(jnp.bfloat16)


def _inproj(stream, mod, ng, w, layer, *, mod_row, tn=TN_IN, tile_stride=1):
    bsz, r, d = stream.shape
    nb = min(bsz, max(1, TM_IN // r))
    rows = min(r, TM_IN)
    assert bsz % nb == 0 and r % rows == 0
    n_tiles = -(-(w.shape[1] // tn) // tile_stride)
    cpt = tn // LANE
    return pl.pallas_call(
        functools.partial(_inproj_kernel, mod_row=mod_row),
        grid=(bsz // nb, r // rows, n_tiles),
        in_specs=[pl.BlockSpec((nb, rows, d), lambda b, i, j: (b, i, 0)),
                  pl.BlockSpec(mod.shape, lambda b, i, j: (0, 0)),
                  pl.BlockSpec((1, d), lambda b, i, j: (0, 0)),
                  pl.BlockSpec((None, tn, d), lambda b, i, j: (layer, j * tile_stride, 0))],
        out_specs=pl.BlockSpec((nb, cpt, rows, LANE), lambda b, i, j: (b, j, i, 0)),
        out_shape=jax.ShapeDtypeStruct((bsz, n_tiles * cpt, r, LANE), jnp.bfloat16),
        scratch_shapes=[pltpu.VMEM((nb * rows, d), jnp.bfloat16)],
        name="inproj",
        compiler_params=_params("arbitrary", "arbitrary", "arbitrary"),
    )(stream, mod, ng, w)


def _norm_rope(x, gain, cos=None, sin=None, scale=None, mxu_sum=False):
    if mxu_sum:
        ones = jnp.ones((x.shape[1], x.shape[1]), jnp.bfloat16)
        ms = jnp.dot((x * x).astype(jnp.bfloat16), ones, preferred_element_type=jnp.float32) / x.shape[1]
    else:
        ms = jnp.mean(x * x, axis=-1, keepdims=True)
    y = x * lax.rsqrt(ms + EPS) * gain
    if cos is not None:
        y = y * cos + pltpu.roll(y, 64, 1) * sin
    if scale is not None:
        y = y * scale
    return y


def _chunk_spec(n, rows, chunk0, row_fn):
    return pl.BlockSpec((None, n, rows, LANE), lambda b, i: (b, chunk0 // n, row_fn(i), 0))


def _gated_store(o_ref, col, o, g):
    g = g.astype(jnp.float32)
    o_ref[:, col:col + LANE] = (o * _silu(g)).astype(o_ref.dtype)


def _with_ones(v):
    lane = lax.broadcasted_iota(jnp.int32, v.shape, 1)
    return jnp.concatenate([v, jnp.where(lane == 0, 1.0, 0.0).astype(v.dtype)], axis=-1)


def _softmax_pv(qs, pieces, lower=None):
    scores = []
    for q, head_pieces in zip(qs, pieces):
        row = []
        for k, _, bias in head_pieces:
            s = lax.dot_general(q, k, _NT, preferred_element_type=jnp.float32)
            if bias is not None:
                rep, (rows, n) = s.shape[0] // bias.shape[0], bias.shape
                s = (s.reshape(rep, rows, n) + bias[None]).reshape(rep * rows, n)
            row.append(s.astype(jnp.bfloat16))
        scores.append(row)
    maxes = []
    for h, row in enumerate(scores):
        m = functools.reduce(jnp.maximum, [jnp.max(s, axis=-1, keepdims=True) for s in row])
        if lower is not None:
            m = jnp.maximum(m, lower[h].astype(jnp.bfloat16))
        maxes.append(m)
    probs = [[jnp.exp2(s - m) for s in row] for row, m in zip(scores, maxes)]
    accs = []
    for row, head_pieces in zip(probs, pieces):
        acc = None
        for p, (_, vp, _) in zip(row, head_pieces):
            part = jnp.dot(p, vp, preferred_element_type=jnp.float32)
            acc = part if acc is None else acc + part
        accs.append(acc)
    return accs, [m.astype(jnp.float32) for m in maxes]


def _normalise(acc, extra=None):
    l = acc[:, LANE:LANE + 1]
    if extra is not None:
        l = l + extra
    return acc[:, 0:LANE] / l


def _key_spans(t):
    cut = (t // 2 + 255) // 256 * 256
    return [(0, cut), (cut, t)]


def _prepare_queries(src_ref, qp_ref, slot, blk, qg_ref, cos_ref=None, sin_ref=None):
    tq = src_ref.shape[1]
    cos = sin = None
    if cos_ref is not None:
        r0 = pl.multiple_of(blk * tq, tq)
        cos, sin = cos_ref[pl.ds(r0, tq), :], sin_ref[pl.ds(r0, tq), :]
    for h in range(N_HEADS):
        qp_ref[slot, h] = _norm_rope(src_ref[h].astype(jnp.float32), qg_ref[...], cos, sin,
                                     HEAD_DIM ** -0.5 * LOG2E).astype(jnp.bfloat16)


def _next_block(i):
    return jnp.minimum(i + 1, pl.num_programs(1) - 1)


def _load_stacked_queries(qp_ref, slot):
    return [jnp.concatenate([qp_ref[slot, 2 * kv], qp_ref[slot, 2 * kv + 1]], axis=0) for kv in range(2)]


def _mla_queries(qn, qr, h, qgn, qgr, cos, sin):
    lane_grp = (lax.broadcasted_iota(jnp.int32, (1, LANE), 1) // (MLA_ROPE // 2)) % 2
    qn = qn.astype(jnp.float32)
    qt = jnp.where(lane_grp == h % 2, qr.astype(jnp.float32), 0.0)
    ms = jnp.sum(qn * qn + qt * qt, axis=-1, keepdims=True) / MLA_QK
    inv = lax.rsqrt(ms + EPS)
    qt = qt * inv * qgr
    if cos is not None:
        qt = qt * cos + pltpu.roll(qt, 64, 1) * sin
    return (jnp.concatenate([qn * inv * qgn, qt], axis=-1) * (MLA_QK ** -0.5 * LOG2E)).astype(jnp.bfloat16)


def _mla_keys_values(ckv, kr, kvg, kgn, kgr, wuk, wuv, cos, sin):
    c = [cj.astype(jnp.float32) for cj in ckv]
    ms = jnp.sum(sum(cj * cj for cj in c), axis=-1, keepdims=True) / MLA_KV_RANK
    inv = lax.rsqrt(ms + EPS)
    cn = jnp.concatenate([c[j] * inv * kvg[j:j + 1, :] for j in range(4)], axis=-1).astype(jnp.bfloat16)
    kn = jnp.dot(cn, wuk, preferred_element_type=jnp.float32)
    vv = jnp.dot(cn, wuv, preferred_element_type=jnp.float32)
    kr = kr.astype(jnp.float32)
    kr_sq = 0.5 * (kr * kr)
    kt = kr * kgr
    if cos is not None:
        kt = kt * cos + pltpu.roll(kt, 64, 1) * sin
    keys, vals = [], []
    for h in range(N_HEADS):
        kh = kn[:, h * LANE:(h + 1) * LANE]
        inv_h = lax.rsqrt(jnp.sum(kh * kh + kr_sq, axis=-1, keepdims=True) / MLA_QK + EPS)
        keys.append(jnp.concatenate([kh * inv_h * kgn, kt * inv_h], axis=-1).astype(jnp.bfloat16))
        vals.append(vv[:, h * LANE:(h + 1) * LANE].astype(jnp.bfloat16))
    return keys, vals


def _gqa_prep(kl_ref, kc_ref, vl_ref, vc_ref, ck_ref, sk_ref, kg_ref, kp_ref, vp_ref, ctx_len):
    for kv in range(2):
        kp_ref[kv, 0:ctx_len, :] = _norm_rope(kc_ref[kv].astype(jnp.float32), kg_ref[...]).astype(jnp.bfloat16)
        kp_ref[kv, ctx_len:, :] = _norm_rope(kl_ref[kv].astype(jnp.float32), kg_ref[...],
                                             ck_ref[...], sk_ref[...], mxu_sum=True).astype(jnp.bfloat16)
        vp_ref[kv, 0:ctx_len, :] = _with_ones(vc_ref[kv])
        vp_ref[kv, ctx_len:, :] = _with_ones(vl_ref[kv])


def _attn_a_kernel(q_ref, qn_ref, kl_ref, kc_ref, vl_ref, vc_ref, g_ref, ck_ref, sk_ref, qg_ref, kg_ref,
                   o_ref, kp_ref, vp_ref, qp_ref, *, ctx_len):
    i = pl.program_id(1)
    tq = q_ref.shape[1]

    @pl.when(i == 0)
    def _():
        _gqa_prep(kl_ref, kc_ref, vl_ref, vc_ref, ck_ref, sk_ref, kg_ref, kp_ref, vp_ref, ctx_len)
        _prepare_queries(q_ref, qp_ref, 0, 0, qg_ref, ck_ref, sk_ref)

    qs = _load_stacked_queries(qp_ref, i % 2)
    _prepare_queries(qn_ref, qp_ref, 1 - i % 2, _next_block(i), qg_ref, ck_ref, sk_ref)
    accs, _ = _softmax_pv(qs, [[(kp_ref[kv, lo:hi, :], vp_ref[kv, lo:hi, :], None)
                                for lo, hi in _key_spans(kp_ref.shape[1])] for kv in range(2)])
    for kv in range(2):
        o = _normalise(accs[kv])
        for j in range(2):
            _gated_store(o_ref, (2 * kv + j) * LANE, o[j * tq:(j + 1) * tq], g_ref[2 * kv + j])


def _attn_c_kernel(sink_ref, q_ref, qn_ref, kl_ref, kc_ref, vl_ref, vc_ref, g_ref, ck_ref, sk_ref,
                   qg_ref, kg_ref, o_ref, kp_ref, vp_ref, qp_ref, *, ctx_len):
    i = pl.program_id(1)
    tq = q_ref.shape[1]
    n_lat = kl_ref.shape[1]

    @pl.when(i == 0)
    def _():
        _gqa_prep(kl_ref, kc_ref, vl_ref, vc_ref, ck_ref, sk_ref, kg_ref, kp_ref, vp_ref, ctx_len)
        _prepare_queries(q_ref, qp_ref, 0, 0, qg_ref, ck_ref, sk_ref)

    qs = _load_stacked_queries(qp_ref, i % 2)
    _prepare_queries(qn_ref, qp_ref, 1 - i % 2, _next_block(i), qg_ref, ck_ref, sk_ref)

    q0 = i * tq
    ks = jnp.clip(q0 - WINDOW, 0, n_lat - WIN_SPAN)
    row0 = pl.multiple_of(ctx_len + ks, LANE)
    qi = lax.broadcasted_iota(jnp.int32, (tq, WIN_SPAN), 0)
    ki = lax.broadcasted_iota(jnp.int32, (tq, WIN_SPAN), 1)
    wmask = jnp.where(jnp.abs((qi - ki) + (q0 - ks)) <= WINDOW, 0.0, NEG)
    head_row = lax.broadcasted_iota(jnp.int32, (2 * tq, 1), 0) < tq
    sinks = [jnp.where(head_row, sink_ref[2 * kv], sink_ref[2 * kv + 1]) * LOG2E for kv in range(2)]
    pieces = [[(kp_ref[kv, 0:ctx_len, :], vp_ref[kv, 0:ctx_len, :], None),
               (kp_ref[kv, pl.ds(row0, WIN_SPAN), :], vp_ref[kv, pl.ds(row0, WIN_SPAN), :], wmask)]
              for kv in range(2)]
    accs, ms = _softmax_pv(qs, pieces, lower=sinks)
    for kv in range(2):
        o = _normalise(accs[kv], jnp.exp2(sinks[kv] - ms[kv]))
        for j in range(2):
            _gated_store(o_ref, (2 * kv + j) * LANE, o[j * tq:(j + 1) * tq], g_ref[2 * kv + j])


def _attn_gqa(proj_x, proj_c, cmap, tabs, qg, kg, sink, chunks):
    cq0, ck0, cv0, cg0 = chunks
    bsz, _, n_lat, _ = proj_x.shape
    ctx_len = proj_c.shape[2]
    t = ctx_len + n_lat
    cos, sin = tabs
    nq = n_lat // TQ
    row = lambda i: i
    nxt = lambda i: jnp.minimum(i + 1, nq - 1)
    zero = lambda i: 0
    in_specs = [_chunk_spec(4, TQ, cq0, row), _chunk_spec(4, TQ, cq0, nxt),
                _chunk_spec(2, n_lat, ck0, zero), _chunk_spec(2, ctx_len, cmap(ck0), zero),
                _chunk_spec(2, n_lat, cv0, zero), _chunk_spec(2, ctx_len, cmap(cv0), zero),
                _chunk_spec(4, TQ, cg0, row),
                pl.BlockSpec((n_lat, LANE), lambda b, i: (0, 0)),
                pl.BlockSpec((n_lat, LANE), lambda b, i: (0, 0)),
                pl.BlockSpec((1, LANE), lambda b, i: (0, 0)),
                pl.BlockSpec((1, LANE), lambda b, i: (0, 0))]
    args = [proj_x, proj_x, proj_x, proj_c, proj_x, proj_c, proj_x, cos, sin, qg, kg]
    if sink is None:
        body = _attn_a_kernel
    else:
        body = _attn_c_kernel
        in_specs = [pl.BlockSpec(memory_space=pltpu.SMEM)] + in_specs
        args = [sink] + args
    return pl.pallas_call(
        functools.partial(body, ctx_len=ctx_len),
        grid=(bsz, n_lat // TQ),
        in_specs=in_specs,
        out_specs=pl.BlockSpec((None, TQ, BRANCH_W), lambda b, i: (b, i, 0)),
        out_shape=jax.ShapeDtypeStruct((bsz, n_lat, BRANCH_W), jnp.bfloat16),
        scratch_shapes=[pltpu.VMEM((2, t, LANE), jnp.bfloat16), pltpu.VMEM((2, t, 2 * LANE), jnp.bfloat16),
                        pltpu.VMEM((2, N_HEADS, TQ, LANE), jnp.bfloat16)],
        name="attn_a" if sink is None else "attn_c",
        compiler_params=_params("arbitrary", "arbitrary"),
    )(*args)


def _attn_b_kernel(q_ref, qn_ref, kl_ref, kc_ref, vl_ref, vc_ref, g_ref, bias_ref, qg_ref, kg_ref,
                   o_ref, kp_ref, vp_ref, qp_ref, *, ctx_len):
    i = pl.program_id(1)
    tq = q_ref.shape[1]
    rows = kl_ref.shape[1] // GRID_W
    strip_rows = NA_STRIP // GRID_W

    @pl.when(i == 0)
    def _():
        for h in range(N_HEADS):
            kp_ref[h, 0:ctx_len, :] = _norm_rope(kc_ref[h].astype(jnp.float32), kg_ref[...]).astype(jnp.bfloat16)
            kp_ref[h, ctx_len:, :] = _norm_rope(kl_ref[h].astype(jnp.float32), kg_ref[...]).astype(jnp.bfloat16)
            vp_ref[h, 0:ctx_len, :] = _with_ones(vc_ref[h])
            vp_ref[h, ctx_len:, :] = _with_ones(vl_ref[h])
        _prepare_queries(q_ref, qp_ref, 0, 0, qg_ref)

    qs = [qp_ref[i % 2, h] for h in range(N_HEADS)]
    _prepare_queries(qn_ref, qp_ref, 1 - i % 2, _next_block(i), qg_ref)
    r0 = i * (tq // GRID_W)
    ss = jnp.clip(r0 - NA_KH // 2, 0, rows - strip_rows)
    row0 = pl.multiple_of(ctx_len + ss * GRID_W, LANE)
    pieces = [[(kp_ref[h, 0:ctx_len, :], vp_ref[h, 0:ctx_len, :], None),
               (kp_ref[h, pl.ds(row0, NA_STRIP), :], vp_ref[h, pl.ds(row0, NA_STRIP), :], bias_ref[h])]
              for h in range(N_HEADS)]
    accs, _ = _softmax_pv(qs, pieces)
    for h in range(N_HEADS):
        _gated_store(o_ref, h * LANE, _normalise(accs[h]), g_ref[h])


def _na_bias_kernel(rp_ref, o_ref, *, rows):
    q_rows, strip_rows = TQ // GRID_W, NA_STRIP // GRID_W
    qc = lax.broadcasted_iota(jnp.int32, (GRID_W, LANE), 0)
    lane = lax.broadcasted_iota(jnp.int32, (GRID_W, LANE), 1)
    kc = lane & (GRID_W - 1)
    cs = jnp.clip(qc - NA_KW // 2, 0, GRID_W - NA_KW)
    col_ok = (kc >= cs) & (kc < cs + NA_KW)
    second = lane >= GRID_W
    for var, r0 in enumerate((0, q_rows, rows - q_rows)):
        ss = min(max(r0 - NA_KH // 2, 0), rows - strip_rows)
        for j in range(q_rows):
            qr = r0 + j
            rs = min(max(qr - NA_KH // 2, 0), rows - NA_KH)
            for p in range(strip_rows // 2):
                kr0 = ss + 2 * p
                ok0, ok1 = rs <= kr0 < rs + NA_KH, rs <= kr0 + 1 < rs + NA_KH
                if ok0 or ok1:
                    e = kr0 - qr + NA_KH - 1
                    x = jnp.broadcast_to(rp_ref[e + 1:e + 2, :], (GRID_W, LANE))
                    band = pltpu.roll(x, LANE - (NA_KW - 1), 1, stride=1, stride_axis=0)
                    row_ok = second if (ok1 and not ok0) else (~second if (ok0 and not ok1) else None)
                    valid = col_ok if row_ok is None else (col_ok & row_ok)
                    tile = jnp.where(valid, band * LOG2E, NEG)
                else:
                    tile = jnp.full((GRID_W, LANE), NEG, jnp.float32)
                o_ref[var, j * GRID_W:(j + 1) * GRID_W, p * LANE:(p + 1) * LANE] = tile


def _na_bias_tables(rpb, n_lat):
    depth, nh, nr, nc = rpb.shape
    z = jnp.zeros((depth, nh, nr + 2, GRID_W), jnp.float32).at[:, :, 1:nr + 1, :nc].set(rpb)
    rp = jnp.concatenate([z[:, :, :-1], z[:, :, 1:]], axis=-1)
    return pl.pallas_call(
        functools.partial(_na_bias_kernel, rows=n_lat // GRID_W),
        grid=(depth, nh),
        in_specs=[pl.BlockSpec((None, None, nr + 1, LANE), lambda l, h: (l, h, 0, 0))],
        out_specs=pl.BlockSpec((None, 3, None, TQ, NA_STRIP), lambda l, h: (l, 0, h, 0, 0)),
        out_shape=jax.ShapeDtypeStruct((depth, 3, nh, TQ, NA_STRIP), jnp.float32),
        name="na_bias",
        compiler_params=_params("arbitrary", "arbitrary"),
    )(rp)


def _attn_b(proj_x, proj_c, cmap, bias, layer, qg, kg):
    bsz, _, n_lat, _ = proj_x.shape
    ctx_len = proj_c.shape[2]
    t = ctx_len + n_lat
    nq = n_lat // TQ
    row = lambda i: i
    nxt = lambda i: jnp.minimum(i + 1, nq - 1)
    zero = lambda i: 0

    def variant(i):
        return jnp.where(i == 0, 0, jnp.where(i == nq - 1, 2, 1))

    return pl.pallas_call(
        functools.partial(_attn_b_kernel, ctx_len=ctx_len),
        grid=(bsz, nq),
        in_specs=[_chunk_spec(4, TQ, B_Q, row), _chunk_spec(4, TQ, B_Q, nxt),
                  _chunk_spec(4, n_lat, B_K, zero), _chunk_spec(4, ctx_len, cmap(B_K), zero),
                  _chunk_spec(4, n_lat, B_V, zero), _chunk_spec(4, ctx_len, cmap(B_V), zero),
                  _chunk_spec(4, TQ, B_G, row),
                  pl.BlockSpec((None, None, N_HEADS, TQ, NA_STRIP), lambda b, i: (layer, variant(i), 0, 0, 0)),
                  pl.BlockSpec((1, LANE), lambda b, i: (0, 0)),
                  pl.BlockSpec((1, LANE), lambda b, i: (0, 0))],
        out_specs=pl.BlockSpec((None, TQ, BRANCH_W), lambda b, i: (b, i, 0)),
        out_shape=jax.ShapeDtypeStruct((bsz, n_lat, BRANCH_W), jnp.bfloat16),
        scratch_shapes=[pltpu.VMEM((N_HEADS, t, LANE), jnp.bfloat16),
                        pltpu.VMEM((N_HEADS, t, 2 * LANE), jnp.bfloat16),
                        pltpu.VMEM((2, N_HEADS, TQ, LANE), jnp.bfloat16)],
        name="attn_b",
        compiler_params=_params("arbitrary", "arbitrary"),
    )(proj_x, proj_x, proj_x, proj_c, proj_x, proj_c, proj_x, bias, qg, kg)


def _attn_d_kernel(qn_ref, qr_ref, qn2_ref, qr2_ref, ckvl_ref, ckvc_ref, krl_ref, krc_ref, g_ref, ck_ref, sk_ref,
                   qgn_ref, qgr_ref, kgn_ref, kgr_ref, kvg_ref, wuk_ref, wuv_ref,
                   o_ref, kp_ref, vp_ref, qp_ref, *, ctx_len):
    i = pl.program_id(1)
    n_lat = ckvl_ref.shape[1]
    tq = qn_ref.shape[1]

    def prepare_queries(nope_ref, rope_ref, slot, blk):
        r0 = pl.multiple_of(blk * tq, tq)
        cos, sin = ck_ref[pl.ds(r0, tq), :], sk_ref[pl.ds(r0, tq), :]
        for h in range(N_HEADS):
            qp_ref[slot, h] = _mla_queries(nope_ref[h], rope_ref[h // 2], h, qgn_ref[...], qgr_ref[...], cos, sin)

    @pl.when(i == 0)
    def _():
        prepare_queries(qn_ref, qr_ref, 0, 0)

        def fill(dst, ckv_ref, kr_ref, src, n, cos, sin):
            keys, vals = _mla_keys_values([ckv_ref[j, src:src + n, :] for j in range(4)], kr_ref[0, src:src + n, :],
                                          kvg_ref[...], kgn_ref[...], kgr_ref[...], wuk_ref[...], wuv_ref[...],
                                          cos, sin)
            for h in range(N_HEADS):
                kp_ref[h, dst:dst + n, :] = keys[h]
                vp_ref[h, dst:dst + n, :] = _with_ones(vals[h])

        fill(0, ckvc_ref, krc_ref, 0, ctx_len, None, None)
        rc = 1024
        for r in range(0, n_lat, rc):
            fill(ctx_len + r, ckvl_ref, krl_ref, r, rc, ck_ref[r:r + rc, :], sk_ref[r:r + rc, :])

    qs = [qp_ref[i % 2, h] for h in range(N_HEADS)]
    prepare_queries(qn2_ref, qr2_ref, 1 - i % 2, _next_block(i))
    accs, _ = _softmax_pv(qs, [[(kp_ref[h, lo:hi, :], vp_ref[h, lo:hi, :], None)
                                for lo, hi in _key_spans(kp_ref.shape[1])] for h in range(N_HEADS)])
    for h in range(N_HEADS):
        _gated_store(o_ref, h * LANE, _normalise(accs[h]), g_ref[h])


def _attn_d(proj_x, proj_c, cmap, tabs, gains, wuk, wuv, layer):
    bsz, _, n_lat, _ = proj_x.shape
    ctx_len = proj_c.shape[2]
    t = ctx_len + n_lat
    nq = n_lat // TQ
    row = lambda i: i
    nxt = lambda i: jnp.minimum(i + 1, nq - 1)
    zero = lambda i: 0
    cos, sin = tabs
    qgn, qgr, kgn, kgr, kvg = gains
    vec = pl.BlockSpec((1, LANE), lambda b, i: (0, 0))
    wspec = pl.BlockSpec((None,) + wuk.shape[1:], lambda b, i: (layer, 0, 0))
    return pl.pallas_call(
        functools.partial(_attn_d_kernel, ctx_len=ctx_len),
        grid=(bsz, n_lat // TQ),
        in_specs=[_chunk_spec(4, TQ, D_QN, row), _chunk_spec(2, TQ, D_QR, row),
                  _chunk_spec(4, TQ, D_QN, nxt), _chunk_spec(2, TQ, D_QR, nxt),
                  _chunk_spec(4, n_lat, D_CKV, zero), _chunk_spec(4, ctx_len, cmap(D_CKV), zero),
                  _chunk_spec(1, n_lat, D_KR, zero), _chunk_spec(1, ctx_len, cmap(D_KR), zero),
                  _chunk_spec(4, TQ, D_G, row),
                  pl.BlockSpec((n_lat, LANE), lambda b, i: (0, 0)),
                  pl.BlockSpec((n_lat, LANE), lambda b, i: (0, 0)),
                  vec, vec, vec, vec,
                  pl.BlockSpec((4, LANE), lambda b, i: (0, 0)),
                  wspec, wspec],
        out_specs=pl.BlockSpec((None, TQ, BRANCH_W), lambda b, i: (b, i, 0)),
        out_shape=jax.ShapeDtypeStruct((bsz, n_lat, BRANCH_W), jnp.bfloat16),
        scratch_shapes=[pltpu.VMEM((N_HEADS, t, 2 * LANE), jnp.bfloat16),
                        pltpu.VMEM((N_HEADS, t, 2 * LANE), jnp.bfloat16),
                        pltpu.VMEM((2, N_HEADS, TQ, 2 * LANE), jnp.bfloat16)],
        name="attn_d",
        compiler_params=_params("arbitrary", "arbitrary"),
    )(proj_x, proj_x, proj_x, proj_x, proj_x, proj_c, proj_x, proj_c, proj_x, cos, sin,
      qgn, qgr, kgn, kgr, kvg, wuk, wuv)


def _attn_ctx_kernel(sink_ref, pc_ref, gq_ref, gk_ref, dg_ref, kvg_ref, wuk_ref, wuv_ref, o_ref):
    tq = pc_ref.shape[1]
    scale = HEAD_DIM ** -0.5 * LOG2E

    def prepared(chunk, gains, row, q_scale=None):
        return _norm_rope(pc_ref[chunk].astype(jnp.float32), gains[row:row + 1, :], scale=q_scale).astype(jnp.bfloat16)

    head_row = lax.broadcasted_iota(jnp.int32, (2 * tq, 1), 0) < tq
    for br, (cq0, ck0, cv0, cg0) in ((0, (A_Q, A_K, A_V, A_G)), (2, (C_Q, C_K, C_V, C_G))):
        qs = [jnp.concatenate([prepared(cq0 + 2 * kv + j, gq_ref, br, scale) for j in range(2)], axis=0)
              for kv in range(2)]
        pieces = [[(prepared(ck0 + kv, gk_ref, br), _with_ones(pc_ref[cv0 + kv]), None)] for kv in range(2)]
        if br == 0:
            accs, _ = _softmax_pv(qs, pieces)
            outs = [_normalise(acc) for acc in accs]
        else:
            sinks = [jnp.where(head_row, sink_ref[2 * kv], sink_ref[2 * kv + 1]) * LOG2E for kv in range(2)]
            accs, ms = _softmax_pv(qs, pieces, lower=sinks)
            outs = [_normalise(acc, jnp.exp2(sk - m)) for acc, sk, m in zip(accs, sinks, ms)]
        for kv in range(2):
            for j in range(2):
                h = 2 * kv + j
                _gated_store(o_ref, br * BRANCH_W + h * LANE, outs[kv][j * tq:(j + 1) * tq], pc_ref[cg0 + h])

    accs, _ = _softmax_pv([prepared(B_Q + h, gq_ref, 1, scale) for h in range(N_HEADS)],
                          [[(prepared(B_K + h, gk_ref, 1), _with_ones(pc_ref[B_V + h]), None)]
                           for h in range(N_HEADS)])
    for h in range(N_HEADS):
        _gated_store(o_ref, BRANCH_W + h * LANE, _normalise(accs[h]), pc_ref[B_G + h])

    keys, vals = _mla_keys_values([pc_ref[D_CKV + j] for j in range(4)], pc_ref[D_KR], kvg_ref[...],
                                  dg_ref[2:3, :], dg_ref[3:4, :], wuk_ref[...], wuv_ref[...], None, None)
    qs = [_mla_queries(pc_ref[D_QN + h], pc_ref[D_QR + h // 2], h, dg_ref[0:1, :], dg_ref[1:2, :], None, None)
          for h in range(N_HEADS)]
    accs, _ = _softmax_pv(qs, [[(keys[h], _with_ones(vals[h]), None)] for h in range(N_HEADS)])
    for h in range(N_HEADS):
        _gated_store(o_ref, 3 * BRANCH_W + h * LANE, _normalise(accs[h]), pc_ref[D_G + h])


def _attn_ctx(proj_c, sink, gq, gk, dg, kvg, wuk, wuv, layer):
    bsz, nc, ctx_len, _ = proj_c.shape
    full = lambda a: pl.BlockSpec(a.shape, lambda b: (0,) * a.ndim)
    wspec = pl.BlockSpec((None,) + wuk.shape[1:], lambda b: (layer, 0, 0))
    return pl.pallas_call(
        _attn_ctx_kernel,
        grid=(bsz,),
        in_specs=[pl.BlockSpec(memory_space=pltpu.SMEM),
                  pl.BlockSpec((None, nc, ctx_len, LANE), lambda b: (b, 0, 0, 0)),
                  full(gq), full(gk), full(dg), full(kvg), wspec, wspec],
        out_specs=pl.BlockSpec((None, ctx_len, 4 * BRANCH_W), lambda b: (b, 0, 0)),
        out_shape=jax.ShapeDtypeStruct((bsz, ctx_len, 4 * BRANCH_W), jnp.bfloat16),
        name="attn_ctx",
        compiler_params=_params("arbitrary"),
    )(sink, proj_c, gq, gk, dg, kvg, wuk, wuv)


def _outproj_kernel(*refs, n_in, mod_row):
    s_ref, mix_refs, (w_ref, mod_ref, o_ref) = refs[0], refs[1:1 + n_in], refs[1 + n_in:]
    d = s_ref.shape[1]
    acc = None
    for k in range(n_in):
        part = jnp.dot(mix_refs[k][...], w_ref[k], preferred_element_type=jnp.float32)
        acc = part if acc is None else acc + part
    mrow = pl.program_id(0) if mod_row is None else mod_row
    gate = mod_ref[pl.ds(mrow, 1), 2 * d:3 * d]
    o_ref[...] = s_ref[...] + gate * acc


def _outproj(stream, mixes, w, layer, mod, *, mod_row, tm):
    bsz, r, d = stream.shape
    n_in = len(mixes)
    width = w.shape[1] // n_in
    wk = w.reshape(w.shape[0], n_in, width, d)
    mix_specs = [pl.BlockSpec((None, tm, width), lambda b, i, col=(k if m.shape[2] > width else 0): (b, i, col))
                 for k, m in enumerate(mixes)]
    return pl.pallas_call(
        functools.partial(_outproj_kernel, n_in=n_in, mod_row=mod_row),
        grid=(bsz, r // tm),
        in_specs=[pl.BlockSpec((None, tm, d), lambda b, i: (b, i, 0))] + mix_specs + [
            pl.BlockSpec((None,) + wk.shape[1:], lambda b, i: (layer, 0, 0, 0)),
            pl.BlockSpec(mod.shape, lambda b, i: (0, 0))],
        out_specs=pl.BlockSpec((None, tm, d), lambda b, i: (b, i, 0)),
        out_shape=jax.ShapeDtypeStruct((bsz, r, d), jnp.float32),
        name="outproj",
        compiler_params=_params("arbitrary", "arbitrary"),
    )(stream, *mixes, wk, mod)


def _permute_w_d(w_t):
    dep, _, d = w_t.shape
    half = MLA_ROPE // 2
    q = w_t[:, ABC_COLS:ABC_COLS + N_HEADS * MLA_QK, :].reshape(dep, N_HEADS, MLA_QK, d)
    nope = q[:, :, :MLA_NOPE, :].reshape(dep, N_HEADS * MLA_NOPE, d)
    rope = q[:, :, MLA_NOPE:, :].reshape(dep, 2, 2, 2, half, d).transpose(0, 1, 3, 2, 4, 5)
    rope = rope.reshape(dep, N_HEADS * MLA_ROPE, d)
    kr = w_t[:, D_KR_COL:D_KR_COL + MLA_ROPE, :].reshape(dep, 2, 1, half, d)
    kr = jnp.broadcast_to(kr, (dep, 2, 2, half, d)).reshape(dep, 2 * MLA_ROPE, d)
    pad = jnp.zeros((dep, LANE, d), w_t.dtype)
    return jnp.concatenate([nope, w_t[:, D_G_COL:D_G_COL + BRANCH_W, :], w_t[:, D_CKV_COL:D_CKV_COL + MLA_KV_RANK, :],
                            rope, kr, pad], axis=1)


def _w_in_kernel(w_ref, wd_ref, o_ref, *, n_abc):
    j = pl.program_id(1)

    @pl.when(j < n_abc)
    def _():
        o_ref[...] = w_ref[...].astype(jnp.bfloat16)

    @pl.when(j >= n_abc)
    def _():
        o_ref[...] = wd_ref[...].astype(jnp.bfloat16)


def _prep_w_in(w_in):
    depth, d, _ = w_in.shape
    w_t = jnp.swapaxes(w_in, 1, 2)
    wd = _permute_w_d(w_t)
    n_abc = ABC_COLS // TN_IN
    n_d = wd.shape[1] // TN_IN
    return pl.pallas_call(
        functools.partial(_w_in_kernel, n_abc=n_abc),
        grid=(depth, n_abc + n_d),
        in_specs=[pl.BlockSpec((None, TN_IN, d), lambda l, j: (l, jnp.minimum(j, n_abc - 1), 0)),
                  pl.BlockSpec((None, TN_IN, d), lambda l, j: (l, jnp.maximum(j - n_abc, 0), 0))],
        out_specs=pl.BlockSpec((None, TN_IN, d), lambda l, j: (l, j, 0)),
        out_shape=jax.ShapeDtypeStruct((depth, ABC_COLS + wd.shape[1], d), jnp.bfloat16),
        name="w_in_cast",
        compiler_params=_params("arbitrary", "arbitrary"),
    )(w_t, wd)


def _rope_tables(n_lat, rot_dim):
    tpos = jnp.arange(n_lat)
    row = (tpos // GRID_W).astype(jnp.float32)
    col = (tpos % GRID_W).astype(jnp.float32)
    n_freq = rot_dim // 4
    inv_freq = ROPE_THETA ** (-jnp.arange(n_freq, dtype=jnp.float32) / n_freq)
    ang = jnp.concatenate([row[:, None] * inv_freq, col[:, None] * inv_freq], axis=-1)
    cos, sin = jnp.cos(ang), jnp.sin(ang)
    rep = LANE // rot_dim
    return (jnp.concatenate([cos] * (2 * rep), axis=-1),
            jnp.concatenate([-sin] * rep + [sin] * rep, axis=-1))


def _dup_rope_gain(g):
    half = MLA_ROPE // 2
    r1, r2 = g[MLA_NOPE:MLA_NOPE + half], g[MLA_NOPE + half:]
    return g[None, :MLA_NOPE], jnp.concatenate([r1, r1, r2, r2])[None, :]


def kernel(x, c, ctx, c_ctx, norm_g, w_ada, b_ada, w_in, w_out, a_q_g, a_k_g, b_q_g, b_k_g, b_rpb,
           c_q_g, c_k_g, c_sink, d_q_g, d_k_g, d_kv_g, d_w_uk, d_w_uv):
    bsz, n_lat, d = x.shape
    ctx_len = ctx.shape[1]
    depth = w_in.shape[0]
    assert bsz <= 8 and n_lat % TM_IN == 0 and TM_IN % ctx_len == 0 and ctx_len % LANE == 0

    cc = jnp.zeros((16, d), jnp.float32).at[:bsz].set(c).at[8].set(c_ctx)
    mod = _ada(cc, w_ada, b_ada)
    w_in_p = _prep_w_in(w_in)
    w_out_p = w_out.astype(jnp.bfloat16)
    wuk, wuv = d_w_uk.astype(jnp.bfloat16), d_w_uv.astype(jnp.bfloat16)
    tabs_h = _rope_tables(n_lat, HEAD_DIM)
    tabs_r = _rope_tables(n_lat, MLA_ROPE)
    na_bias = _na_bias_tables(b_rpb, n_lat)

    hc = ctx.astype(x.dtype)
    for l in range(depth):
        with_ctx = l < depth - 1
        ng = norm_g[l][None, :]
        d_gains = _dup_rope_gain(d_q_g[l]) + _dup_rope_gain(d_k_g[l])
        kvg = d_kv_g[l].reshape(4, LANE)
        proj_x = _inproj(x, mod[l], ng, w_in_p, l, mod_row=None, tn=TN_IN_LATENT)
        if with_ctx:
            proj_c, cmap = _inproj(hc, mod[l], ng, w_in_p, l, mod_row=8, tn=TN_IN_LATENT), (lambda ch: ch)
        else:
            proj_c = _inproj(hc, mod[l], ng, w_in_p, l, mod_row=8, tile_stride=2)
            cmap = lambda ch: (ch // (2 * CPT)) * CPT + ch % CPT
        oa = _attn_gqa(proj_x, proj_c, cmap, tabs_h, a_q_g[l][None, :], a_k_g[l][None, :], None,
                       (A_Q, A_K, A_V, A_G))
        ob = _attn_b(proj_x, proj_c, cmap, na_bias, l, b_q_g[l][None, :], b_k_g[l][None, :])
        oc = _attn_gqa(proj_x, proj_c, cmap, tabs_h, c_q_g[l][None, :], c_k_g[l][None, :], c_sink[l],
                       (C_Q, C_K, C_V, C_G))
        od = _attn_d(proj_x, proj_c, cmap, tabs_r, d_gains + (kvg,), wuk, wuv, l)
        if with_ctx:
            o_ctx = _attn_ctx(proj_c, c_sink[l], jnp.stack([a_q_g[l], b_q_g[l], c_q_g[l]]),
                              jnp.stack([a_k_g[l], b_k_g[l], c_k_g[l]]), jnp.concatenate(d_gains, axis=0),
                              kvg, wuk, wuv, l)
            hc = _outproj(hc, (o_ctx,) * 4, w_out_p, l, mod[l], mod_row=8, tm=ctx_len)
        x = _outproj(x, (oa, ob, oc, od), w_out_p, l, mod[l], mod_row=None, tm=TM_OUT)
    return x
```
